```python
import jax, jax.numpy as jnp
from jax import lax
import numpy as np

D_MODEL = 1024
BATCH = 16
SEQ = 4096
DEPTH = 2
DEC_BATCH = 16
DEC_SEQ = 16
PAST_LEN = 2048

CHUNK = 64
Q_BLOCK = 128
EPS = 1e-6
ROPE_BASE = 10000.0

MLA_HEADS = 8
MLA_NOPE = 64
MLA_ROPE = 32
MLA_V = 64
MLA_Q_LORA = 256
MLA_KV_LORA = 128
MLA_SCALE = (MLA_NOPE + MLA_ROPE) ** -0.5
RET_HEADS = 4
RET_DK = 64
RET_DV = 64
RET_SCALE = RET_DK ** -0.5
SB_HEADS = 4
SB_HD = 64
SB_SCALE = SB_HD ** -0.5
MEM_TOKENS = 256
MEM_HEADS = 4
MEM_HD = 128
MEM_SCALE = MEM_HD ** -0.5
D_FF = 2816
CONV_W = 3

MLA_WIDTH = MLA_HEADS * MLA_V
RET_WIDTH = RET_HEADS * RET_DV
SB_WIDTH = SB_HEADS * SB_HD
MIX_WIDTH = MLA_WIDTH + RET_WIDTH + SB_WIDTH
IN_SIZES = (MLA_Q_LORA, MLA_KV_LORA, MLA_ROPE, RET_HEADS * RET_DK, RET_HEADS * RET_DK, RET_WIDTH, RET_WIDTH, SB_WIDTH, SB_WIDTH, SB_WIDTH)
IN_WIDTH = sum(IN_SIZES)
IN_SPLIT_POINTS = tuple(int(c) for c in np.cumsum(IN_SIZES)[:-1])

kernel_name = 'hybrid_mla_retention_stickbreak_streaming_step'


def rmsnorm(x, g):
    xf = x.astype(jnp.float32)
    y = xf * lax.rsqrt(jnp.mean(xf * xf, axis=-1, keepdims=True) + EPS)
    return (y * g.astype(jnp.float32)).astype(x.dtype)


def rope(x, pos):
    d = x.shape[-1]
    inv = ROPE_BASE ** (-jnp.arange(0, d, 2, dtype=jnp.float32) / d)
    ang = pos.astype(jnp.float32)[:, None] * inv[None, :]
    cos = jnp.cos(ang)[:, None, :]
    sin = jnp.sin(ang)[:, None, :]
    xf = x.astype(jnp.float32)
    x1, x2 = xf[..., : d // 2], xf[..., d // 2:]
    return jnp.concatenate([x1 * cos - x2 * sin, x1 * sin + x2 * cos], axis=-1).astype(x.dtype)


def head_norm(o, g):
    of = o.astype(jnp.float32)
    mu = jnp.mean(of, axis=-1, keepdims=True)
    var = jnp.mean(jnp.square(of - mu), axis=-1, keepdims=True)
    y = ((of - mu) * lax.rsqrt(var + EPS)).reshape(o.shape[0], o.shape[1], -1)
    return y * g.astype(jnp.float32)


def sweep_queries(attend, qs, q_pos):
    T = q_pos.shape[0]
    if T <= Q_BLOCK:
        return attend(*qs, q_pos)
    nb = T // Q_BLOCK
    blocks = tuple(jnp.moveaxis(a.reshape(a.shape[0], nb, Q_BLOCK, *a.shape[2:]), 1, 0) for a in qs)
    out = lax.map(lambda blk: attend(*blk[:-1], blk[-1]), blocks + (q_pos.reshape(nb, Q_BLOCK),))
    out = jnp.moveaxis(out, 0, 1)
    return out.reshape(out.shape[0], T, *out.shape[3:])


def mla_attend(q_nope, q_pe, k_nope, k_pe, v, q_pos, k_pos):
    s = (jnp.einsum('bqhd,bkhd->bhqk', q_nope, k_nope).astype(jnp.float32)
         + jnp.einsum('bqhr,bkr->bhqk', q_pe, k_pe).astype(jnp.float32)) * MLA_SCALE
    visible = (k_pos[None, :] // CHUNK) <= (q_pos[:, None] // CHUNK)
    s = jnp.where(visible[None, None], s, -jnp.inf)
    p = jax.nn.softmax(s, axis=-1).astype(v.dtype)
    return jnp.einsum('bhqk,bkhd->bqhd', p, v)


def stick_breaking_attend(q, k, v, q_pos, k_pos):
    z = jnp.einsum('bqhd,bkhd->bhqk', q, k).astype(jnp.float32) * SB_SCALE
    before = (k_pos[None, :] < q_pos[:, None])[None, None]
    log_beta = jax.nn.log_sigmoid(z)
    log_keep = jnp.where(before, jax.nn.log_sigmoid(-z), 0.0)
    between = lax.cumsum(log_keep, axis=3, reverse=True) - log_keep
    a = jnp.where(before, jnp.exp(log_beta + between), 0.0)
    return jnp.einsum('bhqk,bkhd->bqhd', a.astype(v.dtype), v)


def ret_log_gamma():
    return jnp.log(1.0 - 2.0 ** (-5.0 - jnp.arange(RET_HEADS, dtype=jnp.float32)))


def retention_chunkwise(q, k, v, log_gamma, s0):
    L = q.shape[2]
    qf, kf, vf = q.astype(jnp.float32), k.astype(jnp.float32), v.astype(jnp.float32)
    idx = jnp.arange(L, dtype=jnp.float32)
    diff = idx[:, None] - idx[None, :]
    decay_mask = jnp.where(diff[None] >= 0, jnp.exp(jnp.maximum(diff, 0.0)[None] * log_gamma[:, None, None]), 0.0)
    scores = jnp.einsum('bclhd,bcshd->bchls', qf, kf) * decay_mask[None, None]
    o_inner = jnp.einsum('bchls,bcshe->bclhe', scores, vf)
    decay_k = jnp.exp((L - 1.0 - idx)[:, None] * log_gamma[None, :])
    kv_chunk = jnp.einsum('bclhd,lh,bclhe->bchde', kf, decay_k, vf)
    chunk_decay = jnp.exp(L * log_gamma)[None, :, None, None]

    def step(s, kv_c):
        return chunk_decay * s + kv_c, s

    s_fin, s_prev = lax.scan(step, s0, jnp.moveaxis(kv_chunk, 1, 0))
    s_prev = jnp.moveaxis(s_prev, 0, 1)
    decay_q = jnp.exp((idx + 1.0)[:, None] * log_gamma[None, :])
    o_cross = jnp.einsum('bclhd,lh,bchde->bclhe', qf, decay_q, s_prev)
    return o_inner + o_cross, s_fin


def memory_kv(mem, g, w_k, w_v):
    m = rmsnorm(mem, g)
    B, M, _ = m.shape
    return (m @ w_k).reshape(B, M, MEM_HEADS, MEM_HD), (m @ w_v).reshape(B, M, MEM_HEADS, MEM_HD)


def memory_attend(x, mem_k, mem_v, g, w_q, w_o):
    B, T, _ = x.shape
    q = (rmsnorm(x, g) @ w_q).reshape(B, T, MEM_HEADS, MEM_HD)
    s = jnp.einsum('bqhd,bkhd->bhqk', q, mem_k.astype(q.dtype)).astype(jnp.float32) * MEM_SCALE
    p = jax.nn.softmax(s, axis=-1).astype(x.dtype)
    o = jnp.einsum('bhqk,bkhd->bqhd', p, mem_v.astype(x.dtype)).reshape(B, T, MEM_HEADS * MEM_HD)
    return o @ w_o


def conv_ffn(x, conv_prev, g, w_up, w_conv, b_conv, w_down):
    T = x.shape[1]
    u = rmsnorm(x, g) @ w_up
    ext = jnp.concatenate([conv_prev.astype(u.dtype), u], axis=1)
    c = b_conv
    for j in range(CONV_W):
        c = c + ext[:, j:j + T] * w_conv[j]
    a, b = jnp.split(c, 2, axis=-1)
    return (jax.nn.silu(a) * b) @ w_down, ext[:, T:]


def trunk_layer(x, pos, past, mem_k, mem_v, p):
    B, T, _ = x.shape
    h = rmsnorm(x, p['g_mix'])
    z = h @ p['w_in']
    cq, ckv, kpe, rq, rk, rv, rg, sq, sk, sv = jnp.split(z, IN_SPLIT_POINTS, axis=-1)
    q = (rmsnorm(cq, p['g_q_lora']) @ p['w_q_up']).reshape(B, T, MLA_HEADS, MLA_NOPE + MLA_ROPE)
    q_nope = q[..., :MLA_NOPE]
    q_pe = rope(q[..., MLA_NOPE:], pos)
    ckv = rmsnorm(ckv, p['g_kv_lora'])
    kpe = rope(kpe[:, :, None, :], pos)[:, :, 0, :]
    rq = rope(rq.reshape(B, T, RET_HEADS, RET_DK), pos)
    rk = rope(rk.reshape(B, T, RET_HEADS, RET_DK), pos) * RET_SCALE
    rv = rv.reshape(B, T, RET_HEADS, RET_DV)
    sq = sq.reshape(B, T, SB_HEADS, SB_HD)
    sk = sk.reshape(B, T, SB_HEADS, SB_HD)
    sv = sv.reshape(B, T, SB_HEADS, SB_HD)
    if past is None:
        ckv_all, kpe_all, sk_all, sv_all = ckv, kpe, sk, sv
        s0 = jnp.zeros((B, RET_HEADS, RET_DK, RET_DV), jnp.float32)
        conv0 = jnp.zeros((B, CONV_W - 1, 2 * D_FF), x.dtype)
        n_chunks = T // CHUNK
    else:
        ckv_c, kpe_c, s0, sk_c, sv_c, conv0 = past
        ckv_all = jnp.concatenate([ckv_c.astype(x.dtype), ckv], axis=1)
        kpe_all = jnp.concatenate([kpe_c.astype(x.dtype), kpe], axis=1)
        sk_all = jnp.concatenate([sk_c.astype(x.dtype), sk], axis=1)
        sv_all = jnp.concatenate([sv_c.astype(x.dtype), sv], axis=1)
        n_chunks = 1
    k_pos = jnp.arange(ckv_all.shape[1])
    kv = (ckv_all @ p['w_kv_up']).reshape(B, -1, MLA_HEADS, MLA_NOPE + MLA_V)
    k_nope, v_mla = kv[..., :MLA_NOPE], kv[..., MLA_NOPE:]
    o_mla = sweep_queries(lambda a, b, qp: mla_attend(a, b, k_nope, kpe_all, v_mla, qp, k_pos), (q_nope, q_pe), pos)
    L = T // n_chunks

    def chunked(a):
        return a.reshape(B, n_chunks, L, *a.shape[2:])

    o_ret, s_new = retention_chunkwise(chunked(rq), chunked(rk), chunked(rv), ret_log_gamma(), s0.astype(jnp.float32))
    o_sb = sweep_queries(lambda a, qp: stick_breaking_attend(a, sk_all, sv_all, qp, k_pos), (sq,), pos)
    ret_y = head_norm(o_ret.reshape(B, T, RET_HEADS, RET_DV), p['g_ret_norm']).astype(x.dtype) * jax.nn.silu(rg)
    mixed = jnp.concatenate([o_mla.reshape(B, T, MLA_WIDTH), ret_y, o_sb.reshape(B, T, SB_WIDTH)], axis=-1)
    x = x + mixed @ p['w_out']
    x = x + memory_attend(x, mem_k, mem_v, p['g_mem_q'], p['w_mem_q'], p['w_mem_o'])
    y, conv_new = conv_ffn(x, conv0, p['g_ffn'], p['w_ffn_up'], p['w_ffn_conv'], p['b_ffn_conv'], p['w_ffn_down'])
    x = x + y
    return x, (ckv, kpe, s_new.astype(x.dtype), sk, sv, conv_new)


def setup_inputs(seed: int = 0) -> dict:
    key = jax.random.key(seed)
    ks = iter(jax.random.split(key, 40))

    def nrm(shape, scale=1.0):
        return jax.random.normal(next(ks), shape, jnp.float32) * scale

    def gain(shape):
        return 1.0 + nrm(shape, 0.02)

    f2 = 2 * D_FF
    mem_w = MEM_HEADS * MEM_HD
    return {
        'x_prompt': nrm((BATCH, SEQ, D_MODEL)),
        'x_sample': nrm((DEC_BATCH, DEC_SEQ, D_MODEL)),
        'cache_mla_ckv': nrm((DEPTH, DEC_BATCH, PAST_LEN, MLA_KV_LORA)),
        'cache_mla_kpe': nrm((DEPTH, DEC_BATCH, PAST_LEN, MLA_ROPE)),
        'state_ret': nrm((DEPTH, DEC_BATCH, RET_HEADS, RET_DK, RET_DV), 0.1),
        'cache_sb_k': nrm((DEPTH, DEC_BATCH, PAST_LEN, SB_HEADS, SB_HD)),
        'cache_sb_v': nrm((DEPTH, DEC_BATCH, PAST_LEN, SB_HEADS, SB_HD)),
        'cache_mem_k': nrm((DEPTH, DEC_BATCH, MEM_TOKENS, MEM_HEADS, MEM_HD)),
        'cache_mem_v': nrm((DEPTH, DEC_BATCH, MEM_TOKENS, MEM_HEADS, MEM_HD)),
        'state_ffn_conv': nrm((DEPTH, DEC_BATCH, CONV_W - 1, f2)),
        'mem_prompt': nrm((BATCH, MEM_TOKENS, D_MODEL)),
        'g_mix': gain((DEPTH, D_MODEL)),
        'w_in': nrm((DEPTH, D_MODEL, IN_WIDTH), D_MODEL ** -0.5),
        'g_q_lora': gain((DEPTH, MLA_Q_LORA)),
        'w_q_up': nrm((DEPTH, MLA_Q_LORA, MLA_HEADS * (MLA_NOPE + MLA_ROPE)), MLA_Q_LORA ** -0.5),
        'g_kv_lora': gain((DEPTH, MLA_KV_LORA)),
        'w_kv_up': nrm((DEPTH, MLA_KV_LORA, MLA_HEADS * (MLA_NOPE + MLA_V)), MLA_KV_LORA ** -0.5),
        'g_ret_norm': gain((DEPTH, RET_WIDTH)),
        'w_out': nrm((DEPTH, MIX_WIDTH, D_MODEL), MIX_WIDTH ** -0.5),
        'g_mem_q': gain((DEPTH, D_MODEL)),
        'g_mem_kv': gain((DEPTH, D_MODEL)),
        'w_mem_q': nrm((DEPTH, D_MODEL, mem_w), D_MODEL ** -0.5),
        'w_mem_k': nrm((DEPTH, D_MODEL, mem_w), D_MODEL ** -0.5),
        'w_mem_v': nrm((DEPTH, D_MODEL, mem_w), D_MODEL ** -0.5),
        'w_mem_o': nrm((DEPTH, mem_w, D_MODEL), mem_w ** -0.5),
        'g_ffn': gain((DEPTH, D_MODEL)),
        'w_ffn_up': nrm((DEPTH, D_MODEL, f2), D_MODEL ** -0.5),
        'w_ffn_conv': nrm((DEPTH, CONV_W, f2), CONV_W ** -0.5),
        'b_ffn_conv': nrm((DEPTH, f2), 0.01),
        'w_ffn_down': nrm((DEPTH, D_FF, D_MODEL), D_FF ** -0.5),
        'g_final': gain((D_MODEL,)),
    }


def reference(x_prompt, x_sample, cache_mla_ckv, cache_mla_kpe, state_ret, cache_sb_k, cache_sb_v,
              cache_mem_k, cache_mem_v, state_ffn_conv, mem_prompt,
              g_mix, w_in, g_q_lora, w_q_up, g_kv_lora, w_kv_up, g_ret_norm, w_out,
              g_mem_q, g_mem_kv, w_mem_q, w_mem_k, w_mem_v, w_mem_o,
              g_ffn, w_ffn_up, w_ffn_conv, b_ffn_conv, w_ffn_down, g_final):
    params = [dict(g_mix=g_mix[l], w_in=w_in[l], g_q_lora=g_q_lora[l], w_q_up=w_q_up[l],
                   g_kv_lora=g_kv_lora[l], w_kv_up=w_kv_up[l], g_ret_norm=g_ret_norm[l], w_out=w_out[l],
                   g_mem_q=g_mem_q[l], w_mem_q=w_mem_q[l], w_mem_o=w_mem_o[l],
                   g_ffn=g_ffn[l], w_ffn_up=w_ffn_up[l], w_ffn_conv=w_ffn_conv[l],
                   b_ffn_conv=b_ffn_conv[l], w_ffn_down=w_ffn_down[l]) for l in range(DEPTH)]

    xp = x_prompt
    pos_p = jnp.arange(xp.shape[1])
    p_states, p_mem_k, p_mem_v = [], [], []
    for l in range(DEPTH):
        mk, mv = memory_kv(mem_prompt, g_mem_kv[l], w_mem_k[l], w_mem_v[l])
        xp, st = trunk_layer(xp, pos_p, None, mk, mv, params[l])
        p_states.append(st)
        p_mem_k.append(mk)
        p_mem_v.append(mv)
    y_prompt = rmsnorm(xp, g_final)

    xs = x_sample
    past_len = cache_mla_ckv.shape[2]
    pos_s = past_len + jnp.arange(xs.shape[1])
    s_states = []
    for l in range(DEPTH):
        past = (cache_mla_ckv[l], cache_mla_kpe[l], state_ret[l], cache_sb_k[l], cache_sb_v[l], state_ffn_conv[l])
        xs, st = trunk_layer(xs, pos_s, past, cache_mem_k[l], cache_mem_v[l], params[l])
        s_states.append(st)
    y_sample = rmsnorm(xs, g_final)

    p_ckv, p_kpe, p_ret, p_sbk, p_sbv, p_conv = [jnp.stack(f) for f in zip(*p_states)]
    s_ckv, s_kpe, s_ret, s_sbk, s_sbv, s_conv = [jnp.stack(f) for f in zip(*s_states)]
    p_memk = jnp.stack(p_mem_k)
    p_memv = jnp.stack(p_mem_v)
    return (y_prompt, y_sample, p_ckv, p_kpe, p_ret, p_sbk, p_sbv, p_memk, p_memv, p_conv,
            s_ckv, s_kpe, s_ret, s_sbk, s_sbv, s_conv)
```

```python
import functools

import jax
import jax.numpy as jnp
from jax import lax
from jax.experimental import pallas as pl
from jax.experimental.pallas import tpu as pltpu

F32 = jnp.float32
BF16 = jnp.bfloat16

CHUNK = 64
EPS = 1e-6
ROPE_BASE = 10000.0

MLA_HEADS = 8
MLA_NOPE = 64
MLA_ROPE = 32
MLA_V = 64
MLA_Q_LORA = 256
MLA_KV_LORA = 128
MLA_SCALE = (MLA_NOPE + MLA_ROPE) ** -0.5
RET_HEADS = 4
RET_DK = 64
RET_DV = 64
RET_SCALE = RET_DK ** -0.5
SB_HEADS = 4
SB_HD = 64
SB_SCALE = SB_HD ** -0.5
MEM_HEADS = 4
MEM_HD = 128
MEM_SCALE = MEM_HD ** -0.5
D_FF = 2816
CONV_W = 3

IN_SIZES = (MLA_Q_LORA, MLA_KV_LORA, MLA_ROPE, RET_HEADS * RET_DK, RET_HEADS * RET_DK,
            RET_HEADS * RET_DV, RET_HEADS * RET_DV, SB_HEADS * SB_HD, SB_HEADS * SB_HD, SB_HEADS * SB_HD)

LANES = 128
HALF = LANES // 2
VMEM_LIMIT = 56 * 1024 * 1024

TM_PROJ = 512
TM_KVUP = 512
TQ_MLA = 512
TK_MLA = 512
TQ_SB = 512
TK_SB = 256
TC_RET = 512
L_RET = 64
TM_OUT = 512
TM_FFN = 512
TF_FFN = 256


def _tile(n, target):
    t = min(n, target)
    while n % t:
        t -= 1
    return t


def _cparams(n_axes):
    return pltpu.CompilerParams(dimension_semantics=("arbitrary",) * n_axes, vmem_limit_bytes=VMEM_LIMIT)


def _rms(x, g):
    return x * lax.rsqrt(jnp.mean(x * x, axis=-1, keepdims=True) + EPS) * g


def _full(shape):
    n = len(shape)
    return pl.BlockSpec(shape, lambda *_: (0,) * n)


def _dot(a, b):
    return jnp.dot(a, b, preferred_element_type=F32)


def _dot_nt(a, b):
    return lax.dot_general(a, b, (((1,), (1,)), ((), ())), preferred_element_type=F32)


_C_CQ = 0
_C_CKV = 256
_C_KPE = 384
_C_KPE_SW = 512
_C_RQ = 640
_C_RQ_SW = 896
_C_RK = 1152
_C_RK_SW = 1408
_C_RV = 1664
_C_RG = 1920
_C_SQ = 2176
_C_SK = 2432
_C_SV = 2688
_W1_COLS = 2944


def _in_proj_body(x_ref, gmix_ref, w1_ref, gq_ref, wq_ref, gkv_ref,
                  cslot_ref, sslot_ref, ck_ref, sk_tab_ref, cr_ref, sr_ref,
                  ckv_ref, kpe_ref, kpeb_ref, qcat_ref, rq_ref, rk_ref, rv_ref, rg_ref,
                  sq_ref, sk_ref, sv_ref, skb_ref, svb_ref):
    h = _rms(x_ref[0], gmix_ref[...]).astype(BF16)

    def proj(c0, width):
        return _dot(h, w1_ref[:, c0:c0 + width])

    cqn = _rms(proj(_C_CQ, 256), gq_ref[...]).astype(BF16)
    cslot = cslot_ref[...]
    sslot = sslot_ref[...]
    nslot = MLA_HEADS * LANES
    for p in range(MLA_HEADS // 2):
        q2 = _dot(cqn, wq_ref[:, 2 * LANES * p:2 * LANES * (p + 1)])
        q2s = _dot(cqn, wq_ref[:, nslot + 2 * LANES * p:nslot + 2 * LANES * (p + 1)])
        for a in range(2):
            sl = slice(a * LANES, (a + 1) * LANES)
            qcat_ref[0, 2 * p + a] = (q2[:, sl] * cslot + q2s[:, sl] * sslot).astype(BF16)

    ckv_ref[0] = _rms(proj(_C_CKV, 128), gkv_ref[...])
    kpe = proj(_C_KPE, 128) * ck_ref[...] + proj(_C_KPE_SW, 128) * sk_tab_ref[...]
    kpe_ref[0] = kpe[:, :MLA_ROPE]
    kpeb_ref[0] = kpe.astype(BF16)

    cr = cr_ref[...]
    sr = sr_ref[...]
    rq = proj(_C_RQ, 256) * cr + proj(_C_RQ_SW, 256) * sr
    rk = (proj(_C_RK, 256) * cr + proj(_C_RK_SW, 256) * sr) * RET_SCALE
    rv = proj(_C_RV, 256)
    rg = proj(_C_RG, 256)
    sq = proj(_C_SQ, 256) * SB_SCALE
    sk = proj(_C_SK, 256)
    sv = proj(_C_SV, 256)
    sk_ref[0] = sk
    sv_ref[0] = sv
    for p in range(2):
        sl = slice(p * LANES, (p + 1) * LANES)
        rq_ref[0, p] = rq[:, sl].astype(BF16)
        rk_ref[0, p] = rk[:, sl].astype(BF16)
        rv_ref[0, p] = rv[:, sl].astype(BF16)
        rg_ref[0, p] = rg[:, sl]
        sq_ref[0, p] = sq[:, sl].astype(BF16)
        skb_ref[0, p] = sk[:, sl].astype(BF16)
        svb_ref[0, p] = sv[:, sl].astype(BF16)


def _in_proj(x, tabs, lw):
    B, T, D = x.shape
    tm = _tile(T, TM_PROJ)
    grid = (B, T // tm)
    row = lambda w: pl.BlockSpec((tm, w), lambda b, t: (t, 0))
    pair = lambda n: pl.BlockSpec((1, n, tm, LANES), lambda b, t: (b, 0, t, 0))
    nat = lambda w: pl.BlockSpec((1, tm, w), lambda b, t: (b, t, 0))
    in_specs = [nat(D), _full((1, D)), _full((D, _W1_COLS)), _full((1, MLA_Q_LORA)),
                _full((MLA_Q_LORA, 2 * MLA_HEADS * LANES)), _full((1, MLA_KV_LORA)),
                row(LANES), row(LANES), row(LANES), row(LANES), row(256), row(256)]
    out_shape = [
        jax.ShapeDtypeStruct((B, T, MLA_KV_LORA), F32),
        jax.ShapeDtypeStruct((B, T, MLA_ROPE), F32),
        jax.ShapeDtypeStruct((B, T, LANES), BF16),
        jax.ShapeDtypeStruct((B, MLA_HEADS, T, LANES), BF16),
        jax.ShapeDtypeStruct((B, 2, T, LANES), BF16),
        jax.ShapeDtypeStruct((B, 2, T, LANES), BF16),
        jax.ShapeDtypeStruct((B, 2, T, LANES), BF16),
        jax.ShapeDtypeStruct((B, 2, T, LANES), F32),
        jax.ShapeDtypeStruct((B, 2, T, LANES), BF16),
        jax.ShapeDtypeStruct((B, T, 256), F32),
        jax.ShapeDtypeStruct((B, T, 256), F32),
        jax.ShapeDtypeStruct((B, 2, T, LANES), BF16),
        jax.ShapeDtypeStruct((B, 2, T, LANES), BF16),
    ]
    out_specs = [nat(MLA_KV_LORA), nat(MLA_ROPE), nat(LANES), pair(MLA_HEADS), pair(2), pair(2), pair(2),
                 pair(2), pair(2), nat(256), nat(256), pair(2), pair(2)]
    return pl.pallas_call(
        _in_proj_body, grid=grid, in_specs=in_specs, out_specs=out_specs, out_shape=out_shape,
        compiler_params=_cparams(2), name="in_proj",
    )(x, lw["g_mix"], lw["w1"], lw["g_q"], lw["wq"], lw["g_kv"],
      tabs["cslot"], tabs["sslot"], tabs["ck"], tabs["sk"], tabs["cr"], tabs["sr"])


def _kv_up_body(ckv_ref, kpe_ref, wk_ref, we_ref, wv_ref, kcat_ref, v_ref):
    c = ckv_ref[0].astype(BF16)
    kp = kpe_ref[0]
    for p in range(MLA_HEADS // 2):
        cols = slice(2 * LANES * p, 2 * LANES * (p + 1))
        k2 = _dot(c, wk_ref[:, cols]) + _dot(kp, we_ref[:, cols])
        kcat_ref[0, 2 * p] = k2[:, :LANES].astype(BF16)
        kcat_ref[0, 2 * p + 1] = k2[:, LANES:].astype(BF16)
        v_ref[0, p] = _dot(c, wv_ref[:, LANES * p:LANES * (p + 1)]).astype(BF16)


def _kv_up(ckv, kpe_pad, lw):
    B, T, _ = ckv.shape
    tm = _tile(T, TM_KVUP)
    nat = lambda w: pl.BlockSpec((1, tm, w), lambda b, t: (b, t, 0))
    pair = lambda n: pl.BlockSpec((1, n, tm, LANES), lambda b, t: (b, 0, t, 0))
    return pl.pallas_call(
        _kv_up_body, grid=(B, T // tm),
        in_specs=[nat(MLA_KV_LORA), nat(LANES), _full((MLA_KV_LORA, MLA_HEADS * LANES)),
                  _full((LANES, MLA_HEADS * LANES)), _full((MLA_KV_LORA, MLA_HEADS * MLA_V))],
        out_specs=[pair(MLA_HEADS), pair(MLA_HEADS // 2)],
        out_shape=[jax.ShapeDtypeStruct((B, MLA_HEADS, T, LANES), BF16),
                   jax.ShapeDtypeStruct((B, MLA_HEADS // 2, T, LANES), BF16)],
        compiler_params=_cparams(2), name="kv_up",
    )(ckv, kpe_pad, lw["wk"], lw["we"], lw["wv"])


def _mla_last_block(i, *, q0, tq, tk, nk):
    last_q = q0 + i * tq + (tq - 1)
    last_key = (last_q // CHUNK) * CHUNK + (CHUNK - 1)
    return jnp.minimum(last_key // tk, nk - 1)


def _mla_body(q_ref, k_ref, v_ref, o_ref, m_ref, l_ref, acc_ref, *, q0, tq, tk, nk, kv_len):
    i = pl.program_id(1)
    j = pl.program_id(2)

    @pl.when(j == 0)
    def _():
        m_ref[...] = jnp.full(m_ref.shape, -jnp.inf, F32)
        l_ref[...] = jnp.zeros(l_ref.shape, F32)
        acc_ref[...] = jnp.zeros(acc_ref.shape, F32)

    @pl.when(j <= _mla_last_block(i, q0=q0, tq=tq, tk=tk, nk=nk))
    def _():
        qpos = q0 + i * tq + lax.broadcasted_iota(jnp.int32, (tq, 1), 0)
        limit = jnp.minimum((jnp.right_shift(qpos, 6) + 1) * CHUNK, kv_len)
        kpos = j * tk + lax.broadcasted_iota(jnp.int32, (tq, tk), 1)
        bias = jnp.where(kpos < limit, 0.0, -jnp.inf).astype(F32)
        lane = lax.broadcasted_iota(jnp.int32, (tq, LANES), 1)
        first = lane < HALF

        def pair_step(p, carry):
            vp = v_ref[0, p]
            alphas, pvs = [], []
            for a in range(2):
                hh = 2 * p + a
                s = _dot_nt(q_ref[0, hh], k_ref[0, hh]) * MLA_SCALE + bias
                m_prev = m_ref[hh]
                m_new = jnp.maximum(m_prev, jnp.max(s, axis=1, keepdims=True))
                e = jnp.exp(s - m_new)
                alpha = jnp.exp(m_prev - m_new)
                l_ref[hh] = alpha * l_ref[hh] + jnp.sum(e, axis=1, keepdims=True)
                m_ref[hh] = m_new
                alphas.append(alpha)
                pvs.append(_dot(e.astype(BF16), vp))
            acc_ref[p] = acc_ref[p] * jnp.where(first, alphas[0], alphas[1]) + jnp.where(first, pvs[0], pvs[1])
            return carry

        lax.fori_loop(0, MLA_HEADS // 2, pair_step, 0)

    @pl.when(j == nk - 1)
    def _():
        lane = lax.broadcasted_iota(jnp.int32, (tq, LANES), 1)
        for p in range(MLA_HEADS // 2):
            inv = jnp.where(lane < HALF, 1.0 / l_ref[2 * p], 1.0 / l_ref[2 * p + 1])
            o_ref[0, p] = (acc_ref[p] * inv).astype(BF16)


def _mla_attention(qcat, kcat, v, *, q0, kv_len, tq, tk):
    B, H, Tq, _ = qcat.shape
    Tk = kcat.shape[2]
    nq, nk = Tq // tq, Tk // tk
    kw = dict(q0=q0, tq=tq, tk=tk, nk=nk)

    def kv_map(b, i, j):
        return (b, 0, jnp.minimum(j, _mla_last_block(i, **kw)), 0)

    return pl.pallas_call(
        functools.partial(_mla_body, kv_len=kv_len, **kw),
        grid=(B, nq, nk),
        in_specs=[pl.BlockSpec((1, H, tq, LANES), lambda b, i, j: (b, 0, i, 0)),
                  pl.BlockSpec((1, H, tk, LANES), kv_map),
                  pl.BlockSpec((1, H // 2, tk, LANES), kv_map)],
        out_specs=pl.BlockSpec((1, H // 2, tq, LANES), lambda b, i, j: (b, 0, i, 0)),
        out_shape=jax.ShapeDtypeStruct((B, H // 2, Tq, LANES), BF16),
        scratch_shapes=[pltpu.VMEM((H, tq, 1), F32), pltpu.VMEM((H, tq, 1), F32),
                        pltpu.VMEM((H // 2, tq, LANES), F32)],
        compiler_params=_cparams(3), name="mla_attn",
    )(qcat, kcat, v)


def _sb_last_block(i, *, q0, tq, tk, nk):
    last_key = q0 + i * tq + (tq - 1) - 1
    return jnp.clip(last_key // tk, 0, nk - 1)


def _sb_body(q_ref, k_ref, v_ref, u_ref, o_ref, run_ref, acc_ref, *, q0, tq, tk, nk):
    i = pl.program_id(1)
    j = pl.program_id(2)
    last = _sb_last_block(i, q0=q0, tq=tq, tk=tk, nk=nk)

    @pl.when(j == 0)
    def _():
        run_ref[...] = jnp.zeros(run_ref.shape, F32)
        acc_ref[...] = jnp.zeros(acc_ref.shape, F32)

    @pl.when(j <= last)
    def _():
        kb = last - j
        qpos = q0 + i * tq + lax.broadcasted_iota(jnp.int32, (tq, 1), 0)
        kpos = kb * tk + lax.broadcasted_iota(jnp.int32, (tq, tk), 1)
        before = kpos < qpos
        lane = lax.broadcasted_iota(jnp.int32, (tq, LANES), 1)
        first = lane < HALF
        tri = u_ref[...]

        def pair_step(p, carry):
            qp = q_ref[0, p]
            kp = k_ref[0, p]
            vp = v_ref[0, p]
            zero = jnp.zeros_like(qp)
            pvs = []
            for a in range(2):
                hh = 2 * p + a
                qa = jnp.where(first, qp, zero) if a == 0 else jnp.where(first, zero, qp)
                z = _dot_nt(qa, kp)
                sp = jnp.log(1.0 + jnp.exp(-jnp.abs(z)))
                log_beta = jnp.minimum(z, 0.0) - sp
                log_keep = jnp.where(before, jnp.minimum(-z, 0.0) - sp, 0.0)
                hi = log_keep.astype(BF16)
                lo = (log_keep - hi.astype(F32)).astype(BF16)
                between = _dot(hi, tri) + _dot(lo, tri)
                run = run_ref[hh]
                att = jnp.where(before, jnp.exp(log_beta + between + run), 0.0)
                run_ref[hh] = run + jnp.sum(log_keep, axis=1, keepdims=True)
                pvs.append(_dot(att.astype(BF16), vp))
            acc_ref[p] = acc_ref[p] + jnp.where(first, pvs[0], pvs[1])
            return carry

        lax.fori_loop(0, SB_HEADS // 2, pair_step, 0)

    @pl.when(j == nk - 1)
    def _():
        o_ref[0] = acc_ref[...].astype(BF16)


def _sb_attention(q, k, v, *, q0, tq, tk):
    B, P, Tq, _ = q.shape
    Tk = k.shape[2]
    nq, nk = Tq // tq, Tk // tk
    kw = dict(q0=q0, tq=tq, tk=tk, nk=nk)
    r = lax.broadcasted_iota(jnp.int32, (tk, tk), 0)
    c = lax.broadcasted_iota(jnp.int32, (tk, tk), 1)
    tri = (r > c).astype(BF16)

    def kv_map(b, i, j):
        return (b, 0, jnp.maximum(_sb_last_block(i, **kw) - j, 0), 0)

    return pl.pallas_call(
        functools.partial(_sb_body, **kw),
        grid=(B, nq, nk),
        in_specs=[pl.BlockSpec((1, P, tq, LANES), lambda b, i, j: (b, 0, i, 0)),
                  pl.BlockSpec((1, P, tk, LANES), kv_map),
                  pl.BlockSpec((1, P, tk, LANES), kv_map),
                  _full((tk, tk))],
        out_specs=pl.BlockSpec((1, P, tq, LANES), lambda b, i, j: (b, 0, i, 0)),
        out_shape=jax.ShapeDtypeStruct((B, P, Tq, LANES), BF16),
        scratch_shapes=[pltpu.VMEM((2 * P, tq, 1), F32), pltpu.VMEM((P, tq, LANES), F32)],
        compiler_params=_cparams(3), name="sb_attn",
    )(q, k, v, tri)


def _ret_body(q_ref, k_ref, v_ref, g_ref, s0_ref, dmask_ref, dq_ref, dk_ref, cd_ref, bd_ref, gn_ref,
              y_ref, snew_ref, state_ref, *, tc, lc, nc):
    c = pl.program_id(1)

    @pl.when(c == 0)
    def _():
        state_ref[...] = s0_ref[0]

    lane = lax.broadcasted_iota(jnp.int32, (lc, LANES), 1)
    first = lane < HALF
    for p in range(RET_HEADS // 2):
        for ch in range(tc // lc):
            rows = slice(ch * lc, (ch + 1) * lc)
            q = q_ref[0, p, rows, :]
            k = k_ref[0, p, rows, :]
            v = v_ref[0, p, rows, :]
            state = state_ref[p]
            cross = _dot((q.astype(F32) * dq_ref[p]).astype(BF16), state.astype(BF16))
            zero = jnp.zeros_like(q)
            inner = []
            for a in range(2):
                qa = jnp.where(first, q, zero) if a == 0 else jnp.where(first, zero, q)
                scores = _dot_nt(qa, k) * dmask_ref[2 * p + a]
                inner.append(_dot(scores.astype(BF16), v))
            o = jnp.where(first, inner[0], inner[1]) + cross
            kd = (k.astype(F32) * dk_ref[p]).astype(BF16)
            kv = lax.dot_general(kd, v, (((0,), (0,)), ((), ())), preferred_element_type=F32)
            state_ref[p] = cd_ref[p] * state + bd_ref[...] * kv
            zf = jnp.zeros_like(o)
            mu = jnp.where(first, jnp.sum(jnp.where(first, o, zf), axis=1, keepdims=True),
                           jnp.sum(jnp.where(first, zf, o), axis=1, keepdims=True)) * (1.0 / RET_DV)
            d = o - mu
            d2 = d * d
            var = jnp.where(first, jnp.sum(jnp.where(first, d2, zf), axis=1, keepdims=True),
                            jnp.sum(jnp.where(first, zf, d2), axis=1, keepdims=True)) * (1.0 / RET_DV)
            yn = d * lax.rsqrt(var + EPS) * gn_ref[p]
            gate = g_ref[0, p, rows, :]
            y_ref[0, p, rows, :] = (yn * (gate * jax.nn.sigmoid(gate))).astype(BF16)

    @pl.when(c == nc - 1)
    def _():
        snew_ref[0] = state_ref[...]


def _ret_tables(lc):
    log_gamma = jnp.log(1.0 - 2.0 ** (-5.0 - jnp.arange(RET_HEADS, dtype=F32)))
    idx = jnp.arange(lc, dtype=F32)
    diff = idx[:, None] - idx[None, :]
    dmask = jnp.where(diff[None] >= 0, jnp.exp(jnp.maximum(diff, 0.0)[None] * log_gamma[:, None, None]), 0.0)
    dk = jnp.exp((lc - 1.0 - idx)[:, None] * log_gamma[None, :])
    dq = jnp.exp((idx + 1.0)[:, None] * log_gamma[None, :])
    cd = jnp.exp(lc * log_gamma)

    def lanes(t):
        t = jnp.repeat(t[:, :, None], HALF, axis=2).reshape(t.shape[0], RET_HEADS // 2, LANES)
        return jnp.transpose(t, (1, 0, 2))

    blk = jnp.arange(LANES) // HALF
    bd = (blk[:, None] == blk[None, :]).astype(F32)
    cd_rows = jnp.repeat(cd, HALF).reshape(RET_HEADS // 2, LANES, 1)
    return dmask, lanes(dq), lanes(dk), cd_rows * bd[None], bd


def _state_to_pairs(s):
    B = s.shape[0]
    s = s.astype(F32).reshape(B, RET_HEADS // 2, 2, RET_DK, RET_DV)
    z = jnp.zeros_like(s[:, :, 0])
    top = jnp.concatenate([s[:, :, 0], z], axis=-1)
    bot = jnp.concatenate([z, s[:, :, 1]], axis=-1)
    return jnp.concatenate([top, bot], axis=-2)


def _pairs_to_state(sp):
    a = sp[:, :, :HALF, :HALF]
    b = sp[:, :, HALF:, HALF:]
    B = sp.shape[0]
    return jnp.stack([a, b], axis=2).reshape(B, RET_HEADS, RET_DK, RET_DV)


def _retention(rq, rk, rv, rg, s0_pairs, g_norm_pairs, *, lc):
    B, P, T, _ = rq.shape
    tc = _tile(T, TC_RET)
    nc = T // tc
    dmask, dq, dk, cd, bd = _ret_tables(lc)
    pair = pl.BlockSpec((1, P, tc, LANES), lambda b, c: (b, 0, c, 0))
    st = pl.BlockSpec((1, P, LANES, LANES), lambda b, c: (b, 0, 0, 0))
    return pl.pallas_call(
        functools.partial(_ret_body, tc=tc, lc=lc, nc=nc),
        grid=(B, nc),
        in_specs=[pair, pair, pair, pair, st, _full(dmask.shape), _full(dq.shape), _full(dk.shape),
                  _full(cd.shape), _full(bd.shape), _full(g_norm_pairs.shape)],
        out_specs=[pair, st],
        out_shape=[jax.ShapeDtypeStruct((B, P, T, LANES), BF16),
                   jax.ShapeDtypeStruct((B, P, LANES, LANES), F32)],
        scratch_shapes=[pltpu.VMEM((P, LANES, LANES), F32)],
        compiler_params=_cparams(2), name="retention",
    )(rq, rk, rv, rg, s0_pairs, dmask, dq, dk, cd, bd, g_norm_pairs)


def _out_mem_body(x_ref, omla_ref, ret_ref, osb_ref, wout_ref, gq_ref, wq_ref, mk_ref, mv_ref, wo_ref, y_ref):
    mixed = 0.0
    row0 = 0
    for ref, n in ((omla_ref, MLA_HEADS // 2), (ret_ref, RET_HEADS // 2), (osb_ref, SB_HEADS // 2)):
        for p in range(n):
            mixed = mixed + _dot(ref[0, p], wout_ref[row0:row0 + LANES, :])
            row0 += LANES
    x1 = x_ref[0] + mixed
    hq = _rms(x1, gq_ref[...]).astype(BF16)
    q = _dot(hq, wq_ref[...]).astype(BF16)
    heads = []
    for hh in range(MEM_HEADS):
        sl = slice(hh * MEM_HD, (hh + 1) * MEM_HD)
        s = _dot_nt(q[:, sl], mk_ref[0, :, sl]) * MEM_SCALE
        e = jnp.exp(s - jnp.max(s, axis=1, keepdims=True))
        prob = e / jnp.sum(e, axis=1, keepdims=True)
        heads.append(_dot(prob.astype(BF16), mv_ref[0, :, sl]).astype(BF16))
    o = jnp.concatenate(heads, axis=1)
    y_ref[0] = x1 + _dot(o, wo_ref[...])


def _out_mem(x, o_mla, ret_y, o_sb, mem_k, mem_v, lw):
    B, T, D = x.shape
    tm = _tile(T, TM_OUT)
    M = mem_k.shape[1]
    mw = MEM_HEADS * MEM_HD
    nat = pl.BlockSpec((1, tm, D), lambda b, t: (b, t, 0))
    pair = lambda n: pl.BlockSpec((1, n, tm, LANES), lambda b, t: (b, 0, t, 0))
    mem = pl.BlockSpec((1, M, mw), lambda b, t: (b, 0, 0))
    return pl.pallas_call(
        _out_mem_body, grid=(B, T // tm),
        in_specs=[nat, pair(MLA_HEADS // 2), pair(RET_HEADS // 2), pair(SB_HEADS // 2), _full((D, D)),
                  _full((1, D)), _full((D, mw)), mem, mem, _full((mw, D))],
        out_specs=nat, out_shape=jax.ShapeDtypeStruct((B, T, D), F32),
        compiler_params=_cparams(2), name="out_mem",
    )(x, o_mla, ret_y, o_sb, lw["w_out"], lw["g_mem_q"], lw["w_mem_q"], mem_k, mem_v, lw["w_mem_o"])


def _mem_kv_body(mem_ref, g_ref, wk_ref, wv_ref, k_ref, v_ref, kb_ref, vb_ref):
    m = _rms(mem_ref[0], g_ref[...]).astype(BF16)
    k = _dot(m, wk_ref[...])
    v = _dot(m, wv_ref[...])
    k_ref[0] = k
    v_ref[0] = v
    kb_ref[0] = k.astype(BF16)
    vb_ref[0] = v.astype(BF16)


def _mem_kv(mem, lw):
    B, M, D = mem.shape
    mw = MEM_HEADS * MEM_HD
    o = pl.BlockSpec((1, M, mw), lambda b: (b, 0, 0))
    return pl.pallas_call(
        _mem_kv_body, grid=(B,),
        in_specs=[pl.BlockSpec((1, M, D), lambda b: (b, 0, 0)), _full((1, D)), _full((D, mw)), _full((D, mw))],
        out_specs=[o, o, o, o],
        out_shape=[jax.ShapeDtypeStruct((B, M, mw), F32), jax.ShapeDtypeStruct((B, M, mw), F32),
                   jax.ShapeDtypeStruct((B, M, mw), BF16), jax.ShapeDtypeStruct((B, M, mw), BF16)],
        compiler_params=_cparams(1), name="mem_kv",
    )(mem, lw["g_mem_kv"], lw["w_mem_k"], lw["w_mem_v"])


_CARRY_ROW = 8 - (CONV_W - 1)


def _ffn_body(x_ref, g_ref, wup_ref, wconv_ref, bconv_ref, wdown_ref, conv0_ref, gfin_ref,
              y_ref, convnew_ref, carry_ref, exta_ref, extb_ref, act_ref, *, tm, nt, final):
    t = pl.program_id(1)

    @pl.when(t == 0)
    def _():
        carry_ref[_CARRY_ROW:8, :] = conv0_ref[0]

    x = x_ref[0]
    h = _rms(x, g_ref[...]).astype(BF16)

    def conv_part(c0, ext_ref):
        cols = slice(c0, c0 + TF_FFN)
        u = _dot(h, wup_ref[:, cols])
        ext_ref[_CARRY_ROW:8, :] = carry_ref[_CARRY_ROW:8, cols]
        ext_ref[8:8 + tm, :] = u
        carry_ref[_CARRY_ROW:8, cols] = u[tm - (CONV_W - 1):, :]
        w = wconv_ref[:, cols]
        c = bconv_ref[:, cols]
        c = c + ext_ref[_CARRY_ROW:_CARRY_ROW + tm, :] * w[0:1]
        c = c + ext_ref[_CARRY_ROW + 1:_CARRY_ROW + 1 + tm, :] * w[1:2]
        return c + u * w[2:3]

    for ci in range(D_FF // TF_FFN):
        c0 = ci * TF_FFN
        a = conv_part(c0, exta_ref)
        b = conv_part(D_FF + c0, extb_ref)
        act_ref[:, c0:c0 + TF_FFN] = (a * jax.nn.sigmoid(a) * b).astype(BF16)

    x3 = x + _dot(act_ref[...], wdown_ref[...])
    y_ref[0] = _rms(x3, gfin_ref[...]) if final else x3

    @pl.when(t == nt - 1)
    def _():
        convnew_ref[0] = carry_ref[_CARRY_ROW:8, :]


def _conv_ffn(x, conv0, lw, g_final, *, final):
    B, T, D = x.shape
    tm = _tile(T, TM_FFN)
    nt = T // tm
    f2 = 2 * D_FF
    nat = pl.BlockSpec((1, tm, D), lambda b, t: (b, t, 0))
    cv = pl.BlockSpec((1, CONV_W - 1, f2), lambda b, t: (b, 0, 0))
    return pl.pallas_call(
        functools.partial(_ffn_body, tm=tm, nt=nt, final=final),
        grid=(B, nt),
        in_specs=[nat, _full((1, D)), _full((D, f2)), _full((CONV_W, f2)), _full((1, f2)), _full((D_FF, D)),
                  cv, _full((1, D))],
        out_specs=[nat, cv],
        out_shape=[jax.ShapeDtypeStruct((B, T, D), F32), jax.ShapeDtypeStruct((B, CONV_W - 1, f2), F32)],
        scratch_shapes=[pltpu.VMEM((8, f2), F32), pltpu.VMEM((8 + tm, TF_FFN), F32),
                        pltpu.VMEM((8 + tm, TF_FFN), F32), pltpu.VMEM((tm, D_FF), BF16)],
        compiler_params=_cparams(2), name="conv_ffn",
    )(x, lw["g_ffn"], lw["w_ffn_up"], lw["w_ffn_conv"], lw["b_ffn_conv"], lw["w_ffn_down"], conv0, g_final)


def _rotate_half_cols(w, heads, d):
    k = w.shape[0]
    w = w.reshape(k, heads, 2, d // 2)
    return jnp.stack([-w[:, :, 1], w[:, :, 0]], axis=2).reshape(k, heads * d)


def _pad_cols(w, width):
    return jnp.pad(w, ((0, 0), (0, width - w.shape[1])))


def _prep_layer(l, g_mix, w_in, g_q_lora, w_q_up, g_kv_lora, w_kv_up, g_ret_norm, w_out,
                g_mem_q, g_mem_kv, w_mem_q, w_mem_k, w_mem_v, w_mem_o,
                g_ffn, w_ffn_up, w_ffn_conv, b_ffn_conv, w_ffn_down):
    w = w_in[l]
    parts, c0 = [], 0
    for n in IN_SIZES:
        parts.append(w[:, c0:c0 + n])
        c0 += n
    cq, ckv, kpe, rq, rk, rv, rg, sq, sk, sv = parts
    w1 = jnp.concatenate([
        cq, ckv, _pad_cols(kpe, LANES), _pad_cols(_rotate_half_cols(kpe, 1, MLA_ROPE), LANES),
        rq, _rotate_half_cols(rq, RET_HEADS, RET_DK), rk, _rotate_half_cols(rk, RET_HEADS, RET_DK),
        rv, rg, sq, sk, sv], axis=1).astype(BF16)
    kq = w_q_up.shape[1]
    wq3 = w_q_up[l].reshape(kq, MLA_HEADS, MLA_NOPE + MLA_ROPE)
    pe = wq3[:, :, MLA_NOPE:].reshape(kq, MLA_HEADS * MLA_ROPE)
    pe_sw = _rotate_half_cols(pe, MLA_HEADS, MLA_ROPE).reshape(kq, MLA_HEADS, MLA_ROPE)
    tail = LANES - MLA_NOPE - MLA_ROPE
    q_pad = jnp.pad(wq3, ((0, 0), (0, 0), (0, tail))).reshape(kq, MLA_HEADS * LANES)
    q_sw = jnp.pad(pe_sw, ((0, 0), (0, 0), (MLA_NOPE, tail))).reshape(kq, MLA_HEADS * LANES)
    wkv3 = w_kv_up[l].reshape(MLA_KV_LORA, MLA_HEADS, MLA_NOPE + MLA_V)
    wk = jnp.pad(wkv3[:, :, :MLA_NOPE], ((0, 0), (0, 0), (0, LANES - MLA_NOPE))).reshape(MLA_KV_LORA, MLA_HEADS * LANES)
    wv = wkv3[:, :, MLA_NOPE:].reshape(MLA_KV_LORA, MLA_HEADS * MLA_V)
    r = jnp.arange(LANES)[:, None]
    c = jnp.arange(MLA_HEADS * LANES)[None, :]
    we = ((r < MLA_ROPE) & (c % LANES == r + MLA_NOPE)).astype(BF16)
    row = lambda g: g[l].reshape(1, -1).astype(F32)
    return dict(
        g_mix=row(g_mix), w1=w1, g_q=row(g_q_lora), wq=jnp.concatenate([q_pad, q_sw], axis=1).astype(BF16),
        g_kv=row(g_kv_lora), wk=wk.astype(BF16), we=we, wv=wv.astype(BF16),
        g_ret=g_ret_norm[l].astype(F32).reshape(RET_HEADS // 2, 1, LANES),
        w_out=w_out[l].astype(BF16), g_mem_q=row(g_mem_q), g_mem_kv=row(g_mem_kv),
        w_mem_q=w_mem_q[l].astype(BF16), w_mem_k=w_mem_k[l].astype(BF16), w_mem_v=w_mem_v[l].astype(BF16),
        w_mem_o=w_mem_o[l].astype(BF16), g_ffn=row(g_ffn), w_ffn_up=w_ffn_up[l].astype(BF16),
        w_ffn_conv=w_ffn_conv[l].astype(F32), b_ffn_conv=row(b_ffn_conv), w_ffn_down=w_ffn_down[l].astype(BF16))


def _rope_tables(pos):
    t = pos.shape[0]
    posf = pos.astype(F32)[:, None]

    def cos_sin(d):
        inv = ROPE_BASE ** (-jnp.arange(0, d, 2, dtype=F32) / d)
        ang = posf * inv[None, :]
        return jnp.cos(ang), jnp.sin(ang)

    c16, s16 = cos_sin(MLA_ROPE)
    c32, s32 = cos_sin(RET_DK)
    z = lambda n: jnp.zeros((t, n), F32)
    tail = LANES - MLA_NOPE - MLA_ROPE
    return dict(
        cslot=jnp.concatenate([jnp.ones((t, MLA_NOPE), F32), c16, c16, z(tail)], axis=1),
        sslot=jnp.concatenate([z(MLA_NOPE), s16, s16, z(tail)], axis=1),
        ck=jnp.concatenate([c16, c16, z(LANES - MLA_ROPE)], axis=1),
        sk=jnp.concatenate([s16, s16, z(LANES - MLA_ROPE)], axis=1),
        cr=jnp.tile(jnp.concatenate([c32, c32], axis=1), (1, RET_HEADS)),
        sr=jnp.tile(jnp.concatenate([s32, s32], axis=1), (1, RET_HEADS)))


def _pad_rows(a, axis, n):
    pad = [(0, 0)] * a.ndim
    pad[axis] = (0, n - a.shape[axis])
    return jnp.pad(a, pad)


def _to_pairs(a):
    B, T, _ = a.shape
    return jnp.transpose(a.reshape(B, T, 2, LANES), (0, 2, 1, 3))


def _layer(x, tabs, lw, mem_k, mem_v, past, g_final, *, q0, final):
    B, T, _ = x.shape
    (ckv, kpe, kpe_b, qcat, rq, rk, rv, rg, sq, sk, sv, sk_b, sv_b) = _in_proj(x, tabs, lw)
    if past is None:
        ckv_all, kpe_all, sk_all, sv_all = ckv, kpe_b, sk_b, sv_b
        s0 = jnp.zeros((B, RET_HEADS // 2, LANES, LANES), F32)
        conv0 = jnp.zeros((B, CONV_W - 1, 2 * D_FF), F32)
        kv_len = T
        lc = _tile(T, L_RET)
        tq_mla, tk_mla = _tile(T, TQ_MLA), _tile(T, TK_MLA)
        tq_sb, tk_sb = _tile(T, TQ_SB), _tile(T, TK_SB)
    else:
        ckv_c, kpe_c, s0, sk_c, sv_c, conv0 = past
        P = ckv_c.shape[1]
        kv_len = P + T
        tk_sb = TK_SB
        tk_pad = -(-kv_len // tk_sb) * tk_sb
        ckv_all = _pad_rows(jnp.concatenate([ckv_c.astype(F32), ckv], axis=1), 1, tk_pad)
        kpe_cb = _pad_cols(kpe_c.reshape(B * P, MLA_ROPE), LANES).reshape(B, P, LANES).astype(BF16)
        kpe_all = _pad_rows(jnp.concatenate([kpe_cb, kpe_b], axis=1), 1, tk_pad)
        sk_all = _pad_rows(jnp.concatenate([_to_pairs(sk_c.reshape(B, P, -1).astype(BF16)), sk_b], axis=2), 2, tk_pad)
        sv_all = _pad_rows(jnp.concatenate([_to_pairs(sv_c.reshape(B, P, -1).astype(BF16)), sv_b], axis=2), 2, tk_pad)
        s0 = _state_to_pairs(s0)
        conv0 = conv0.astype(F32)
        lc = T
        tq_mla, tk_mla = T, tk_pad
        tq_sb = T
    kcat, vmla = _kv_up(ckv_all, kpe_all, lw)
    o_mla = _mla_attention(qcat, kcat, vmla, q0=q0, kv_len=kv_len, tq=tq_mla, tk=tk_mla)
    o_sb = _sb_attention(sq, sk_all, sv_all, q0=q0, tq=tq_sb, tk=tk_sb)
    ret_y, s_new = _retention(rq, rk, rv, rg, s0, lw["g_ret"], lc=lc)
    x = _out_mem(x, o_mla, ret_y, o_sb, mem_k, mem_v, lw)
    x, conv_new = _conv_ffn(x, conv0, lw, g_final, final=final)
    state = (ckv, kpe, _pairs_to_state(s_new), sk.reshape(B, T, SB_HEADS, SB_HD), sv.reshape(B, T, SB_HEADS, SB_HD),
             conv_new)
    return x, state


def kernel(x_prompt, x_sample, cache_mla_ckv, cache_mla_kpe, state_ret, cache_sb_k, cache_sb_v, cache_mem_k, cache_mem_v, state_ffn_conv, mem_prompt, g_mix, w_in, g_q_lora, w_q_up, g_kv_lora, w_kv_up, g_ret_norm, w_out, g_mem_q, g_mem_kv, w_mem_q, w_mem_k, w_mem_v, w_mem_o, g_ffn, w_ffn_up, w_ffn_conv, b_ffn_conv, w_ffn_down, g_final):
    depth = w_in.shape[0]
    layers = [_prep_layer(l, g_mix, w_in, g_q_lora, w_q_up, g_kv_lora, w_kv_up, g_ret_norm, w_out,
                          g_mem_q, g_mem_kv, w_mem_q, w_mem_k, w_mem_v, w_mem_o,
                          g_ffn, w_ffn_up, w_ffn_conv, b_ffn_conv, w_ffn_down) for l in range(depth)]
    gfin = g_final.reshape(1, -1).astype(F32)
    mw = MEM_HEADS * MEM_HD

    xp = x_prompt
    Bp, Tp, _ = xp.shape
    tabs_p = _rope_tables(jnp.arange(Tp))
    p_states, p_mem_k, p_mem_v = [], [], []
    for l in range(depth):
        mk, mv, mk_b, mv_b = _mem_kv(mem_prompt, layers[l])
        xp, st = _layer(xp, tabs_p, layers[l], mk_b, mv_b, None, gfin, q0=0, final=(l == depth - 1))
        p_states.append(st)
        p_mem_k.append(mk.reshape(Bp, -1, MEM_HEADS, MEM_HD))
        p_mem_v.append(mv.reshape(Bp, -1, MEM_HEADS, MEM_HD))

    xs = x_sample
    Bs, Ts, _ = xs.shape
    past_len = cache_mla_ckv.shape[2]
    tabs_s = _rope_tables(past_len + jnp.arange(Ts))
    s_states = []
    for l in range(depth):
        past = (cache_mla_ckv[l], cache_mla_kpe[l], state_ret[l], cache_sb_k[l], cache_sb_v[l], state_ffn_conv[l])
        mk_b = cache_mem_k[l].reshape(Bs, -1, mw).astype(BF16)
        mv_b = cache_mem_v[l].reshape(Bs, -1, mw).astype(BF16)
        xs, st = _layer(xs, tabs_s, layers[l], mk_b, mv_b, past, gfin, q0=past_len, final=(l == depth - 1))
        s_states.append(st)

    p_ckv, p_kpe, p_ret, p_sbk, p_sbv, p_conv = [jnp.stack(f) for f in zip(*p_states)]
    s_ckv, s_kpe, s_ret, s_sbk, s_sbv, s_conv = [jnp.stack(f) for f in zip(*s_states)]
    return (xp, xs, p_ckv, p_kpe, p_ret, p_sbk, p_sbv, jnp.stack(p_mem_k), jnp.stack(p_mem_v), p_conv,
            s_ckv, s_kpe, s_ret, s_sbk, s_sbv, s_conv)
```

```python
import functools

import jax
import jax.numpy as jnp
from jax import lax
from jax.experimental import pallas as pl
from jax.experimental.pallas import tpu as pltpu

F32 = jnp.float32
BF16 = jnp.bfloat16

CHUNK = 64
EPS = 1e-6
ROPE_BASE = 10000.0

MLA_HEADS = 8
MLA_NOPE = 64
MLA_ROPE = 32
MLA_V = 64
MLA_Q_LORA = 256
MLA_KV_LORA = 128
MLA_SCALE = (MLA_NOPE + MLA_ROPE) ** -0.5
_LOG2E = 1.4426950408889634
_MLA_EXP2_SCALE = MLA_SCALE * _LOG2E
RET_HEADS = 4
RET_DK = 64
RET_DV = 64
RET_SCALE = RET_DK ** -0.5
SB_HEADS = 4
SB_HD = 64
SB_SCALE = SB_HD ** -0.5
MEM_HEADS = 4
MEM_HD = 128
MEM_SCALE = MEM_HD ** -0.5
D_FF = 2816
CONV_W = 3

IN_SIZES = (MLA_Q_LORA, MLA_KV_LORA, MLA_ROPE, RET_HEADS * RET_DK, RET_HEADS * RET_DK,
            RET_HEADS * RET_DV, RET_HEADS * RET_DV, SB_HEADS * SB_HD, SB_HEADS * SB_HD, SB_HEADS * SB_HD)

LANES = 128
HALF = LANES // 2
VMEM_LIMIT = 56 * 1024 * 1024

TM_PROJ = 512
TM_KVUP = 512
TQ_MLA = 512
TK_MLA = 512
TQ_SB = 512
TK_SB = 256
TC_RET = 512
L_RET = 256
TM_OUT = 512
TM_FFN = 512
TF_FFN = 256


def _tile(n, target):
    t = min(n, target)
    while n % t:
        t -= 1
    return t


def _cparams(n_axes):
    return pltpu.CompilerParams(dimension_semantics=("arbitrary",) * n_axes, vmem_limit_bytes=VMEM_LIMIT)


def _rms(x, g):
    return x * lax.rsqrt(jnp.mean(x * x, axis=-1, keepdims=True) + EPS) * g


def _full(shape):
    n = len(shape)
    return pl.BlockSpec(shape, lambda *_: (0,) * n)


def _dot(a, b):
    return jnp.dot(a, b, preferred_element_type=F32)


def _dot_nt(a, b):
    return lax.dot_general(a, b, (((1,), (1,)), ((), ())), preferred_element_type=F32)


_C_CQ = 0
_C_CKV = 256
_C_KPE = 384
_C_KPE_SW = 512
_C_RQ = 640
_C_RQ_SW = 896
_C_RK = 1152
_C_RK_SW = 1408
_C_RV = 1664
_C_RG = 1920
_C_SQ = 2176
_C_SK = 2432
_C_SV = 2688
_W1_COLS = 2944


def _in_proj_body(x_ref, gmix_ref, w1_ref, gq_ref, wq_ref, gkv_ref,
                  cslot_ref, sslot_ref, ck_ref, sk_tab_ref, cr_ref, sr_ref,
                  ckv_ref, kpe_ref, kpeb_ref, qcat_ref, rq_ref, rk_ref, rv_ref, rg_ref,
                  sq_ref, sk_ref, sv_ref, skb_ref, svb_ref):
    h = _rms(x_ref[0], gmix_ref[...]).astype(BF16)

    def proj(c0, width):
        return _dot(h, w1_ref[:, c0:c0 + width])

    cqn = _rms(proj(_C_CQ, 256), gq_ref[...]).astype(BF16)
    cslot = cslot_ref[...]
    sslot = sslot_ref[...]
    nslot = MLA_HEADS * LANES
    for p in range(MLA_HEADS // 2):
        q2 = _dot(cqn, wq_ref[:, 2 * LANES * p:2 * LANES * (p + 1)])
        q2s = _dot(cqn, wq_ref[:, nslot + 2 * LANES * p:nslot + 2 * LANES * (p + 1)])
        for a in range(2):
            sl = slice(a * LANES, (a + 1) * LANES)
            qcat_ref[0, 2 * p + a] = (q2[:, sl] * cslot + q2s[:, sl] * sslot).astype(BF16)

    ckv_ref[0] = _rms(proj(_C_CKV, 128), gkv_ref[...])
    kpe = proj(_C_KPE, 128) * ck_ref[...] + proj(_C_KPE_SW, 128) * sk_tab_ref[...]
    kpe_ref[0] = kpe[:, :MLA_ROPE]
    kpeb_ref[0] = kpe.astype(BF16)

    cr = cr_ref[...]
    sr = sr_ref[...]
    rq = proj(_C_RQ, 256) * cr + proj(_C_RQ_SW, 256) * sr
    rk = (proj(_C_RK, 256) * cr + proj(_C_RK_SW, 256) * sr) * RET_SCALE
    rv = proj(_C_RV, 256)
    rg = proj(_C_RG, 256)
    sq = proj(_C_SQ, 256) * SB_SCALE
    sk = proj(_C_SK, 256)
    sv = proj(_C_SV, 256)
    sk_ref[0] = sk
    sv_ref[0] = sv
    for p in range(2):
        sl = slice(p * LANES, (p + 1) * LANES)
        rq_ref[0, p] = rq[:, sl].astype(BF16)
        rk_ref[0, p] = rk[:, sl].astype(BF16)
        rv_ref[0, p] = rv[:, sl].astype(BF16)
        rg_ref[0, p] = rg[:, sl]
        sq_ref[0, p] = sq[:, sl].astype(BF16)
        skb_ref[0, p] = sk[:, sl].astype(BF16)
        svb_ref[0, p] = sv[:, sl].astype(BF16)


def _in_proj(x, tabs, lw):
    B, T, D = x.shape
    tm = _tile(T, TM_PROJ)
    grid = (B, T // tm)
    row = lambda w: pl.BlockSpec((tm, w), lambda b, t: (t, 0))
    pair = lambda n: pl.BlockSpec((1, n, tm, LANES), lambda b, t: (b, 0, t, 0))
    nat = lambda w: pl.BlockSpec((1, tm, w), lambda b, t: (b, t, 0))
    in_specs = [nat(D), _full((1, D)), _full((D, _W1_COLS)), _full((1, MLA_Q_LORA)),
                _full((MLA_Q_LORA, 2 * MLA_HEADS * LANES)), _full((1, MLA_KV_LORA)),
                row(LANES), row(LANES), row(LANES), row(LANES), row(256), row(256)]
    out_shape = [
        jax.ShapeDtypeStruct((B, T, MLA_KV_LORA), F32),
        jax.ShapeDtypeStruct((B, T, MLA_ROPE), F32),
        jax.ShapeDtypeStruct((B, T, LANES), BF16),
        jax.ShapeDtypeStruct((B, MLA_HEADS, T, LANES), BF16),
        jax.ShapeDtypeStruct((B, 2, T, LANES), BF16),
        jax.ShapeDtypeStruct((B, 2, T, LANES), BF16),
        jax.ShapeDtypeStruct((B, 2, T, LANES), BF16),
        jax.ShapeDtypeStruct((B, 2, T, LANES), F32),
        jax.ShapeDtypeStruct((B, 2, T, LANES), BF16),
        jax.ShapeDtypeStruct((B, T, 256), F32),
        jax.ShapeDtypeStruct((B, T, 256), F32),
        jax.ShapeDtypeStruct((B, 2, T, LANES), BF16),
        jax.ShapeDtypeStruct((B, 2, T, LANES), BF16),
    ]
    out_specs = [nat(MLA_KV_LORA), nat(MLA_ROPE), nat(LANES), pair(MLA_HEADS), pair(2), pair(2), pair(2),
                 pair(2), pair(2), nat(256), nat(256), pair(2), pair(2)]
    return pl.pallas_call(
        _in_proj_body, grid=grid, in_specs=in_specs, out_specs=out_specs, out_shape=out_shape,
        compiler_params=_cparams(2), name="in_proj",
    )(x, lw["g_mix"], lw["w1"], lw["g_q"], lw["wq"], lw["g_kv"],
      tabs["cslot"], tabs["sslot"], tabs["ck"], tabs["sk"], tabs["cr"], tabs["sr"])


def _kv_up_body(ckv_ref, kpe_ref, wk_ref, we_ref, wv_ref, kcat_ref, v_ref):
    c = ckv_ref[0].astype(BF16)
    kp = kpe_ref[0]
    for p in range(MLA_HEADS // 2):
        cols = slice(2 * LANES * p, 2 * LANES * (p + 1))
        k2 = _dot(c, wk_ref[:, cols]) + _dot(kp, we_ref[:, cols])
        kcat_ref[0, 2 * p] = k2[:, :LANES].astype(BF16)
        kcat_ref[0, 2 * p + 1] = k2[:, LANES:].astype(BF16)
        v_ref[0, p] = _dot(c, wv_ref[:, LANES * p:LANES * (p + 1)]).astype(BF16)


def _kv_up(ckv, kpe_pad, lw):
    B, T, _ = ckv.shape
    tm = _tile(T, TM_KVUP)
    nat = lambda w: pl.BlockSpec((1, tm, w), lambda b, t: (b, t, 0))
    pair = lambda n: pl.BlockSpec((1, n, tm, LANES), lambda b, t: (b, 0, t, 0))
    return pl.pallas_call(
        _kv_up_body, grid=(B, T // tm),
        in_specs=[nat(MLA_KV_LORA), nat(LANES), _full((MLA_KV_LORA, MLA_HEADS * LANES)),
                  _full((LANES, MLA_HEADS * LANES)), _full((MLA_KV_LORA, MLA_HEADS * MLA_V))],
        out_specs=[pair(MLA_HEADS), pair(MLA_HEADS // 2)],
        out_shape=[jax.ShapeDtypeStruct((B, MLA_HEADS, T, LANES), BF16),
                   jax.ShapeDtypeStruct((B, MLA_HEADS // 2, T, LANES), BF16)],
        compiler_params=_cparams(2), name="kv_up",
    )(ckv, kpe_pad, lw["wk"], lw["we"], lw["wv"])


def _mla_last_block(i, *, q0, tq, tk, nk):
    last_q = q0 + i * tq + (tq - 1)
    last_key = (last_q // CHUNK) * CHUNK + (CHUNK - 1)
    return jnp.minimum(last_key // tk, nk - 1)


def _mla_body(q_ref, k_ref, v_ref, o_ref, m_ref, l_ref, acc_ref, *, q0, tq, tk, nk, kv_len):
    i = pl.program_id(1)
    j = pl.program_id(2)

    @pl.when(j == 0)
    def _():
        m_ref[...] = jnp.full(m_ref.shape, -jnp.inf, F32)
        l_ref[...] = jnp.zeros(l_ref.shape, F32)
        acc_ref[...] = jnp.zeros(acc_ref.shape, F32)

    def step(masked):
        lane = lax.broadcasted_iota(jnp.int32, (tq, LANES), 1)
        first = lane < HALF
        if masked:
            qpos = q0 + i * tq + lax.broadcasted_iota(jnp.int32, (tq, 1), 0)
            limit = jnp.minimum((jnp.right_shift(qpos, 6) + 1) * CHUNK, kv_len)
            kpos = j * tk + lax.broadcasted_iota(jnp.int32, (tq, tk), 1)
            bias = jnp.where(kpos < limit, 0.0, -jnp.inf).astype(F32)

        for p in range(MLA_HEADS // 2):
            vp = v_ref[0, p]
            alphas, pvs = [], []
            for a in range(2):
                hh = 2 * p + a
                s = _dot_nt(q_ref[0, hh], k_ref[0, hh])
                if masked:
                    s = s + bias
                m_prev = m_ref[hh]
                m_new = jnp.maximum(m_prev, jnp.max(s, axis=1, keepdims=True))
                e = jnp.exp2((s - pltpu.repeat(m_new, tk // LANES, axis=1)) * _MLA_EXP2_SCALE)
                alpha = jnp.exp2((m_prev - m_new) * _MLA_EXP2_SCALE)
                l_ref[hh] = alpha * l_ref[hh] + jnp.sum(e, axis=1, keepdims=True)
                m_ref[hh] = m_new
                alphas.append(alpha)
                pvs.append(_dot(e.astype(BF16), vp))
            acc_ref[p] = acc_ref[p] * jnp.where(first, alphas[0], alphas[1]) + jnp.where(first, pvs[0], pvs[1])

    all_visible = jnp.minimum(((q0 + i * tq) // CHUNK + 1) * CHUNK, kv_len)
    fully_visible = (j + 1) * tk <= all_visible
    needed = j <= _mla_last_block(i, q0=q0, tq=tq, tk=tk, nk=nk)

    @pl.when(jnp.logical_and(needed, fully_visible))
    def _():
        step(False)

    @pl.when(jnp.logical_and(needed, jnp.logical_not(fully_visible)))
    def _():
        step(True)

    @pl.when(j == nk - 1)
    def _():
        lane = lax.broadcasted_iota(jnp.int32, (tq, LANES), 1)
        for p in range(MLA_HEADS // 2):
            inv = jnp.where(lane < HALF, 1.0 / l_ref[2 * p], 1.0 / l_ref[2 * p + 1])
            o_ref[0, p] = (acc_ref[p] * inv).astype(BF16)


def _mla_attention(qcat, kcat, v, *, q0, kv_len, tq, tk):
    B, H, Tq, _ = qcat.shape
    Tk = kcat.shape[2]
    nq, nk = Tq // tq, Tk // tk
    kw = dict(q0=q0, tq=tq, tk=tk, nk=nk)

    def kv_map(b, i, j):
        return (b, 0, jnp.minimum(j, _mla_last_block(i, **kw)), 0)

    return pl.pallas_call(
        functools.partial(_mla_body, kv_len=kv_len, **kw),
        grid=(B, nq, nk),
        in_specs=[pl.BlockSpec((1, H, tq, LANES), lambda b, i, j: (b, 0, i, 0)),
                  pl.BlockSpec((1, H, tk, LANES), kv_map),
                  pl.BlockSpec((1, H // 2, tk, LANES), kv_map)],
        out_specs=pl.BlockSpec((1, H // 2, tq, LANES), lambda b, i, j: (b, 0, i, 0)),
        out_shape=jax.ShapeDtypeStruct((B, H // 2, Tq, LANES), BF16),
        scratch_shapes=[pltpu.VMEM((H, tq, LANES), F32), pltpu.VMEM((H, tq, LANES), F32),
                        pltpu.VMEM((H // 2, tq, LANES), F32)],
        compiler_params=_cparams(3), name="mla_attn",
    )(qcat, kcat, v)


def _sb_last_block(i, *, q0, tq, tk, nk):
    last_key = q0 + i * tq + (tq - 1) - 1
    return jnp.clip(last_key // tk, 0, nk - 1)


def _sb_body(q_ref, k_ref, v_ref, u_ref, o_ref, run_ref, acc_ref, *, q0, tq, tk, nk):
    i = pl.program_id(1)
    j = pl.program_id(2)
    last = _sb_last_block(i, q0=q0, tq=tq, tk=tk, nk=nk)
    kb = last - j

    @pl.when(j == 0)
    def _():
        run_ref[...] = jnp.zeros(run_ref.shape, F32)
        acc_ref[...] = jnp.zeros(acc_ref.shape, F32)

    def step(masked):
        first_k = lax.broadcasted_iota(jnp.int32, (tk, LANES), 1) < HALF
        tri = u_ref[...]
        reps = tk // LANES
        if masked:
            qpos = q0 + i * tq + lax.broadcasted_iota(jnp.int32, (tq, 1), 0)
            kpos = kb * tk + lax.broadcasted_iota(jnp.int32, (tq, tk), 1)
            before1 = kpos < qpos
            before = jnp.concatenate([before1, before1], axis=1)

        for p in range(SB_HEADS // 2):
            kp = k_ref[0, p]
            vp = v_ref[0, p]
            zero = jnp.zeros_like(kp)
            k2 = jnp.concatenate([jnp.where(first_k, kp, zero), jnp.where(first_k, zero, kp)], axis=0)
            v2 = jnp.concatenate([jnp.where(first_k, vp, zero), jnp.where(first_k, zero, vp)], axis=0)
            z = _dot_nt(q_ref[0, p], k2) * _LOG2E
            nz = -z
            sp = jnp.log(1.0 + jnp.exp2(jnp.minimum(z, nz))) * _LOG2E
            log_keep = jnp.minimum(nz, 0.0) - sp
            log_beta = log_keep + z
            if masked:
                log_keep = jnp.where(before, log_keep, 0.0)
            hi = log_keep.astype(BF16)
            lo = (log_keep - hi.astype(F32)).astype(BF16)
            between = jnp.concatenate(
                [_dot(hi[:, :tk], tri) + _dot(lo[:, :tk], tri), _dot(hi[:, tk:], tri) + _dot(lo[:, tk:], tri)], axis=1)
            run_a = run_ref[2 * p]
            run_b = run_ref[2 * p + 1]
            run = jnp.concatenate([pltpu.repeat(run_a, reps, axis=1), pltpu.repeat(run_b, reps, axis=1)], axis=1)
            att = jnp.exp2(log_beta + between + run)
            if masked:
                att = jnp.where(before, att, 0.0)
            run_ref[2 * p] = run_a + jnp.sum(log_keep[:, :tk], axis=1, keepdims=True)
            run_ref[2 * p + 1] = run_b + jnp.sum(log_keep[:, tk:], axis=1, keepdims=True)
            acc_ref[p] = acc_ref[p] + _dot(att.astype(BF16), v2)

    unmasked = (kb + 1) * tk <= q0 + i * tq

    @pl.when(jnp.logical_and(j <= last, unmasked))
    def _():
        step(False)

    @pl.when(jnp.logical_and(j <= last, jnp.logical_not(unmasked)))
    def _():
        step(True)

    @pl.when(j == nk - 1)
    def _():
        o_ref[0] = acc_ref[...].astype(BF16)


def _sb_attention(q, k, v, *, q0, tq, tk):
    B, P, Tq, _ = q.shape
    Tk = k.shape[2]
    nq, nk = Tq // tq, Tk // tk
    kw = dict(q0=q0, tq=tq, tk=tk, nk=nk)
    r = lax.broadcasted_iota(jnp.int32, (tk, tk), 0)
    c = lax.broadcasted_iota(jnp.int32, (tk, tk), 1)
    tri = (r > c).astype(BF16)

    def kv_map(b, i, j):
        return (b, 0, jnp.maximum(_sb_last_block(i, **kw) - j, 0), 0)

    return pl.pallas_call(
        functools.partial(_sb_body, **kw),
        grid=(B, nq, nk),
        in_specs=[pl.BlockSpec((1, P, tq, LANES), lambda b, i, j: (b, 0, i, 0)),
                  pl.BlockSpec((1, P, tk, LANES), kv_map),
                  pl.BlockSpec((1, P, tk, LANES), kv_map),
                  _full((tk, tk))],
        out_specs=pl.BlockSpec((1, P, tq, LANES), lambda b, i, j: (b, 0, i, 0)),
        out_shape=jax.ShapeDtypeStruct((B, P, Tq, LANES), BF16),
        scratch_shapes=[pltpu.VMEM((2 * P, tq, LANES), F32), pltpu.VMEM((P, tq, LANES), F32)],
        compiler_params=_cparams(3), name="sb_attn",
    )(q, k, v, tri)


def _ret_body(q_ref, k_ref, v_ref, g_ref, s0_ref, dmask_ref, dq_ref, dk_ref, cd_ref, bd_ref, gn_ref,
              y_ref, snew_ref, state_ref, *, tc, lc, nc):
    c = pl.program_id(1)

    @pl.when(c == 0)
    def _():
        state_ref[...] = s0_ref[0]

    lane = lax.broadcasted_iota(jnp.int32, (lc, LANES), 1)
    first = lane < HALF
    for p in range(RET_HEADS // 2):
        for ch in range(tc // lc):
            rows = slice(ch * lc, (ch + 1) * lc)
            q = q_ref[0, p, rows, :]
            k = k_ref[0, p, rows, :]
            v = v_ref[0, p, rows, :]
            state = state_ref[p]
            cross = _dot((q.astype(F32) * dq_ref[p]).astype(BF16), state.astype(BF16))
            zero = jnp.zeros_like(q)
            inner = []
            for a in range(2):
                qa = jnp.where(first, q, zero) if a == 0 else jnp.where(first, zero, q)
                scores = _dot_nt(qa, k) * dmask_ref[2 * p + a]
                inner.append(_dot(scores.astype(BF16), v))
            o = jnp.where(first, inner[0], inner[1]) + cross
            kd = (k.astype(F32) * dk_ref[p]).astype(BF16)
            kv = lax.dot_general(kd, v, (((0,), (0,)), ((), ())), preferred_element_type=F32)
            state_ref[p] = cd_ref[p] * state + bd_ref[...] * kv
            zf = jnp.zeros_like(o)
            mu = jnp.where(first, jnp.sum(jnp.where(first, o, zf), axis=1, keepdims=True),
                           jnp.sum(jnp.where(first, zf, o), axis=1, keepdims=True)) * (1.0 / RET_DV)
            d = o - mu
            d2 = d * d
            var = jnp.where(first, jnp.sum(jnp.where(first, d2, zf), axis=1, keepdims=True),
                            jnp.sum(jnp.where(first, zf, d2), axis=1, keepdims=True)) * (1.0 / RET_DV)
            yn = d * lax.rsqrt(var + EPS) * gn_ref[p]
            gate = g_ref[0, p, rows, :]
            y_ref[0, p, rows, :] = (yn * (gate * jax.nn.sigmoid(gate))).astype(BF16)

    @pl.when(c == nc - 1)
    def _():
        snew_ref[0] = state_ref[...]


def _ret_tables(lc):
    log_gamma = jnp.log(1.0 - 2.0 ** (-5.0 - jnp.arange(RET_HEADS, dtype=F32)))
    idx = jnp.arange(lc, dtype=F32)
    diff = idx[:, None] - idx[None, :]
    dmask = jnp.where(diff[None] >= 0, jnp.exp(jnp.maximum(diff, 0.0)[None] * log_gamma[:, None, None]), 0.0)
    dk = jnp.exp((lc - 1.0 - idx)[:, None] * log_gamma[None, :])
    dq = jnp.exp((idx + 1.0)[:, None] * log_gamma[None, :])
    cd = jnp.exp(lc * log_gamma)

    def lanes(t):
        t = jnp.repeat(t[:, :, None], HALF, axis=2).reshape(t.shape[0], RET_HEADS // 2, LANES)
        return jnp.transpose(t, (1, 0, 2))

    blk = jnp.arange(LANES) // HALF
    bd = (blk[:, None] == blk[None, :]).astype(F32)
    cd_rows = jnp.repeat(cd, HALF).reshape(RET_HEADS // 2, LANES, 1)
    return dmask, lanes(dq), lanes(dk), cd_rows * bd[None], bd


def _state_to_pairs(s):
    B = s.shape[0]
    s = s.astype(F32).reshape(B, RET_HEADS // 2, 2, RET_DK, RET_DV)
    z = jnp.zeros_like(s[:, :, 0])
    top = jnp.concatenate([s[:, :, 0], z], axis=-1)
    bot = jnp.concatenate([z, s[:, :, 1]], axis=-1)
    return jnp.concatenate([top, bot], axis=-2)


def _pairs_to_state(sp):
    a = sp[:, :, :HALF, :HALF]
    b = sp[:, :, HALF:, HALF:]
    B = sp.shape[0]
    return jnp.stack([a, b], axis=2).reshape(B, RET_HEADS, RET_DK, RET_DV)


def _retention(rq, rk, rv, rg, s0_pairs, g_norm_pairs, *, lc):
    B, P, T, _ = rq.shape
    tc = _tile(T, TC_RET)
    nc = T // tc
    dmask, dq, dk, cd, bd = _ret_tables(lc)
    pair = pl.BlockSpec((1, P, tc, LANES), lambda b, c: (b, 0, c, 0))
    st = pl.BlockSpec((1, P, LANES, LANES), lambda b, c: (b, 0, 0, 0))
    return pl.pallas_call(
        functools.partial(_ret_body, tc=tc, lc=lc, nc=nc),
        grid=(B, nc),
        in_specs=[pair, pair, pair, pair, st, _full(dmask.shape), _full(dq.shape), _full(dk.shape),
                  _full(cd.shape), _full(bd.shape), _full(g_norm_pairs.shape)],
        out_specs=[pair, st],
        out_shape=[jax.ShapeDtypeStruct((B, P, T, LANES), BF16),
                   jax.ShapeDtypeStruct((B, P, LANES, LANES), F32)],
        scratch_shapes=[pltpu.VMEM((P, LANES, LANES), F32)],
        compiler_params=_cparams(2), name="retention",
    )(rq, rk, rv, rg, s0_pairs, dmask, dq, dk, cd, bd, g_norm_pairs)


def _out_mem_body(x_ref, omla_ref, ret_ref, osb_ref, wout_ref, gq_ref, wq_ref, mk_ref, mv_ref, wo_ref, y_ref):
    mixed = 0.0
    row0 = 0
    for ref, n in ((omla_ref, MLA_HEADS // 2), (ret_ref, RET_HEADS // 2), (osb_ref, SB_HEADS // 2)):
        for p in range(n):
            mixed = mixed + _dot(ref[0, p], wout_ref[row0:row0 + LANES, :])
            row0 += LANES
    x1 = x_ref[0] + mixed
    hq = _rms(x1, gq_ref[...]).astype(BF16)
    q = _dot(hq, wq_ref[...]).astype(BF16)
    heads = []
    for hh in range(MEM_HEADS):
        sl = slice(hh * MEM_HD, (hh + 1) * MEM_HD)
        s = _dot_nt(q[:, sl], mk_ref[0, :, sl]) * MEM_SCALE
        e = jnp.exp(s - jnp.max(s, axis=1, keepdims=True))
        prob = e / jnp.sum(e, axis=1, keepdims=True)
        heads.append(_dot(prob.astype(BF16), mv_ref[0, :, sl]).astype(BF16))
    o = jnp.concatenate(heads, axis=1)
    y_ref[0] = x1 + _dot(o, wo_ref[...])


def _out_mem(x, o_mla, ret_y, o_sb, mem_k, mem_v, lw):
    B, T, D = x.shape
    tm = _tile(T, TM_OUT)
    M = mem_k.shape[1]
    mw = MEM_HEADS * MEM_HD
    nat = pl.BlockSpec((1, tm, D), lambda b, t: (b, t, 0))
    pair = lambda n: pl.BlockSpec((1, n, tm, LANES), lambda b, t: (b, 0, t, 0))
    mem = pl.BlockSpec((1, M, mw), lambda b, t: (b, 0, 0))
    return pl.pallas_call(
        _out_mem_body, grid=(B, T // tm),
        in_specs=[nat, pair(MLA_HEADS // 2), pair(RET_HEADS // 2), pair(SB_HEADS // 2), _full((D, D)),
                  _full((1, D)), _full((D, mw)), mem, mem, _full((mw, D))],
        out_specs=nat, out_shape=jax.ShapeDtypeStruct((B, T, D), F32),
        compiler_params=_cparams(2), name="out_mem",
    )(x, o_mla, ret_y, o_sb, lw["w_out"], lw["g_mem_q"], lw["w_mem_q"], mem_k, mem_v, lw["w_mem_o"])


def _mem_kv_body(mem_ref, g_ref, wk_ref, wv_ref, k_ref, v_ref, kb_ref, vb_ref):
    m = _rms(mem_ref[0], g_ref[...]).astype(BF16)
    k = _dot(m, wk_ref[...])
    v = _dot(m, wv_ref[...])
    k_ref[0] = k
    v_ref[0] = v
    kb_ref[0] = k.astype(BF16)
    vb_ref[0] = v.astype(BF16)


def _mem_kv(mem, lw):
    B, M, D = mem.shape
    mw = MEM_HEADS * MEM_HD
    o = pl.BlockSpec((1, M, mw), lambda b: (b, 0, 0))
    return pl.pallas_call(
        _mem_kv_body, grid=(B,),
        in_specs=[pl.BlockSpec((1, M, D), lambda b: (b, 0, 0)), _full((1, D)), _full((D, mw)), _full((D, mw))],
        out_specs=[o, o, o, o],
        out_shape=[jax.ShapeDtypeStruct((B, M, mw), F32), jax.ShapeDtypeStruct((B, M, mw), F32),
                   jax.ShapeDtypeStruct((B, M, mw), BF16), jax.ShapeDtypeStruct((B, M, mw), BF16)],
        compiler_params=_cparams(1), name="mem_kv",
    )(mem, lw["g_mem_kv"], lw["w_mem_k"], lw["w_mem_v"])


_CARRY_ROW = 8 - (CONV_W - 1)


def _ffn_body(x_ref, g_ref, wup_ref, wconv_ref, bconv_ref, wdown_ref, conv0_ref, gfin_ref,
              y_ref, convnew_ref, carry_ref, exta_ref, extb_ref, act_ref, *, tm, nt, final):
    t = pl.program_id(1)

    @pl.when(t == 0)
    def _():
        carry_ref[_CARRY_ROW:8, :] = conv0_ref[0]

    x = x_ref[0]
    h = _rms(x, g_ref[...]).astype(BF16)

    def conv_part(c0, ext_ref):
        cols = slice(c0, c0 + TF_FFN)
        u = _dot(h, wup_ref[:, cols])
        ext_ref[_CARRY_ROW:8, :] = carry_ref[_CARRY_ROW:8, cols]
        ext_ref[8:8 + tm, :] = u
        carry_ref[_CARRY_ROW:8, cols] = u[tm - (CONV_W - 1):, :]
        w = wconv_ref[:, cols]
        c = bconv_ref[:, cols]
        c = c + ext_ref[_CARRY_ROW:_CARRY_ROW + tm, :] * w[0:1]
        c = c + ext_ref[_CARRY_ROW + 1:_CARRY_ROW + 1 + tm, :] * w[1:2]
        return c + u * w[2:3]

    for ci in range(D_FF // TF_FFN):
        c0 = ci * TF_FFN
        a = conv_part(c0, exta_ref)
        b = conv_part(D_FF + c0, extb_ref)
        act_ref[:, c0:c0 + TF_FFN] = (a * jax.nn.sigmoid(a) * b).astype(BF16)

    x3 = x + _dot(act_ref[...], wdown_ref[...])
    y_ref[0] = _rms(x3, gfin_ref[...]) if final else x3

    @pl.when(t == nt - 1)
    def _():
        convnew_ref[0] = carry_ref[_CARRY_ROW:8, :]


def _conv_ffn(x, conv0, lw, g_final, *, final):
    B, T, D = x.shape
    tm = _tile(T, TM_FFN)
    nt = T // tm
    f2 = 2 * D_FF
    nat = pl.BlockSpec((1, tm, D), lambda b, t: (b, t, 0))
    cv = pl.BlockSpec((1, CONV_W - 1, f2), lambda b, t: (b, 0, 0))
    return pl.pallas_call(
        functools.partial(_ffn_body, tm=tm, nt=nt, final=final),
        grid=(B, nt),
        in_specs=[nat, _full((1, D)), _full((D, f2)), _full((CONV_W, f2)), _full((1, f2)), _full((D_FF, D)),
                  cv, _full((1, D))],
        out_specs=[nat, cv],
        out_shape=[jax.ShapeDtypeStruct((B, T, D), F32), jax.ShapeDtypeStruct((B, CONV_W - 1, f2), F32)],
        scratch_shapes=[pltpu.VMEM((8, f2), F32), pltpu.VMEM((8 + tm, TF_FFN), F32),
                        pltpu.VMEM((8 + tm, TF_FFN), F32), pltpu.VMEM((tm, D_FF), BF16)],
        compiler_params=_cparams(2), name="conv_ffn",
    )(x, lw["g_ffn"], lw["w_ffn_up"], lw["w_ffn_conv"], lw["b_ffn_conv"], lw["w_ffn_down"], conv0, g_final)


def _rotate_half_cols(w, heads, d):
    k = w.shape[0]
    w = w.reshape(k, heads, 2, d // 2)
    return jnp.stack([-w[:, :, 1], w[:, :, 0]], axis=2).reshape(k, heads * d)


def _pad_cols(w, width):
    return jnp.pad(w, ((0, 0), (0, width - w.shape[1])))


def _prep_layer(l, g_mix, w_in, g_q_lora, w_q_up, g_kv_lora, w_kv_up, g_ret_norm, w_out,
                g_mem_q, g_mem_kv, w_mem_q, w_mem_k, w_mem_v, w_mem_o,
                g_ffn, w_ffn_up, w_ffn_conv, b_ffn_conv, w_ffn_down):
    w = w_in[l]
    parts, c0 = [], 0
    for n in IN_SIZES:
        parts.append(w[:, c0:c0 + n])
        c0 += n
    cq, ckv, kpe, rq, rk, rv, rg, sq, sk, sv = parts
    w1 = jnp.concatenate([
        cq, ckv, _pad_cols(kpe, LANES), _pad_cols(_rotate_half_cols(kpe, 1, MLA_ROPE), LANES),
        rq, _rotate_half_cols(rq, RET_HEADS, RET_DK), rk, _rotate_half_cols(rk, RET_HEADS, RET_DK),
        rv, rg, sq, sk, sv], axis=1).astype(BF16)
    kq = w_q_up.shape[1]
    wq3 = w_q_up[l].reshape(kq, MLA_HEADS, MLA_NOPE + MLA_ROPE)
    pe = wq3[:, :, MLA_NOPE:].reshape(kq, MLA_HEADS * MLA_ROPE)
    pe_sw = _rotate_half_cols(pe, MLA_HEADS, MLA_ROPE).reshape(kq, MLA_HEADS, MLA_ROPE)
    tail = LANES - MLA_NOPE - MLA_ROPE
    q_pad = jnp.pad(wq3, ((0, 0), (0, 0), (0, tail))).reshape(kq, MLA_HEADS * LANES)
    q_sw = jnp.pad(pe_sw, ((0, 0), (0, 0), (MLA_NOPE, tail))).reshape(kq, MLA_HEADS * LANES)
    wkv3 = w_kv_up[l].reshape(MLA_KV_LORA, MLA_HEADS, MLA_NOPE + MLA_V)
    wk = jnp.pad(wkv3[:, :, :MLA_NOPE], ((0, 0), (0, 0), (0, LANES - MLA_NOPE))).reshape(MLA_KV_LORA, MLA_HEADS * LANES)
    wv = wkv3[:, :, MLA_NOPE:].reshape(MLA_KV_LORA, MLA_HEADS * MLA_V)
    r = jnp.arange(LANES)[:, None]
    c = jnp.arange(MLA_HEADS * LANES)[None, :]
    we = ((r < MLA_ROPE) & (c % LANES == r + MLA_NOPE)).astype(BF16)
    row = lambda g: g[l].reshape(1, -1).astype(F32)
    return dict(
        g_mix=row(g_mix), w1=w1, g_q=row(g_q_lora), wq=jnp.concatenate([q_pad, q_sw], axis=1).astype(BF16),
        g_kv=row(g_kv_lora), wk=wk.astype(BF16), we=we, wv=wv.astype(BF16),
        g_ret=g_ret_norm[l].astype(F32).reshape(RET_HEADS // 2, 1, LANES),
        w_out=w_out[l].astype(BF16), g_mem_q=row(g_mem_q), g_mem_kv=row(g_mem_kv),
        w_mem_q=w_mem_q[l].astype(BF16), w_mem_k=w_mem_k[l].astype(BF16), w_mem_v=w_mem_v[l].astype(BF16),
        w_mem_o=w_mem_o[l].astype(BF16), g_ffn=row(g_ffn), w_ffn_up=w_ffn_up[l].astype(BF16),
        w_ffn_conv=w_ffn_conv[l].astype(F32), b_ffn_conv=row(b_ffn_conv), w_ffn_down=w_ffn_down[l].astype(BF16))


def _rope_tables(pos):
    t = pos.shape[0]
    posf = pos.astype(F32)[:, None]

    def cos_sin(d):
        inv = ROPE_BASE ** (-jnp.arange(0, d, 2, dtype=F32) / d)
        ang = posf * inv[None, :]
        return jnp.cos(ang), jnp.sin(ang)

    c16, s16 = cos_sin(MLA_ROPE)
    c32, s32 = cos_sin(RET_DK)
    z = lambda n: jnp.zeros((t, n), F32)
    tail = LANES - MLA_NOPE - MLA_ROPE
    return dict(
        cslot=jnp.concatenate([jnp.ones((t, MLA_NOPE), F32), c16, c16, z(tail)], axis=1),
        sslot=jnp.concatenate([z(MLA_NOPE), s16, s16, z(tail)], axis=1),
        ck=jnp.concatenate([c16, c16, z(LANES - MLA_ROPE)], axis=1),
        sk=jnp.concatenate([s16, s16, z(LANES - MLA_ROPE)], axis=1),
        cr=jnp.tile(jnp.concatenate([c32, c32], axis=1), (1, RET_HEADS)),
        sr=jnp.tile(jnp.concatenate([s32, s32], axis=1), (1, RET_HEADS)))


def _pad_rows(a, axis, n):
    pad = [(0, 0)] * a.ndim
    pad[axis] = (0, n - a.shape[axis])
    return jnp.pad(a, pad)


def _to_pairs(a):
    B, T, _ = a.shape
    return jnp.transpose(a.reshape(B, T, 2, LANES), (0, 2, 1, 3))


def _layer(x, tabs, lw, mem_k, mem_v, past, g_final, *, q0, final):
    B, T, _ = x.shape
    (ckv, kpe, kpe_b, qcat, rq, rk, rv, rg, sq, sk, sv, sk_b, sv_b) = _in_proj(x, tabs, lw)
    if past is None:
        ckv_all, kpe_all, sk_all, sv_all = ckv, kpe_b, sk_b, sv_b
        s0 = jnp.zeros((B, RET_HEADS // 2, LANES, LANES), F32)
        conv0 = jnp.zeros((B, CONV_W - 1, 2 * D_FF), F32)
        kv_len = T
        lc = _tile(T, L_RET)
        tq_mla, tk_mla = _tile(T, TQ_MLA), _tile(T, TK_MLA)
        tq_sb, tk_sb = _tile(T, TQ_SB), _tile(T, TK_SB)
    else:
        ckv_c, kpe_c, s0, sk_c, sv_c, conv0 = past
        P = ckv_c.shape[1]
        kv_len = P + T
        tk_sb = TK_SB
        tk_pad = -(-kv_len // tk_sb) * tk_sb
        ckv_all = _pad_rows(jnp.concatenate([ckv_c.astype(F32), ckv], axis=1), 1, tk_pad)
        kpe_cb = _pad_cols(kpe_c.reshape(B * P, MLA_ROPE), LANES).reshape(B, P, LANES).astype(BF16)
        kpe_all = _pad_rows(jnp.concatenate([kpe_cb, kpe_b], axis=1), 1, tk_pad)
        sk_all = _pad_rows(jnp.concatenate([_to_pairs(sk_c.reshape(B, P, -1).astype(BF16)), sk_b], axis=2), 2, tk_pad)
        sv_all = _pad_rows(jnp.concatenate([_to_pairs(sv_c.reshape(B, P, -1).astype(BF16)), sv_b], axis=2), 2, tk_pad)
        s0 = _state_to_pairs(s0)
        conv0 = conv0.astype(F32)
        lc = T
        tq_mla, tk_mla = T, tk_pad
        tq_sb = T
    kcat, vmla = _kv_up(ckv_all, kpe_all, lw)
    o_mla = _mla_attention(qcat, kcat, vmla, q0=q0, kv_len=kv_len, tq=tq_mla, tk=tk_mla)
    o_sb = _sb_attention(sq, sk_all, sv_all, q0=q0, tq=tq_sb, tk=tk_sb)
    ret_y, s_new = _retention(rq, rk, rv, rg, s0, lw["g_ret"], lc=lc)
    x = _out_mem(x, o_mla, ret_y, o_sb, mem_k, mem_v, lw)
    x, conv_new = _conv_ffn(x, conv0, lw, g_final, final=final)
    state = (ckv, kpe, _pairs_to_state(s_new), sk.reshape(B, T, SB_HEADS, SB_HD), sv.reshape(B, T, SB_HEADS, SB_HD),
             conv_new)
    return x, state


def kernel(x_prompt, x_sample, cache_mla_ckv, cache_mla_kpe, state_ret, cache_sb_k, cache_sb_v, cache_mem_k, cache_mem_v, state_ffn_conv, mem_prompt, g_mix, w_in, g_q_lora, w_q_up, g_kv_lora, w_kv_up, g_ret_norm, w_out, g_mem_q, g_mem_kv, w_mem_q, w_mem_k, w_mem_v, w_mem_o, g_ffn, w_ffn_up, w_ffn_conv, b_ffn_conv, w_ffn_down, g_final):
    depth = w_in.shape[0]
    layers = [_prep_layer(l, g_mix, w_in, g_q_lora, w_q_up, g_kv_lora, w_kv_up, g_ret_norm, w_out,
                          g_mem_q, g_mem_kv, w_mem_q, w_mem_k, w_mem_v, w_mem_o,
                          g_ffn, w_ffn_up, w_ffn_conv, b_ffn_conv, w_ffn_down) for l in range(depth)]
    gfin = g_final.reshape(1, -1).astype(F32)
    mw = MEM_HEADS * MEM_HD

    xp = x_prompt
    Bp, Tp, _ = xp.shape
    tabs_p = _rope_tables(jnp.arange(Tp))
    p_states, p_mem_k, p_mem_v = [], [], []
    for l in range(depth):
        mk, mv, mk_b, mv_b = _mem_kv(mem_prompt, layers[l])
        xp, st = _layer(xp, tabs_p, layers[l], mk_b, mv_b, None, gfin, q0=0, final=(l == depth - 1))
        p_states.append(st)
        p_mem_k.append(mk.reshape(Bp, -1, MEM_HEADS, MEM_HD))
        p_mem_v.append(mv.reshape(Bp, -1, MEM_HEADS, MEM_HD))

    xs = x_sample
    Bs, Ts, _ = xs.shape
    past_len = cache_mla_ckv.shape[2]
    tabs_s = _rope_tables(past_len + jnp.arange(Ts))
    s_states = []
    for l in range(depth):
        past = (cache_mla_ckv[l], cache_mla_kpe[l], state_ret[l], cache_sb_k[l], cache_sb_v[l], state_ffn_conv[l])
        mk_b = cache_mem_k[l].reshape(Bs, -1, mw).astype(BF16)
        mv_b = cache_mem_v[l].reshape(Bs, -1, mw).astype(BF16)
        xs, st = _layer(xs, tabs_s, layers[l], mk_b, mv_b, past, gfin, q0=past_len, final=(l == depth - 1))
        s_states.append(st)

    p_ckv, p_kpe, p_ret, p_sbk, p_sbv, p_conv = [jnp.stack(f) for f in zip(*p_states)]
    s_ckv, s_kpe, s_ret, s_sbk, s_sbv, s_conv = [jnp.stack(f) for f in zip(*s_states)]
    return (xp, xs, p_ckv, p_kpe, p_ret, p_sbk, p_sbv, jnp.stack(p_mem_k), jnp.stack(p_mem_v), p_conv,
            s_ckv, s_kpe, s_ret, s_sbk, s_sbv, s_conv)
```

```python
import functools

import jax
import jax.numpy as jnp
from jax import lax
from jax.experimental import pallas as pl
from jax.experimental.pallas import tpu as pltpu

F32 = jnp.float32
BF16 = jnp.bfloat16

CHUNK = 64
EPS = 1e-6
ROPE_BASE = 10000.0

MLA_HEADS = 8
MLA_NOPE = 64
MLA_ROPE = 32
MLA_V = 64
MLA_Q_LORA = 256
MLA_KV_LORA = 128
MLA_SCALE = (MLA_NOPE + MLA_ROPE) ** -0.5
_LOG2E = 1.4426950408889634
_MLA_EXP2_SCALE = MLA_SCALE * _LOG2E
_SB_UNDERFLOW_LOG2 = -160.0
RET_HEADS = 4
RET_DK = 64
RET_DV = 64
RET_SCALE = RET_DK ** -0.5
SB_HEADS = 4
SB_HD = 64
SB_SCALE = SB_HD ** -0.5
MEM_HEADS = 4
MEM_HD = 128
MEM_SCALE = MEM_HD ** -0.5
D_FF = 2816
CONV_W = 3

IN_SIZES = (MLA_Q_LORA, MLA_KV_LORA, MLA_ROPE, RET_HEADS * RET_DK, RET_HEADS * RET_DK,
            RET_HEADS * RET_DV, RET_HEADS * RET_DV, SB_HEADS * SB_HD, SB_HEADS * SB_HD, SB_HEADS * SB_HD)

LANES = 128
HALF = LANES // 2
VMEM_LIMIT = 56 * 1024 * 1024

TM_PROJ = 512
TM_KVUP = 512
TQ_MLA = 512
TK_MLA = 512
TQ_SB = 256
TK_SB = 256
TC_RET = 512
L_RET = 256
TM_OUT = 512
TM_FFN = 512
TF_FFN = 256


def _tile(n, target):
    t = min(n, target)
    while n % t:
        t -= 1
    return t


def _cparams(n_axes):
    return pltpu.CompilerParams(dimension_semantics=("arbitrary",) * n_axes, vmem_limit_bytes=VMEM_LIMIT)


def _rms(x, g):
    return x * lax.rsqrt(jnp.mean(x * x, axis=-1, keepdims=True) + EPS) * g


def _lane_tile(x, reps):
    return jnp.concatenate([x] * reps, axis=1) if reps > 1 else x


def _full(shape):
    n = len(shape)
    return pl.BlockSpec(shape, lambda *_: (0,) * n)


def _dot(a, b):
    return jnp.dot(a, b, preferred_element_type=F32)


def _dot_nt(a, b):
    return lax.dot_general(a, b, (((1,), (1,)), ((), ())), preferred_element_type=F32)


_C_CQ = 0
_C_CKV = 256
_C_KPE = 384
_C_KPE_SW = 512
_C_RQ = 640
_C_RQ_SW = 896
_C_RK = 1152
_C_RK_SW = 1408
_C_RV = 1664
_C_RG = 1920
_C_SQ = 2176
_C_SK = 2432
_C_SV = 2688
_W1_COLS = 2944


def _in_proj_body(x_ref, gmix_ref, w1_ref, gq_ref, wq_ref, gkv_ref,
                  cslot_ref, sslot_ref, ck_ref, sk_tab_ref, cr_ref, sr_ref,
                  ckv_ref, kpe_ref, kpeb_ref, qcat_ref, rq_ref, rk_ref, rv_ref, rg_ref,
                  sq_ref, sk_ref, sv_ref, skb_ref, svb_ref):
    h = _rms(x_ref[0], gmix_ref[...]).astype(BF16)

    def proj(c0, width):
        return _dot(h, w1_ref[:, c0:c0 + width])

    cqn = _rms(proj(_C_CQ, 256), gq_ref[...]).astype(BF16)
    cslot = cslot_ref[...]
    sslot = sslot_ref[...]
    nslot = MLA_HEADS * LANES
    for p in range(MLA_HEADS // 2):
        q2 = _dot(cqn, wq_ref[:, 2 * LANES * p:2 * LANES * (p + 1)])
        q2s = _dot(cqn, wq_ref[:, nslot + 2 * LANES * p:nslot + 2 * LANES * (p + 1)])
        for a in range(2):
            sl = slice(a * LANES, (a + 1) * LANES)
            qcat_ref[0, 2 * p + a] = (q2[:, sl] * cslot + q2s[:, sl] * sslot).astype(BF16)

    ckv_ref[0] = _rms(proj(_C_CKV, 128), gkv_ref[...])
    kpe = proj(_C_KPE, 128) * ck_ref[...] + proj(_C_KPE_SW, 128) * sk_tab_ref[...]
    kpe_ref[0] = kpe[:, :MLA_ROPE]
    kpeb_ref[0] = kpe.astype(BF16)

    cr = cr_ref[...]
    sr = sr_ref[...]
    rq = proj(_C_RQ, 256) * cr + proj(_C_RQ_SW, 256) * sr
    rk = (proj(_C_RK, 256) * cr + proj(_C_RK_SW, 256) * sr) * RET_SCALE
    rv = proj(_C_RV, 256)
    rg = proj(_C_RG, 256)
    sq = proj(_C_SQ, 256) * SB_SCALE
    sk = proj(_C_SK, 256)
    sv = proj(_C_SV, 256)
    sk_ref[0] = sk
    sv_ref[0] = sv
    for p in range(2):
        sl = slice(p * LANES, (p + 1) * LANES)
        rq_ref[0, p] = rq[:, sl].astype(BF16)
        rk_ref[0, p] = rk[:, sl].astype(BF16)
        rv_ref[0, p] = rv[:, sl].astype(BF16)
        rg_ref[0, p] = rg[:, sl]
        sq_ref[0, p] = sq[:, sl].astype(BF16)
        skb_ref[0, p] = sk[:, sl].astype(BF16)
        svb_ref[0, p] = sv[:, sl].astype(BF16)


def _in_proj(x, tabs, lw):
    B, T, D = x.shape
    tm = _tile(T, TM_PROJ)
    grid = (B, T // tm)
    row = lambda w: pl.BlockSpec((tm, w), lambda b, t: (t, 0))
    pair = lambda n: pl.BlockSpec((1, n, tm, LANES), lambda b, t: (b, 0, t, 0))
    nat = lambda w: pl.BlockSpec((1, tm, w), lambda b, t: (b, t, 0))
    in_specs = [nat(D), _full((1, D)), _full((D, _W1_COLS)), _full((1, MLA_Q_LORA)),
                _full((MLA_Q_LORA, 2 * MLA_HEADS * LANES)), _full((1, MLA_KV_LORA)),
                row(LANES), row(LANES), row(LANES), row(LANES), row(256), row(256)]
    out_shape = [
        jax.ShapeDtypeStruct((B, T, MLA_KV_LORA), F32),
        jax.ShapeDtypeStruct((B, T, MLA_ROPE), F32),
        jax.ShapeDtypeStruct((B, T, LANES), BF16),
        jax.ShapeDtypeStruct((B, MLA_HEADS, T, LANES), BF16),
        jax.ShapeDtypeStruct((B, 2, T, LANES), BF16),
        jax.ShapeDtypeStruct((B, 2, T, LANES), BF16),
        jax.ShapeDtypeStruct((B, 2, T, LANES), BF16),
        jax.ShapeDtypeStruct((B, 2, T, LANES), F32),
        jax.ShapeDtypeStruct((B, 2, T, LANES), BF16),
        jax.ShapeDtypeStruct((B, T, 256), F32),
        jax.ShapeDtypeStruct((B, T, 256), F32),
        jax.ShapeDtypeStruct((B, 2, T, LANES), BF16),
        jax.ShapeDtypeStruct((B, 2, T, LANES), BF16),
    ]
    out_specs = [nat(MLA_KV_LORA), nat(MLA_ROPE), nat(LANES), pair(MLA_HEADS), pair(2), pair(2), pair(2),
                 pair(2), pair(2), nat(256), nat(256), pair(2), pair(2)]
    return pl.pallas_call(
        _in_proj_body, grid=grid, in_specs=in_specs, out_specs=out_specs, out_shape=out_shape,
        compiler_params=_cparams(2), name="in_proj",
    )(x, lw["g_mix"], lw["w1"], lw["g_q"], lw["wq"], lw["g_kv"],
      tabs["cslot"], tabs["sslot"], tabs["ck"], tabs["sk"], tabs["cr"], tabs["sr"])


def _kv_up_body(ckv_ref, kpe_ref, wk_ref, we_ref, wv_ref, kcat_ref, v_ref):
    c = ckv_ref[0].astype(BF16)
    kp = kpe_ref[0]
    for p in range(MLA_HEADS // 2):
        cols = slice(2 * LANES * p, 2 * LANES * (p + 1))
        k2 = _dot(c, wk_ref[:, cols]) + _dot(kp, we_ref[:, cols])
        kcat_ref[0, 2 * p] = k2[:, :LANES].astype(BF16)
        kcat_ref[0, 2 * p + 1] = k2[:, LANES:].astype(BF16)
        v_ref[0, p] = _dot(c, wv_ref[:, LANES * p:LANES * (p + 1)]).astype(BF16)


def _kv_up(ckv, kpe_pad, lw):
    B, T, _ = ckv.shape
    tm = _tile(T, TM_KVUP)
    nat = lambda w: pl.BlockSpec((1, tm, w), lambda b, t: (b, t, 0))
    pair = lambda n: pl.BlockSpec((1, n, tm, LANES), lambda b, t: (b, 0, t, 0))
    return pl.pallas_call(
        _kv_up_body, grid=(B, T // tm),
        in_specs=[nat(MLA_KV_LORA), nat(LANES), _full((MLA_KV_LORA, MLA_HEADS * LANES)),
                  _full((LANES, MLA_HEADS * LANES)), _full((MLA_KV_LORA, MLA_HEADS * MLA_V))],
        out_specs=[pair(MLA_HEADS), pair(MLA_HEADS // 2)],
        out_shape=[jax.ShapeDtypeStruct((B, MLA_HEADS, T, LANES), BF16),
                   jax.ShapeDtypeStruct((B, MLA_HEADS // 2, T, LANES), BF16)],
        compiler_params=_cparams(2), name="kv_up",
    )(ckv, kpe_pad, lw["wk"], lw["we"], lw["wv"])


def _mla_last_block(i, *, q0, tq, tk, nk):
    last_q = q0 + i * tq + (tq - 1)
    last_key = (last_q // CHUNK) * CHUNK + (CHUNK - 1)
    return jnp.minimum(last_key // tk, nk - 1)


def _mla_body(q_ref, k_ref, v_ref, o_ref, m_ref, l_ref, acc_ref, *, q0, tq, tk, nk, kv_len):
    i = pl.program_id(1)
    j = pl.program_id(2)

    @pl.when(j == 0)
    def _():
        m_ref[...] = jnp.full(m_ref.shape, -jnp.inf, F32)
        l_ref[...] = jnp.zeros(l_ref.shape, F32)
        acc_ref[...] = jnp.zeros(acc_ref.shape, F32)

    def step(masked):
        lane = lax.broadcasted_iota(jnp.int32, (tq, LANES), 1)
        first = lane < HALF
        if masked:
            qpos = q0 + i * tq + lax.broadcasted_iota(jnp.int32, (tq, 1), 0)
            limit = jnp.minimum((jnp.right_shift(qpos, 6) + 1) * CHUNK, kv_len)
            kpos = j * tk + lax.broadcasted_iota(jnp.int32, (tq, tk), 1)
            bias = jnp.where(kpos < limit, 0.0, -jnp.inf).astype(F32)

        for p in range(MLA_HEADS // 2):
            vp = v_ref[0, p]
            alphas, pvs = [], []
            for a in range(2):
                hh = 2 * p + a
                s = _dot_nt(q_ref[0, hh], k_ref[0, hh])
                if masked:
                    s = s + bias
                m_prev = m_ref[hh]
                m_new = jnp.maximum(m_prev, jnp.max(s, axis=1, keepdims=True))
                e = jnp.exp2((s - _lane_tile(m_new, tk // LANES)) * _MLA_EXP2_SCALE)
                alpha = jnp.exp2((m_prev - m_new) * _MLA_EXP2_SCALE)
                l_ref[hh] = alpha * l_ref[hh] + jnp.sum(e, axis=1, keepdims=True)
                m_ref[hh] = m_new
                alphas.append(alpha)
                pvs.append(_dot(e.astype(BF16), vp))
            acc_ref[p] = acc_ref[p] * jnp.where(first, alphas[0], alphas[1]) + jnp.where(first, pvs[0], pvs[1])

    all_visible = jnp.minimum(((q0 + i * tq) // CHUNK + 1) * CHUNK, kv_len)
    fully_visible = (j + 1) * tk <= all_visible
    needed = j <= _mla_last_block(i, q0=q0, tq=tq, tk=tk, nk=nk)

    @pl.when(jnp.logical_and(needed, fully_visible))
    def _():
        step(False)

    @pl.when(jnp.logical_and(needed, jnp.logical_not(fully_visible)))
    def _():
        step(True)

    @pl.when(j == nk - 1)
    def _():
        lane = lax.broadcasted_iota(jnp.int32, (tq, LANES), 1)
        for p in range(MLA_HEADS // 2):
            inv = jnp.where(lane < HALF, 1.0 / l_ref[2 * p], 1.0 / l_ref[2 * p + 1])
            o_ref[0, p] = (acc_ref[p] * inv).astype(BF16)


def _mla_attention(qcat, kcat, v, *, q0, kv_len, tq, tk):
    B, H, Tq, _ = qcat.shape
    Tk = kcat.shape[2]
    nq, nk = Tq // tq, Tk // tk
    kw = dict(q0=q0, tq=tq, tk=tk, nk=nk)

    def kv_map(b, i, j):
        return (b, 0, jnp.minimum(j, _mla_last_block(i, **kw)), 0)

    return pl.pallas_call(
        functools.partial(_mla_body, kv_len=kv_len, **kw),
        grid=(B, nq, nk),
        in_specs=[pl.BlockSpec((1, H, tq, LANES), lambda b, i, j: (b, 0, i, 0)),
                  pl.BlockSpec((1, H, tk, LANES), kv_map),
                  pl.BlockSpec((1, H // 2, tk, LANES), kv_map)],
        out_specs=pl.BlockSpec((1, H // 2, tq, LANES), lambda b, i, j: (b, 0, i, 0)),
        out_shape=jax.ShapeDtypeStruct((B, H // 2, Tq, LANES), BF16),
        scratch_shapes=[pltpu.VMEM((H, tq, LANES), F32), pltpu.VMEM((H, tq, LANES), F32),
                        pltpu.VMEM((H // 2, tq, LANES), F32)],
        compiler_params=_cparams(3), name="mla_attn",
    )(qcat, kcat, v)


def _sb_last_block(i, *, q0, tq, tk, nk):
    last_key = q0 + i * tq + (tq - 1) - 1
    return jnp.clip(last_key // tk, 0, nk - 1)


def _sb_body(q_ref, k_ref, v_ref, u_ref, o_ref, run_ref, acc_ref, *, q0, tq, tk, nk):
    i = pl.program_id(1)
    q_first = q0 + i * tq
    last = _sb_last_block(i, q0=q0, tq=tq, tk=tk, nk=nk)
    first_unmasked = jnp.minimum(q_first // tk - 1, last)
    run_ref[...] = jnp.zeros(run_ref.shape, F32)
    acc_ref[...] = jnp.zeros(acc_ref.shape, F32)

    def step(kb, masked):
        first_k = lax.broadcasted_iota(jnp.int32, (tk, LANES), 1) < HALF
        tri = u_ref[...]
        reps = tk // LANES
        rows = pl.ds(pl.multiple_of(kb * tk, tk), tk)
        if masked:
            qpos = q_first + lax.broadcasted_iota(jnp.int32, (tq, 1), 0)
            kpos = kb * tk + lax.broadcasted_iota(jnp.int32, (tq, tk), 1)
            before1 = kpos < qpos
            before = jnp.concatenate([before1, before1], axis=1)

        for p in range(SB_HEADS // 2):
            kp = k_ref[0, p, rows, :]
            vp = v_ref[0, p, rows, :]
            zero = jnp.zeros_like(kp)
            k2 = jnp.concatenate([jnp.where(first_k, kp, zero), jnp.where(first_k, zero, kp)], axis=0)
            v2 = jnp.concatenate([jnp.where(first_k, vp, zero), jnp.where(first_k, zero, vp)], axis=0)
            z = _dot_nt(q_ref[0, p], k2) * _LOG2E
            nz = -z
            sp = jnp.log(1.0 + jnp.exp2(jnp.minimum(z, nz))) * _LOG2E
            log_keep = jnp.minimum(nz, 0.0) - sp
            log_beta = log_keep + z
            if masked:
                log_keep = jnp.where(before, log_keep, 0.0)
            hi = log_keep.astype(BF16)
            lo = (log_keep - hi.astype(F32)).astype(BF16)
            between = jnp.concatenate(
                [_dot(hi[:, :tk], tri) + _dot(lo[:, :tk], tri), _dot(hi[:, tk:], tri) + _dot(lo[:, tk:], tri)], axis=1)
            run_a = run_ref[2 * p]
            run_b = run_ref[2 * p + 1]
            run = jnp.concatenate([_lane_tile(run_a, reps), _lane_tile(run_b, reps)], axis=1)
            att = jnp.exp2(log_beta + between + run)
            if masked:
                att = jnp.where(before, att, 0.0)
            run_ref[2 * p] = run_a + jnp.sum(log_keep[:, :tk], axis=1, keepdims=True)
            run_ref[2 * p + 1] = run_b + jnp.sum(log_keep[:, tk:], axis=1, keepdims=True)
            acc_ref[p] = acc_ref[p] + _dot(att.astype(BF16), v2)

    def masked_step(t, carry):
        step(last - t, True)
        return carry

    lax.fori_loop(0, last - first_unmasked, masked_step, 0)

    def more(carry):
        kb, live = carry
        return jnp.logical_and(kb >= 0, live > 0)

    def unmasked_step(carry):
        kb, _ = carry
        step(kb, False)
        live = (jnp.max(run_ref[...]) >= _SB_UNDERFLOW_LOG2).astype(jnp.int32)
        return kb - 1, live

    lax.while_loop(more, unmasked_step, (first_unmasked, jnp.int32(1)))
    o_ref[0] = acc_ref[...].astype(BF16)


def _sb_attention(q, k, v, *, q0, tq, tk):
    B, P, Tq, _ = q.shape
    Tk = k.shape[2]
    nq, nk = Tq // tq, Tk // tk
    kw = dict(q0=q0, tq=tq, tk=tk, nk=nk)
    r = lax.broadcasted_iota(jnp.int32, (tk, tk), 0)
    c = lax.broadcasted_iota(jnp.int32, (tk, tk), 1)
    tri = (r > c).astype(BF16)

    kv_spec = pl.BlockSpec((1, P, Tk, LANES), lambda b, i: (b, 0, 0, 0))
    return pl.pallas_call(
        functools.partial(_sb_body, **kw),
        grid=(B, nq),
        in_specs=[pl.BlockSpec((1, P, tq, LANES), lambda b, i: (b, 0, i, 0)), kv_spec, kv_spec, _full((tk, tk))],
        out_specs=pl.BlockSpec((1, P, tq, LANES), lambda b, i: (b, 0, i, 0)),
        out_shape=jax.ShapeDtypeStruct((B, P, Tq, LANES), BF16),
        scratch_shapes=[pltpu.VMEM((2 * P, tq, LANES), F32), pltpu.VMEM((P, tq, LANES), F32)],
        compiler_params=_cparams(2), name="sb_attn",
    )(q, k, v, tri)


def _ret_body(q_ref, k_ref, v_ref, g_ref, s0_ref, dmask_ref, dq_ref, dk_ref, cd_ref, bd_ref, gn_ref,
              y_ref, snew_ref, state_ref, *, tc, lc, nc):
    c = pl.program_id(1)

    @pl.when(c == 0)
    def _():
        state_ref[...] = s0_ref[0]

    lane = lax.broadcasted_iota(jnp.int32, (lc, LANES), 1)
    first = lane < HALF
    for p in range(RET_HEADS // 2):
        for ch in range(tc // lc):
            rows = slice(ch * lc, (ch + 1) * lc)
            q = q_ref[0, p, rows, :]
            k = k_ref[0, p, rows, :]
            v = v_ref[0, p, rows, :]
            state = state_ref[p]
            cross = _dot((q.astype(F32) * dq_ref[p]).astype(BF16), state.astype(BF16))
            zero = jnp.zeros_like(q)
            inner = []
            for a in range(2):
                qa = jnp.where(first, q, zero) if a == 0 else jnp.where(first, zero, q)
                scores = _dot_nt(qa, k) * dmask_ref[2 * p + a]
                inner.append(_dot(scores.astype(BF16), v))
            o = jnp.where(first, inner[0], inner[1]) + cross
            kd = (k.astype(F32) * dk_ref[p]).astype(BF16)
            kv = lax.dot_general(kd, v, (((0,), (0,)), ((), ())), preferred_element_type=F32)
            state_ref[p] = cd_ref[p] * state + bd_ref[...] * kv
            zf = jnp.zeros_like(o)
            mu = jnp.where(first, jnp.sum(jnp.where(first, o, zf), axis=1, keepdims=True),
                           jnp.sum(jnp.where(first, zf, o), axis=1, keepdims=True)) * (1.0 / RET_DV)
            d = o - mu
            d2 = d * d
            var = jnp.where(first, jnp.sum(jnp.where(first, d2, zf), axis=1, keepdims=True),
                            jnp.sum(jnp.where(first, zf, d2), axis=1, keepdims=True)) * (1.0 / RET_DV)
            yn = d * lax.rsqrt(var + EPS) * gn_ref[p]
            gate = g_ref[0, p, rows, :]
            y_ref[0, p, rows, :] = (yn * (gate * jax.nn.sigmoid(gate))).astype(BF16)

    @pl.when(c == nc - 1)
    def _():
        snew_ref[0] = state_ref[...]


def _ret_tables(lc):
    log_gamma = jnp.log(1.0 - 2.0 ** (-5.0 - jnp.arange(RET_HEADS, dtype=F32)))
    idx = jnp.arange(lc, dtype=F32)
    diff = idx[:, None] - idx[None, :]
    dmask = jnp.where(diff[None] >= 0, jnp.exp(jnp.maximum(diff, 0.0)[None] * log_gamma[:, None, None]), 0.0)
    dk = jnp.exp((lc - 1.0 - idx)[:, None] * log_gamma[None, :])
    dq = jnp.exp((idx + 1.0)[:, None] * log_gamma[None, :])
    cd = jnp.exp(lc * log_gamma)

    def lanes(t):
        t = jnp.repeat(t[:, :, None], HALF, axis=2).reshape(t.shape[0], RET_HEADS // 2, LANES)
        return jnp.transpose(t, (1, 0, 2))

    blk = jnp.arange(LANES) // HALF
    bd = (blk[:, None] == blk[None, :]).astype(F32)
    cd_rows = jnp.repeat(cd, HALF).reshape(RET_HEADS // 2, LANES, 1)
    return dmask, lanes(dq), lanes(dk), cd_rows * bd[None], bd


def _state_to_pairs(s):
    B = s.shape[0]
    s = s.astype(F32).reshape(B, RET_HEADS // 2, 2, RET_DK, RET_DV)
    z = jnp.zeros_like(s[:, :, 0])
    top = jnp.concatenate([s[:, :, 0], z], axis=-1)
    bot = jnp.concatenate([z, s[:, :, 1]], axis=-1)
    return jnp.concatenate([top, bot], axis=-2)


def _pairs_to_state(sp):
    a = sp[:, :, :HALF, :HALF]
    b = sp[:, :, HALF:, HALF:]
    B = sp.shape[0]
    return jnp.stack([a, b], axis=2).reshape(B, RET_HEADS, RET_DK, RET_DV)


def _retention(rq, rk, rv, rg, s0_pairs, g_norm_pairs, *, lc):
    B, P, T, _ = rq.shape
    tc = _tile(T, TC_RET)
    nc = T // tc
    dmask, dq, dk, cd, bd = _ret_tables(lc)
    pair = pl.BlockSpec((1, P, tc, LANES), lambda b, c: (b, 0, c, 0))
    st = pl.BlockSpec((1, P, LANES, LANES), lambda b, c: (b, 0, 0, 0))
    return pl.pallas_call(
        functools.partial(_ret_body, tc=tc, lc=lc, nc=nc),
        grid=(B, nc),
        in_specs=[pair, pair, pair, pair, st, _full(dmask.shape), _full(dq.shape), _full(dk.shape),
                  _full(cd.shape), _full(bd.shape), _full(g_norm_pairs.shape)],
        out_specs=[pair, st],
        out_shape=[jax.ShapeDtypeStruct((B, P, T, LANES), BF16),
                   jax.ShapeDtypeStruct((B, P, LANES, LANES), F32)],
        scratch_shapes=[pltpu.VMEM((P, LANES, LANES), F32)],
        compiler_params=_cparams(2), name="retention",
    )(rq, rk, rv, rg, s0_pairs, dmask, dq, dk, cd, bd, g_norm_pairs)


def _out_mem_body(x_ref, omla_ref, ret_ref, osb_ref, wout_ref, gq_ref, wq_ref, mk_ref, mv_ref, wo_ref, y_ref):
    mixed = 0.0
    row0 = 0
    for ref, n in ((omla_ref, MLA_HEADS // 2), (ret_ref, RET_HEADS // 2), (osb_ref, SB_HEADS // 2)):
        for p in range(n):
            mixed = mixed + _dot(ref[0, p], wout_ref[row0:row0 + LANES, :])
            row0 += LANES
    x1 = x_ref[0] + mixed
    hq = _rms(x1, gq_ref[...]).astype(BF16)
    q = _dot(hq, wq_ref[...]).astype(BF16)
    heads = []
    for hh in range(MEM_HEADS):
        sl = slice(hh * MEM_HD, (hh + 1) * MEM_HD)
        s = _dot_nt(q[:, sl], mk_ref[0, :, sl]) * MEM_SCALE
        e = jnp.exp(s - jnp.max(s, axis=1, keepdims=True))
        prob = e / jnp.sum(e, axis=1, keepdims=True)
        heads.append(_dot(prob.astype(BF16), mv_ref[0, :, sl]).astype(BF16))
    o = jnp.concatenate(heads, axis=1)
    y_ref[0] = x1 + _dot(o, wo_ref[...])


def _out_mem(x, o_mla, ret_y, o_sb, mem_k, mem_v, lw):
    B, T, D = x.shape
    tm = _tile(T, TM_OUT)
    M = mem_k.shape[1]
    mw = MEM_HEADS * MEM_HD
    nat = pl.BlockSpec((1, tm, D), lambda b, t: (b, t, 0))
    pair = lambda n: pl.BlockSpec((1, n, tm, LANES), lambda b, t: (b, 0, t, 0))
    mem = pl.BlockSpec((1, M, mw), lambda b, t: (b, 0, 0))
    return pl.pallas_call(
        _out_mem_body, grid=(B, T // tm),
        in_specs=[nat, pair(MLA_HEADS // 2), pair(RET_HEADS // 2), pair(SB_HEADS // 2), _full((D, D)),
                  _full((1, D)), _full((D, mw)), mem, mem, _full((mw, D))],
        out_specs=nat, out_shape=jax.ShapeDtypeStruct((B, T, D), F32),
        compiler_params=_cparams(2), name="out_mem",
    )(x, o_mla, ret_y, o_sb, lw["w_out"], lw["g_mem_q"], lw["w_mem_q"], mem_k, mem_v, lw["w_mem_o"])


def _mem_kv_body(mem_ref, g_ref, wk_ref, wv_ref, k_ref, v_ref, kb_ref, vb_ref):
    m = _rms(mem_ref[0], g_ref[...]).astype(BF16)
    k = _dot(m, wk_ref[...])
    v = _dot(m, wv_ref[...])
    k_ref[0] = k
    v_ref[0] = v
    kb_ref[0] = k.astype(BF16)
    vb_ref[0] = v.astype(BF16)


def _mem_kv(mem, lw):
    B, M, D = mem.shape
    mw = MEM_HEADS * MEM_HD
    o = pl.BlockSpec((1, M, mw), lambda b: (b, 0, 0))
    return pl.pallas_call(
        _mem_kv_body, grid=(B,),
        in_specs=[pl.BlockSpec((1, M, D), lambda b: (b, 0, 0)), _full((1, D)), _full((D, mw)), _full((D, mw))],
        out_specs=[o, o, o, o],
        out_shape=[jax.ShapeDtypeStruct((B, M, mw), F32), jax.ShapeDtypeStruct((B, M, mw), F32),
                   jax.ShapeDtypeStruct((B, M, mw), BF16), jax.ShapeDtypeStruct((B, M, mw), BF16)],
        compiler_params=_cparams(1), name="mem_kv",
    )(mem, lw["g_mem_kv"], lw["w_mem_k"], lw["w_mem_v"])


_CARRY_ROW = 8 - (CONV_W - 1)


def _ffn_body(x_ref, g_ref, wup_ref, wconv_ref, bconv_ref, wdown_ref, conv0_ref, gfin_ref,
              y_ref, convnew_ref, carry_ref, exta_ref, extb_ref, act_ref, *, tm, nt, final):
    t = pl.program_id(1)

    @pl.when(t == 0)
    def _():
        carry_ref[_CARRY_ROW:8, :] = conv0_ref[0]

    x = x_ref[0]
    h = _rms(x, g_ref[...]).astype(BF16)

    def conv_part(c0, ext_ref):
        cols = slice(c0, c0 + TF_FFN)
        u = _dot(h, wup_ref[:, cols])
        ext_ref[_CARRY_ROW:8, :] = carry_ref[_CARRY_ROW:8, cols]
        ext_ref[8:8 + tm, :] = u
        carry_ref[_CARRY_ROW:8, cols] = u[tm - (CONV_W - 1):, :]
        w = wconv_ref[:, cols]
        c = bconv_ref[:, cols]
        c = c + ext_ref[_CARRY_ROW:_CARRY_ROW + tm, :] * w[0:1]
        c = c + ext_ref[_CARRY_ROW + 1:_CARRY_ROW + 1 + tm, :] * w[1:2]
        return c + u * w[2:3]

    for ci in range(D_FF // TF_FFN):
        c0 = ci * TF_FFN
        a = conv_part(c0, exta_ref)
        b = conv_part(D_FF + c0, extb_ref)
        act_ref[:, c0:c0 + TF_FFN] = (a * jax.nn.sigmoid(a) * b).astype(BF16)

    x3 = x + _dot(act_ref[...], wdown_ref[...])
    y_ref[0] = _rms(x3, gfin_ref[...]) if final else x3

    @pl.when(t == nt - 1)
    def _():
        convnew_ref[0] = carry_ref[_CARRY_ROW:8, :]


def _conv_ffn(x, conv0, lw, g_final, *, final):
    B, T, D = x.shape
    tm = _tile(T, TM_FFN)
    nt = T // tm
    f2 = 2 * D_FF
    nat = pl.BlockSpec((1, tm, D), lambda b, t: (b, t, 0))
    cv = pl.BlockSpec((1, CONV_W - 1, f2), lambda b, t: (b, 0, 0))
    return pl.pallas_call(
        functools.partial(_ffn_body, tm=tm, nt=nt, final=final),
        grid=(B, nt),
        in_specs=[nat, _full((1, D)), _full((D, f2)), _full((CONV_W, f2)), _full((1, f2)), _full((D_FF, D)),
                  cv, _full((1, D))],
        out_specs=[nat, cv],
        out_shape=[jax.ShapeDtypeStruct((B, T, D), F32), jax.ShapeDtypeStruct((B, CONV_W - 1, f2), F32)],
        scratch_shapes=[pltpu.VMEM((8, f2), F32), pltpu.VMEM((8 + tm, TF_FFN), F32),
                        pltpu.VMEM((8 + tm, TF_FFN), F32), pltpu.VMEM((tm, D_FF), BF16)],
        compiler_params=_cparams(2), name="conv_ffn",
    )(x, lw["g_ffn"], lw["w_ffn_up"], lw["w_ffn_conv"], lw["b_ffn_conv"], lw["w_ffn_down"], conv0, g_final)


def _rotate_half_cols(w, heads, d):
    k = w.shape[0]
    w = w.reshape(k, heads, 2, d // 2)
    return jnp.stack([-w[:, :, 1], w[:, :, 0]], axis=2).reshape(k, heads * d)


def _pad_cols(w, width):
    return jnp.pad(w, ((0, 0), (0, width - w.shape[1])))


def _prep_layer(l, g_mix, w_in, g_q_lora, w_q_up, g_kv_lora, w_kv_up, g_ret_norm, w_out,
                g_mem_q, g_mem_kv, w_mem_q, w_mem_k, w_mem_v, w_mem_o,
                g_ffn, w_ffn_up, w_ffn_conv, b_ffn_conv, w_ffn_down):
    w = w_in[l]
    parts, c0 = [], 0
    for n in IN_SIZES:
        parts.append(w[:, c0:c0 + n])
        c0 += n
    cq, ckv, kpe, rq, rk, rv, rg, sq, sk, sv = parts
    w1 = jnp.concatenate([
        cq, ckv, _pad_cols(kpe, LANES), _pad_cols(_rotate_half_cols(kpe, 1, MLA_ROPE), LANES),
        rq, _rotate_half_cols(rq, RET_HEADS, RET_DK), rk, _rotate_half_cols(rk, RET_HEADS, RET_DK),
        rv, rg, sq, sk, sv], axis=1).astype(BF16)
    kq = w_q_up.shape[1]
    wq3 = w_q_up[l].reshape(kq, MLA_HEADS, MLA_NOPE + MLA_ROPE)
    pe = wq3[:, :, MLA_NOPE:].reshape(kq, MLA_HEADS * MLA_ROPE)
    pe_sw = _rotate_half_cols(pe, MLA_HEADS, MLA_ROPE).reshape(kq, MLA_HEADS, MLA_ROPE)
    tail = LANES - MLA_NOPE - MLA_ROPE
    q_pad = jnp.pad(wq3, ((0, 0), (0, 0), (0, tail))).reshape(kq, MLA_HEADS * LANES)
    q_sw = jnp.pad(pe_sw, ((0, 0), (0, 0), (MLA_NOPE, tail))).reshape(kq, MLA_HEADS * LANES)
    wkv3 = w_kv_up[l].reshape(MLA_KV_LORA, MLA_HEADS, MLA_NOPE + MLA_V)
    wk = jnp.pad(wkv3[:, :, :MLA_NOPE], ((0, 0), (0, 0), (0, LANES - MLA_NOPE))).reshape(MLA_KV_LORA, MLA_HEADS * LANES)
    wv = wkv3[:, :, MLA_NOPE:].reshape(MLA_KV_LORA, MLA_HEADS * MLA_V)
    r = jnp.arange(LANES)[:, None]
    c = jnp.arange(MLA_HEADS * LANES)[None, :]
    we = ((r < MLA_ROPE) & (c % LANES == r + MLA_NOPE)).astype(BF16)
    row = lambda g: g[l].reshape(1, -1).astype(F32)
    return dict(
        g_mix=row(g_mix), w1=w1, g_q=row(g_q_lora), wq=jnp.concatenate([q_pad, q_sw], axis=1).astype(BF16),
        g_kv=row(g_kv_lora), wk=wk.astype(BF16), we=we, wv=wv.astype(BF16),
        g_ret=g_ret_norm[l].astype(F32).reshape(RET_HEADS // 2, 1, LANES),
        w_out=w_out[l].astype(BF16), g_mem_q=row(g_mem_q), g_mem_kv=row(g_mem_kv),
        w_mem_q=w_mem_q[l].astype(BF16), w_mem_k=w_mem_k[l].astype(BF16), w_mem_v=w_mem_v[l].astype(BF16),
        w_mem_o=w_mem_o[l].astype(BF16), g_ffn=row(g_ffn), w_ffn_up=w_ffn_up[l].astype(BF16),
        w_ffn_conv=w_ffn_conv[l].astype(F32), b_ffn_conv=row(b_ffn_conv), w_ffn_down=w_ffn_down[l].astype(BF16))


def _rope_tables(pos):
    t = pos.shape[0]
    posf = pos.astype(F32)[:, None]

    def cos_sin(d):
        inv = ROPE_BASE ** (-jnp.arange(0, d, 2, dtype=F32) / d)
        ang = posf * inv[None, :]
        return jnp.cos(ang), jnp.sin(ang)

    c16, s16 = cos_sin(MLA_ROPE)
    c32, s32 = cos_sin(RET_DK)
    z = lambda n: jnp.zeros((t, n), F32)
    tail = LANES - MLA_NOPE - MLA_ROPE
    return dict(
        cslot=jnp.concatenate([jnp.ones((t, MLA_NOPE), F32), c16, c16, z(tail)], axis=1),
        sslot=jnp.concatenate([z(MLA_NOPE), s16, s16, z(tail)], axis=1),
        ck=jnp.concatenate([c16, c16, z(LANES - MLA_ROPE)], axis=1),
        sk=jnp.concatenate([s16, s16, z(LANES - MLA_ROPE)], axis=1),
        cr=jnp.tile(jnp.concatenate([c32, c32], axis=1), (1, RET_HEADS)),
        sr=jnp.tile(jnp.concatenate([s32, s32], axis=1), (1, RET_HEADS)))


def _pad_rows(a, axis, n):
    pad = [(0, 0)] * a.ndim
    pad[axis] = (0, n - a.shape[axis])
    return jnp.pad(a, pad)


def _to_pairs(a):
    B, T, _ = a.shape
    return jnp.transpose(a.reshape(B, T, 2, LANES), (0, 2, 1, 3))


def _layer(x, tabs, lw, mem_k, mem_v, past, g_final, *, q0, final):
    B, T, _ = x.shape
    (ckv, kpe, kpe_b, qcat, rq, rk, rv, rg, sq, sk, sv, sk_b, sv_b) = _in_proj(x, tabs, lw)
    if past is None:
        ckv_all, kpe_all, sk_all, sv_all = ckv, kpe_b, sk_b, sv_b
        s0 = jnp.zeros((B, RET_HEADS // 2, LANES, LANES), F32)
        conv0 = jnp.zeros((B, CONV_W - 1, 2 * D_FF), F32)
        kv_len = T
        lc = _tile(T, L_RET)
        tq_mla, tk_mla = _tile(T, TQ_MLA), _tile(T, TK_MLA)
        tq_sb, tk_sb = _tile(T, TQ_SB), _tile(T, TK_SB)
    else:
        ckv_c, kpe_c, s0, sk_c, sv_c, conv0 = past
        P = ckv_c.shape[1]
        kv_len = P + T
        tk_sb = TK_SB
        tk_pad = -(-kv_len // tk_sb) * tk_sb
        ckv_all = _pad_rows(jnp.concatenate([ckv_c.astype(F32), ckv], axis=1), 1, tk_pad)
        kpe_cb = _pad_cols(kpe_c.reshape(B * P, MLA_ROPE), LANES).reshape(B, P, LANES).astype(BF16)
        kpe_all = _pad_rows(jnp.concatenate([kpe_cb, kpe_b], axis=1), 1, tk_pad)
        sk_all = _pad_rows(jnp.concatenate([_to_pairs(sk_c.reshape(B, P, -1).astype(BF16)), sk_b], axis=2), 2, tk_pad)
        sv_all = _pad_rows(jnp.concatenate([_to_pairs(sv_c.reshape(B, P, -1).astype(BF16)), sv_b], axis=2), 2, tk_pad)
        s0 = _state_to_pairs(s0)
        conv0 = conv0.astype(F32)
        lc = T
        tq_mla, tk_mla = T, tk_pad
        tq_sb = T
    kcat, vmla = _kv_up(ckv_all, kpe_all, lw)
    o_mla = _mla_attention(qcat, kcat, vmla, q0=q0, kv_len=kv_len, tq=tq_mla, tk=tk_mla)
    o_sb = _sb_attention(sq, sk_all, sv_all, q0=q0, tq=tq_sb, tk=tk_sb)
    ret_y, s_new = _retention(rq, rk, rv, rg, s0, lw["g_ret"], lc=lc)
    x = _out_mem(x, o_mla, ret_y, o_sb, mem_k, mem_v, lw)
    x, conv_new = _conv_ffn(x, conv0, lw, g_final, final=final)
    state = (ckv, kpe, _pairs_to_state(s_new), sk.reshape(B, T, SB_HEADS, SB_HD), sv.reshape(B, T, SB_HEADS, SB_HD),
             conv_new)
    return x, state


def kernel(x_prompt, x_sample, cache_mla_ckv, cache_mla_kpe, state_ret, cache_sb_k, cache_sb_v, cache_mem_k, cache_mem_v, state_ffn_conv, mem_prompt, g_mix, w_in, g_q_lora, w_q_up, g_kv_lora, w_kv_up, g_ret_norm, w_out, g_mem_q, g_mem_kv, w_mem_q, w_mem_k, w_mem_v, w_mem_o, g_ffn, w_ffn_up, w_ffn_conv, b_ffn_conv, w_ffn_down, g_final):
    depth = w_in.shape[0]
    layers = [_prep_layer(l, g_mix, w_in, g_q_lora, w_q_up, g_kv_lora, w_kv_up, g_ret_norm, w_out,
                          g_mem_q, g_mem_kv, w_mem_q, w_mem_k, w_mem_v, w_mem_o,
                          g_ffn, w_ffn_up, w_ffn_conv, b_ffn_conv, w_ffn_down) for l in range(depth)]
    gfin = g_final.reshape(1, -1).astype(F32)
    mw = MEM_HEADS * MEM_HD

    xp = x_prompt
    Bp, Tp, _ = xp.shape
    tabs_p = _rope_tables(jnp.arange(Tp))
    p_states, p_mem_k, p_mem_v = [], [], []
    for l in range(depth):
        mk, mv, mk_b, mv_b = _mem_kv(mem_prompt, layers[l])
        xp, st = _layer(xp, tabs_p, layers[l], mk_b, mv_b, None, gfin, q0=0, final=(l == depth - 1))
        p_states.append(st)
        p_mem_k.append(mk.reshape(Bp, -1, MEM_HEADS, MEM_HD))
        p_mem_v.append(mv.reshape(Bp, -1, MEM_HEADS, MEM_HD))

    xs = x_sample
    Bs, Ts, _ = xs.shape
    past_len = cache_mla_ckv.shape[2]
    tabs_s = _rope_tables(past_len + jnp.arange(Ts))
    s_states = []
    for l in range(depth):
        past = (cache_mla_ckv[l], cache_mla_kpe[l], state_ret[l], cache_sb_k[l], cache_sb_v[l], state_ffn_conv[l])
        mk_b = cache_mem_k[l].reshape(Bs, -1, mw).astype(BF16)
        mv_b = cache_mem_v[l].reshape(Bs, -1, mw).astype(BF16)
        xs, st = _layer(xs, tabs_s, layers[l], mk_b, mv_b, past, gfin, q0=past_len, final=(l == depth - 1))
        s_states.append(st)

    p_ckv, p_kpe, p_ret, p_sbk, p_sbv, p_conv = [jnp.stack(f) for f in zip(*p_states)]
    s_ckv, s_kpe, s_ret, s_sbk, s_sbv, s_conv = [jnp.stack(f) for f in zip(*s_states)]
    return (xp, xs, p_ckv, p_kpe, p_ret, p_sbk, p_sbv, jnp.stack(p_mem_k), jnp.stack(p_mem_v), p_conv,
            s_ckv, s_kpe, s_ret, s_sbk, s_sbv, s_conv)
```

```python
import functools

import jax
import jax.numpy as jnp
from jax import lax
from jax.experimental import pallas as pl
from jax.experimental.pallas import tpu as pltpu

F32 = jnp.float32
BF16 = jnp.bfloat16

CHUNK = 64
EPS = 1e-6
ROPE_BASE = 10000.0

MLA_HEADS = 8
MLA_NOPE = 64
MLA_ROPE = 32
MLA_V = 64
MLA_Q_LORA = 256
MLA_KV_LORA = 128
MLA_SCALE = (MLA_NOPE + MLA_ROPE) ** -0.5
_LOG2E = 1.4426950408889634
_MLA_EXP2_SCALE = MLA_SCALE * _LOG2E
_SB_UNDERFLOW_LOG2 = -160.0
RET_HEADS = 4
RET_DK = 64
RET_DV = 64
RET_SCALE = RET_DK ** -0.5
SB_HEADS = 4
SB_HD = 64
SB_SCALE = SB_HD ** -0.5
MEM_HEADS = 4
MEM_HD = 128
MEM_SCALE = MEM_HD ** -0.5
D_FF = 2816
CONV_W = 3

IN_SIZES = (MLA_Q_LORA, MLA_KV_LORA, MLA_ROPE, RET_HEADS * RET_DK, RET_HEADS * RET_DK,
            RET_HEADS * RET_DV, RET_HEADS * RET_DV, SB_HEADS * SB_HD, SB_HEADS * SB_HD, SB_HEADS * SB_HD)

LANES = 128
HALF = LANES // 2
VMEM_LIMIT = 56 * 1024 * 1024

TM_PROJ = 512
TM_KVUP = 512
TQ_MLA = 512
TK_MLA = 512
TQ_SB = 256
TK_SB = 256
TC_RET = 512
L_RET = 256
TM_OUT = 512
TM_FFN = 512
TF_FFN = 256


def _tile(n, target):
    t = min(n, target)
    while n % t:
        t -= 1
    return t


def _cparams(n_axes):
    return pltpu.CompilerParams(dimension_semantics=("arbitrary",) * n_axes, vmem_limit_bytes=VMEM_LIMIT)


def _rms(x, g):
    return x * lax.rsqrt(jnp.mean(x * x, axis=-1, keepdims=True) + EPS) * g


def _lane_tile(x, reps):
    return jnp.concatenate([x] * reps, axis=1) if reps > 1 else x


def _full(shape):
    n = len(shape)
    return pl.BlockSpec(shape, lambda *_: (0,) * n)


def _dot(a, b):
    return jnp.dot(a, b, preferred_element_type=F32)


def _dot_nt(a, b):
    return lax.dot_general(a, b, (((1,), (1,)), ((), ())), preferred_element_type=F32)


_C_CQ = 0
_C_CKV = 256
_C_KPE = 384
_C_RQ = 512
_C_RK = 768
_C_RV = 1024
_C_RG = 1280
_C_SQ = 1536
_C_SK = 1792
_C_SV = 2048
_W1_COLS = 2304


def _rope(x, cos, sin_lo, sin_hi, half):
    return x * cos + pltpu.roll(x, half, 1) * sin_hi + pltpu.roll(x, LANES - half, 1) * sin_lo


def _kv_slots(c, kp, wke_ref, wv_ref, kcat_ref, v_ref):
    ck = jnp.concatenate([c, kp], axis=1)
    for p in range(MLA_HEADS // 2):
        k2 = _dot(ck, wke_ref[:, 2 * LANES * p:2 * LANES * (p + 1)])
        kcat_ref[0, 2 * p] = k2[:, :LANES].astype(BF16)
        kcat_ref[0, 2 * p + 1] = k2[:, LANES:].astype(BF16)
    for p in range(MLA_HEADS // 4):
        v2 = _dot(c, wv_ref[:, 2 * LANES * p:2 * LANES * (p + 1)])
        v_ref[0, 2 * p] = v2[:, :LANES].astype(BF16)
        v_ref[0, 2 * p + 1] = v2[:, LANES:].astype(BF16)


def _in_proj_body(x_ref, gmix_ref, w1_ref, gq_ref, wq_ref, gkv_ref, wke_ref, wv_ref,
                  cslot_ref, slo_slot_ref, shi_slot_ref, ck_ref, slo_k_ref, shi_k_ref, cr_ref, slo_r_ref, shi_r_ref,
                  ckv_ref, kpe_ref, kpeb_ref, qcat_ref, rq_ref, rk_ref, rv_ref, rg_ref,
                  sq_ref, sk_ref, sv_ref, skb_ref, svb_ref, *kv_refs):
    h = _rms(x_ref[0], gmix_ref[...]).astype(BF16)

    def proj(c0, width):
        return _dot(h, w1_ref[:, c0:c0 + width])

    cqn = _rms(proj(_C_CQ, 256), gq_ref[...]).astype(BF16)
    cslot, slo_slot, shi_slot = cslot_ref[...], slo_slot_ref[...], shi_slot_ref[...]
    for p in range(MLA_HEADS // 2):
        q2 = _dot(cqn, wq_ref[:, 2 * LANES * p:2 * LANES * (p + 1)])
        for a in range(2):
            qs = q2[:, a * LANES:(a + 1) * LANES]
            qcat_ref[0, 2 * p + a] = _rope(qs, cslot, slo_slot, shi_slot, MLA_ROPE // 2).astype(BF16)

    ckv = _rms(proj(_C_CKV, 128), gkv_ref[...])
    ckv_ref[0] = ckv
    kpe = _rope(proj(_C_KPE, 128), ck_ref[...], slo_k_ref[...], shi_k_ref[...], MLA_ROPE // 2)
    kpe_ref[0] = kpe[:, :MLA_ROPE]
    kpe_b = kpe.astype(BF16)
    kpeb_ref[0] = kpe_b
    if kv_refs:
        _kv_slots(ckv.astype(BF16), kpe_b, wke_ref, wv_ref, *kv_refs)

    cr, slo_r, shi_r = cr_ref[...], slo_r_ref[...], shi_r_ref[...]
    rq = proj(_C_RQ, 256)
    rk = proj(_C_RK, 256)
    rv = proj(_C_RV, 256)
    rg = proj(_C_RG, 256)
    sq = proj(_C_SQ, 256)
    sk = proj(_C_SK, 256)
    sv = proj(_C_SV, 256)
    sk_ref[0] = sk
    sv_ref[0] = sv
    for p in range(2):
        sl = slice(p * LANES, (p + 1) * LANES)
        rq_ref[0, p] = _rope(rq[:, sl], cr, slo_r, shi_r, RET_DK // 2).astype(BF16)
        rk_ref[0, p] = (_rope(rk[:, sl], cr, slo_r, shi_r, RET_DK // 2) * RET_SCALE).astype(BF16)
        rv_ref[0, p] = rv[:, sl].astype(BF16)
        rg_ref[0, p] = rg[:, sl]
        sq_ref[0, p] = (sq[:, sl] * SB_SCALE).astype(BF16)
        skb_ref[0, p] = sk[:, sl].astype(BF16)
        svb_ref[0, p] = sv[:, sl].astype(BF16)


def _in_proj(x, tabs, lw, *, fuse_kv):
    B, T, D = x.shape
    tm = _tile(T, TM_PROJ)
    grid = (B, T // tm)
    row = pl.BlockSpec((tm, LANES), lambda b, t: (t, 0))
    pair = lambda n: pl.BlockSpec((1, n, tm, LANES), lambda b, t: (b, 0, t, 0))
    nat = lambda w: pl.BlockSpec((1, tm, w), lambda b, t: (b, t, 0))
    in_specs = [nat(D), _full((1, D)), _full((D, _W1_COLS)), _full((1, MLA_Q_LORA)),
                _full((MLA_Q_LORA, MLA_HEADS * LANES)), _full((1, MLA_KV_LORA)),
                _full((MLA_KV_LORA + LANES, MLA_HEADS * LANES)), _full((MLA_KV_LORA, MLA_HEADS * MLA_V))] + [row] * 9
    out_shape = [
        jax.ShapeDtypeStruct((B, T, MLA_KV_LORA), F32),
        jax.ShapeDtypeStruct((B, T, MLA_ROPE), F32),
        jax.ShapeDtypeStruct((B, T, LANES), BF16),
        jax.ShapeDtypeStruct((B, MLA_HEADS, T, LANES), BF16),
        jax.ShapeDtypeStruct((B, 2, T, LANES), BF16),
        jax.ShapeDtypeStruct((B, 2, T, LANES), BF16),
        jax.ShapeDtypeStruct((B, 2, T, LANES), BF16),
        jax.ShapeDtypeStruct((B, 2, T, LANES), F32),
        jax.ShapeDtypeStruct((B, 2, T, LANES), BF16),
        jax.ShapeDtypeStruct((B, T, 256), F32),
        jax.ShapeDtypeStruct((B, T, 256), F32),
        jax.ShapeDtypeStruct((B, 2, T, LANES), BF16),
        jax.ShapeDtypeStruct((B, 2, T, LANES), BF16),
    ]
    out_specs = [nat(MLA_KV_LORA), nat(MLA_ROPE), nat(LANES), pair(MLA_HEADS), pair(2), pair(2), pair(2),
                 pair(2), pair(2), nat(256), nat(256), pair(2), pair(2)]
    if fuse_kv:
        out_shape += [jax.ShapeDtypeStruct((B, MLA_HEADS, T, LANES), BF16),
                      jax.ShapeDtypeStruct((B, MLA_HEADS // 2, T, LANES), BF16)]
        out_specs += [pair(MLA_HEADS), pair(MLA_HEADS // 2)]
    return pl.pallas_call(
        _in_proj_body, grid=grid, in_specs=in_specs, out_specs=out_specs, out_shape=out_shape,
        compiler_params=_cparams(2), name="in_proj",
    )(x, lw["g_mix"], lw["w1"], lw["g_q"], lw["wq"], lw["g_kv"], lw["wke"], lw["wv"],
      tabs["cslot"], tabs["slo_slot"], tabs["shi_slot"], tabs["ck"], tabs["slo_k"], tabs["shi_k"],
      tabs["cr"], tabs["slo_r"], tabs["shi_r"])


def _kv_up_body(ckv_ref, kpe_ref, wke_ref, wv_ref, kcat_ref, v_ref):
    _kv_slots(ckv_ref[0].astype(BF16), kpe_ref[0], wke_ref, wv_ref, kcat_ref, v_ref)


def _kv_up(ckv, kpe_pad, lw):
    B, T, _ = ckv.shape
    tm = _tile(T, TM_KVUP)
    nat = lambda w: pl.BlockSpec((1, tm, w), lambda b, t: (b, t, 0))
    pair = lambda n: pl.BlockSpec((1, n, tm, LANES), lambda b, t: (b, 0, t, 0))
    return pl.pallas_call(
        _kv_up_body, grid=(B, T // tm),
        in_specs=[nat(MLA_KV_LORA), nat(LANES), _full((MLA_KV_LORA + LANES, MLA_HEADS * LANES)),
                  _full((MLA_KV_LORA, MLA_HEADS * MLA_V))],
        out_specs=[pair(MLA_HEADS), pair(MLA_HEADS // 2)],
        out_shape=[jax.ShapeDtypeStruct((B, MLA_HEADS, T, LANES), BF16),
                   jax.ShapeDtypeStruct((B, MLA_HEADS // 2, T, LANES), BF16)],
        compiler_params=_cparams(2), name="kv_up",
    )(ckv, kpe_pad, lw["wke"], lw["wv"])


def _mla_last_block(i, *, q0, tq, tk, nk):
    last_q = q0 + i * tq + (tq - 1)
    last_key = (last_q // CHUNK) * CHUNK + (CHUNK - 1)
    return jnp.minimum(last_key // tk, nk - 1)


def _mla_body(q_ref, k_ref, v_ref, o_ref, m_ref, l_ref, acc_ref, *, q0, tq, tk, nk, kv_len):
    i = pl.program_id(1)
    j = pl.program_id(2)

    @pl.when(j == 0)
    def _():
        m_ref[...] = jnp.full(m_ref.shape, -jnp.inf, F32)
        l_ref[...] = jnp.zeros(l_ref.shape, F32)
        acc_ref[...] = jnp.zeros(acc_ref.shape, F32)

    def step(masked):
        lane = lax.broadcasted_iota(jnp.int32, (tq, LANES), 1)
        first = lane < HALF
        if masked:
            qpos = q0 + i * tq + lax.broadcasted_iota(jnp.int32, (tq, 1), 0)
            limit = jnp.minimum((jnp.right_shift(qpos, 6) + 1) * CHUNK, kv_len)
            kpos = j * tk + lax.broadcasted_iota(jnp.int32, (tq, tk), 1)
            bias = jnp.where(kpos < limit, 0.0, -jnp.inf).astype(F32)

        for p in range(MLA_HEADS // 2):
            vp = v_ref[0, p]
            alphas, pvs = [], []
            for a in range(2):
                hh = 2 * p + a
                s = _dot_nt(q_ref[0, hh], k_ref[0, hh])
                if masked:
                    s = s + bias
                m_prev = m_ref[hh]
                m_new = jnp.maximum(m_prev, jnp.max(s, axis=1, keepdims=True))
                e = jnp.exp2((s - _lane_tile(m_new, tk // LANES)) * _MLA_EXP2_SCALE)
                alpha = jnp.exp2((m_prev - m_new) * _MLA_EXP2_SCALE)
                l_ref[hh] = alpha * l_ref[hh] + jnp.sum(e, axis=1, keepdims=True)
                m_ref[hh] = m_new
                alphas.append(alpha)
                pvs.append(_dot(e.astype(BF16), vp))
            acc_ref[p] = acc_ref[p] * jnp.where(first, alphas[0], alphas[1]) + jnp.where(first, pvs[0], pvs[1])

    all_visible = jnp.minimum(((q0 + i * tq) // CHUNK + 1) * CHUNK, kv_len)
    fully_visible = (j + 1) * tk <= all_visible
    needed = j <= _mla_last_block(i, q0=q0, tq=tq, tk=tk, nk=nk)

    @pl.when(jnp.logical_and(needed, fully_visible))
    def _():
        step(False)

    @pl.when(jnp.logical_and(needed, jnp.logical_not(fully_visible)))
    def _():
        step(True)

    @pl.when(j == nk - 1)
    def _():
        lane = lax.broadcasted_iota(jnp.int32, (tq, LANES), 1)
        for p in range(MLA_HEADS // 2):
            inv = jnp.where(lane < HALF, 1.0 / l_ref[2 * p], 1.0 / l_ref[2 * p + 1])
            o_ref[0, p] = (acc_ref[p] * inv).astype(BF16)


def _mla_attention(qcat, kcat, v, *, q0, kv_len, tq, tk):
    B, H, Tq, _ = qcat.shape
    Tk = kcat.shape[2]
    nq, nk = Tq // tq, Tk // tk
    kw = dict(q0=q0, tq=tq, tk=tk, nk=nk)

    def kv_map(b, i, j):
        return (b, 0, jnp.minimum(j, _mla_last_block(i, **kw)), 0)

    return pl.pallas_call(
        functools.partial(_mla_body, kv_len=kv_len, **kw),
        grid=(B, nq, nk),
        in_specs=[pl.BlockSpec((1, H, tq, LANES), lambda b, i, j: (b, 0, i, 0)),
                  pl.BlockSpec((1, H, tk, LANES), kv_map),
                  pl.BlockSpec((1, H // 2, tk, LANES), kv_map)],
        out_specs=pl.BlockSpec((1, H // 2, tq, LANES), lambda b, i, j: (b, 0, i, 0)),
        out_shape=jax.ShapeDtypeStruct((B, H // 2, Tq, LANES), BF16),
        scratch_shapes=[pltpu.VMEM((H, tq, LANES), F32), pltpu.VMEM((H, tq, LANES), F32),
                        pltpu.VMEM((H // 2, tq, LANES), F32)],
        compiler_params=_cparams(3), name="mla_attn",
    )(qcat, kcat, v)


def _sb_last_block(i, *, q0, tq, tk, nk):
    last_key = q0 + i * tq + (tq - 1) - 1
    return jnp.clip(last_key // tk, 0, nk - 1)


def _sb_body(q_ref, k_ref, v_ref, u_ref, o_ref, run_ref, acc_ref, *, q0, tq, tk, nk):
    i = pl.program_id(1)
    q_first = q0 + i * tq
    last = _sb_last_block(i, q0=q0, tq=tq, tk=tk, nk=nk)
    first_unmasked = jnp.minimum(q_first // tk - 1, last)
    run_ref[...] = jnp.zeros(run_ref.shape, F32)
    acc_ref[...] = jnp.zeros(acc_ref.shape, F32)

    def step(kb, masked):
        first_k = lax.broadcasted_iota(jnp.int32, (tk, LANES), 1) < HALF
        tri = u_ref[...]
        reps = tk // LANES
        rows = pl.ds(pl.multiple_of(kb * tk, tk), tk)
        if masked:
            qpos = q_first + lax.broadcasted_iota(jnp.int32, (tq, 1), 0)
            kpos = kb * tk + lax.broadcasted_iota(jnp.int32, (tq, tk), 1)
            before1 = kpos < qpos
            before = jnp.concatenate([before1, before1], axis=1)

        for p in range(SB_HEADS // 2):
            kp = k_ref[0, p, rows, :]
            vp = v_ref[0, p, rows, :]
            zero = jnp.zeros_like(kp)
            k2 = jnp.concatenate([jnp.where(first_k, kp, zero), jnp.where(first_k, zero, kp)], axis=0)
            v2 = jnp.concatenate([jnp.where(first_k, vp, zero), jnp.where(first_k, zero, vp)], axis=0)
            z = _dot_nt(q_ref[0, p], k2) * _LOG2E
            nz = -z
            sp = jnp.log(1.0 + jnp.exp2(jnp.minimum(z, nz))) * _LOG2E
            log_keep = jnp.minimum(nz, 0.0) - sp
            log_beta = log_keep + z
            if masked:
                log_keep = jnp.where(before, log_keep, 0.0)
            hi = log_keep.astype(BF16)
            lo = (log_keep - hi.astype(F32)).astype(BF16)
            between = jnp.concatenate(
                [_dot(hi[:, :tk], tri) + _dot(lo[:, :tk], tri), _dot(hi[:, tk:], tri) + _dot(lo[:, tk:], tri)], axis=1)
            run_a = run_ref[2 * p]
            run_b = run_ref[2 * p + 1]
            run = jnp.concatenate([_lane_tile(run_a, reps), _lane_tile(run_b, reps)], axis=1)
            att = jnp.exp2(log_beta + between + run)
            if masked:
                att = jnp.where(before, att, 0.0)
            run_ref[2 * p] = run_a + jnp.sum(log_keep[:, :tk], axis=1, keepdims=True)
            run_ref[2 * p + 1] = run_b + jnp.sum(log_keep[:, tk:], axis=1, keepdims=True)
            acc_ref[p] = acc_ref[p] + _dot(att.astype(BF16), v2)

    def masked_step(t, carry):
        step(last - t, True)
        return carry

    lax.fori_loop(0, last - first_unmasked, masked_step, 0)

    def more(carry):
        kb, live = carry
        return jnp.logical_and(kb >= 0, live > 0)

    def unmasked_step(carry):
        kb, _ = carry
        step(kb, False)
        live = (jnp.max(run_ref[...]) >= _SB_UNDERFLOW_LOG2).astype(jnp.int32)
        return kb - 1, live

    lax.while_loop(more, unmasked_step, (first_unmasked, jnp.int32(1)))
    o_ref[0] = acc_ref[...].astype(BF16)


def _sb_attention(q, k, v, *, q0, tq, tk):
    B, P, Tq, _ = q.shape
    Tk = k.shape[2]
    nq, nk = Tq // tq, Tk // tk
    kw = dict(q0=q0, tq=tq, tk=tk, nk=nk)
    r = lax.broadcasted_iota(jnp.int32, (tk, tk), 0)
    c = lax.broadcasted_iota(jnp.int32, (tk, tk), 1)
    tri = (r > c).astype(BF16)

    kv_spec = pl.BlockSpec((1, P, Tk, LANES), lambda b, i: (b, 0, 0, 0))
    return pl.pallas_call(
        functools.partial(_sb_body, **kw),
        grid=(B, nq),
        in_specs=[pl.BlockSpec((1, P, tq, LANES), lambda b, i: (b, 0, i, 0)), kv_spec, kv_spec, _full((tk, tk))],
        out_specs=pl.BlockSpec((1, P, tq, LANES), lambda b, i: (b, 0, i, 0)),
        out_shape=jax.ShapeDtypeStruct((B, P, Tq, LANES), BF16),
        scratch_shapes=[pltpu.VMEM((2 * P, tq, LANES), F32), pltpu.VMEM((P, tq, LANES), F32)],
        compiler_params=_cparams(2), name="sb_attn",
    )(q, k, v, tri)


def _ret_body(q_ref, k_ref, v_ref, g_ref, s0_ref, dmask_ref, dq_ref, dk_ref, cd_ref, bd_ref, gn_ref,
              y_ref, snew_ref, state_ref, *, tc, lc, nc):
    c = pl.program_id(1)

    @pl.when(c == 0)
    def _():
        state_ref[...] = s0_ref[0]

    lane = lax.broadcasted_iota(jnp.int32, (lc, LANES), 1)
    first = lane < HALF
    for p in range(RET_HEADS // 2):
        for ch in range(tc // lc):
            rows = slice(ch * lc, (ch + 1) * lc)
            q = q_ref[0, p, rows, :]
            k = k_ref[0, p, rows, :]
            v = v_ref[0, p, rows, :]
            state = state_ref[p]
            cross = _dot((q.astype(F32) * dq_ref[p]).astype(BF16), state.astype(BF16))
            zero = jnp.zeros_like(q)
            inner = []
            for a in range(2):
                qa = jnp.where(first, q, zero) if a == 0 else jnp.where(first, zero, q)
                scores = _dot_nt(qa, k) * dmask_ref[2 * p + a]
                inner.append(_dot(scores.astype(BF16), v))
            o = jnp.where(first, inner[0], inner[1]) + cross
            kd = (k.astype(F32) * dk_ref[p]).astype(BF16)
            kv = lax.dot_general(kd, v, (((0,), (0,)), ((), ())), preferred_element_type=F32)
            state_ref[p] = cd_ref[p] * state + bd_ref[...] * kv
            zf = jnp.zeros_like(o)
            mu = jnp.where(first, jnp.sum(jnp.where(first, o, zf), axis=1, keepdims=True),
                           jnp.sum(jnp.where(first, zf, o), axis=1, keepdims=True)) * (1.0 / RET_DV)
            d = o - mu
            d2 = d * d
            var = jnp.where(first, jnp.sum(jnp.where(first, d2, zf), axis=1, keepdims=True),
                            jnp.sum(jnp.where(first, zf, d2), axis=1, keepdims=True)) * (1.0 / RET_DV)
            yn = d * lax.rsqrt(var + EPS) * gn_ref[p]
            gate = g_ref[0, p, rows, :]
            y_ref[0, p, rows, :] = (yn * (gate * jax.nn.sigmoid(gate))).astype(BF16)

    @pl.when(c == nc - 1)
    def _():
        snew_ref[0] = state_ref[...]


def _ret_tables(lc):
    log_gamma = jnp.log(1.0 - 2.0 ** (-5.0 - jnp.arange(RET_HEADS, dtype=F32)))
    idx = jnp.arange(lc, dtype=F32)
    diff = idx[:, None] - idx[None, :]
    dmask = jnp.where(diff[None] >= 0, jnp.exp(jnp.maximum(diff, 0.0)[None] * log_gamma[:, None, None]), 0.0)
    dk = jnp.exp((lc - 1.0 - idx)[:, None] * log_gamma[None, :])
    dq = jnp.exp((idx + 1.0)[:, None] * log_gamma[None, :])
    cd = jnp.exp(lc * log_gamma)

    def lanes(t):
        t = jnp.repeat(t[:, :, None], HALF, axis=2).reshape(t.shape[0], RET_HEADS // 2, LANES)
        return jnp.transpose(t, (1, 0, 2))

    blk = jnp.arange(LANES) // HALF
    bd = (blk[:, None] == blk[None, :]).astype(F32)
    cd_rows = jnp.repeat(cd, HALF).reshape(RET_HEADS // 2, LANES, 1)
    return dmask, lanes(dq), lanes(dk), cd_rows * bd[None], bd


def _state_to_pairs(s):
    B = s.shape[0]
    s = s.astype(F32).reshape(B, RET_HEADS // 2, 2, RET_DK, RET_DV)
    z = jnp.zeros_like(s[:, :, 0])
    top = jnp.concatenate([s[:, :, 0], z], axis=-1)
    bot = jnp.concatenate([z, s[:, :, 1]], axis=-1)
    return jnp.concatenate([top, bot], axis=-2)


def _pairs_to_state(sp):
    a = sp[:, :, :HALF, :HALF]
    b = sp[:, :, HALF:, HALF:]
    B = sp.shape[0]
    return jnp.stack([a, b], axis=2).reshape(B, RET_HEADS, RET_DK, RET_DV)


def _retention(rq, rk, rv, rg, s0_pairs, g_norm_pairs, *, lc):
    B, P, T, _ = rq.shape
    tc = _tile(T, TC_RET)
    nc = T // tc
    dmask, dq, dk, cd, bd = _ret_tables(lc)
    pair = pl.BlockSpec((1, P, tc, LANES), lambda b, c: (b, 0, c, 0))
    st = pl.BlockSpec((1, P, LANES, LANES), lambda b, c: (b, 0, 0, 0))
    return pl.pallas_call(
        functools.partial(_ret_body, tc=tc, lc=lc, nc=nc),
        grid=(B, nc),
        in_specs=[pair, pair, pair, pair, st, _full(dmask.shape), _full(dq.shape), _full(dk.shape),
                  _full(cd.shape), _full(bd.shape), _full(g_norm_pairs.shape)],
        out_specs=[pair, st],
        out_shape=[jax.ShapeDtypeStruct((B, P, T, LANES), BF16),
                   jax.ShapeDtypeStruct((B, P, LANES, LANES), F32)],
        scratch_shapes=[pltpu.VMEM((P, LANES, LANES), F32)],
        compiler_params=_cparams(2), name="retention",
    )(rq, rk, rv, rg, s0_pairs, dmask, dq, dk, cd, bd, g_norm_pairs)


def _out_mem_body(x_ref, omla_ref, ret_ref, osb_ref, wout_ref, gq_ref, wq_ref, mk_ref, mv_ref, wo_ref, y_ref):
    mixed = jnp.concatenate([ref[0, p] for ref, n in ((omla_ref, MLA_HEADS // 2), (ret_ref, RET_HEADS // 2),
                                                      (osb_ref, SB_HEADS // 2)) for p in range(n)], axis=1)
    x1 = x_ref[0] + _dot(mixed, wout_ref[...])
    hq = _rms(x1, gq_ref[...]).astype(BF16)
    q = _dot(hq, wq_ref[...]).astype(BF16)
    heads = []
    for hh in range(MEM_HEADS):
        sl = slice(hh * MEM_HD, (hh + 1) * MEM_HD)
        s = _dot_nt(q[:, sl], mk_ref[0, :, sl]) * MEM_SCALE
        e = jnp.exp(s - jnp.max(s, axis=1, keepdims=True))
        prob = e / jnp.sum(e, axis=1, keepdims=True)
        heads.append(_dot(prob.astype(BF16), mv_ref[0, :, sl]).astype(BF16))
    o = jnp.concatenate(heads, axis=1)
    y_ref[0] = x1 + _dot(o, wo_ref[...])


def _out_mem(x, o_mla, ret_y, o_sb, mem_k, mem_v, lw):
    B, T, D = x.shape
    tm = _tile(T, TM_OUT)
    M = mem_k.shape[1]
    mw = MEM_HEADS * MEM_HD
    nat = pl.BlockSpec((1, tm, D), lambda b, t: (b, t, 0))
    pair = lambda n: pl.BlockSpec((1, n, tm, LANES), lambda b, t: (b, 0, t, 0))
    mem = pl.BlockSpec((1, M, mw), lambda b, t: (b, 0, 0))
    return pl.pallas_call(
        _out_mem_body, grid=(B, T // tm),
        in_specs=[nat, pair(MLA_HEADS // 2), pair(RET_HEADS // 2), pair(SB_HEADS // 2), _full((D, D)),
                  _full((1, D)), _full((D, mw)), mem, mem, _full((mw, D))],
        out_specs=nat, out_shape=jax.ShapeDtypeStruct((B, T, D), F32),
        compiler_params=_cparams(2), name="out_mem",
    )(x, o_mla, ret_y, o_sb, lw["w_out"], lw["g_mem_q"], lw["w_mem_q"], mem_k, mem_v, lw["w_mem_o"])


def _mem_kv_body(mem_ref, g_ref, wk_ref, wv_ref, k_ref, v_ref, kb_ref, vb_ref):
    m = _rms(mem_ref[0], g_ref[...]).astype(BF16)
    k = _dot(m, wk_ref[...])
    v = _dot(m, wv_ref[...])
    k_ref[0] = k
    v_ref[0] = v
    kb_ref[0] = k.astype(BF16)
    vb_ref[0] = v.astype(BF16)


def _mem_kv(mem, lw):
    B, M, D = mem.shape
    mw = MEM_HEADS * MEM_HD
    o = pl.BlockSpec((1, M, mw), lambda b: (b, 0, 0))
    return pl.pallas_call(
        _mem_kv_body, grid=(B,),
        in_specs=[pl.BlockSpec((1, M, D), lambda b: (b, 0, 0)), _full((1, D)), _full((D, mw)), _full((D, mw))],
        out_specs=[o, o, o, o],
        out_shape=[jax.ShapeDtypeStruct((B, M, mw), F32), jax.ShapeDtypeStruct((B, M, mw), F32),
                   jax.ShapeDtypeStruct((B, M, mw), BF16), jax.ShapeDtypeStruct((B, M, mw), BF16)],
        compiler_params=_cparams(1), name="mem_kv",
    )(mem, lw["g_mem_kv"], lw["w_mem_k"], lw["w_mem_v"])


_CARRY_ROW = 8 - (CONV_W - 1)


def _ffn_body(x_ref, g_ref, wup_ref, wconv_ref, bconv_ref, wdown_ref, conv0_ref, gfin_ref,
              y_ref, convnew_ref, carry_ref, exta_ref, extb_ref, act_ref, *, tm, nt, final):
    t = pl.program_id(1)

    @pl.when(t == 0)
    def _():
        carry_ref[_CARRY_ROW:8, :] = conv0_ref[0]

    x = x_ref[0]
    h = _rms(x, g_ref[...]).astype(BF16)

    def conv_part(c0, ext_ref):
        cols = slice(c0, c0 + TF_FFN)
        u = _dot(h, wup_ref[:, cols])
        ext_ref[_CARRY_ROW:8, :] = carry_ref[_CARRY_ROW:8, cols]
        ext_ref[8:8 + tm, :] = u
        carry_ref[_CARRY_ROW:8, cols] = u[tm - (CONV_W - 1):, :]
        w = wconv_ref[:, cols]
        c = bconv_ref[:, cols]
        c = c + ext_ref[_CARRY_ROW:_CARRY_ROW + tm, :] * w[0:1]
        c = c + ext_ref[_CARRY_ROW + 1:_CARRY_ROW + 1 + tm, :] * w[1:2]
        return c + u * w[2:3]

    for ci in range(D_FF // TF_FFN):
        c0 = ci * TF_FFN
        a = conv_part(c0, exta_ref)
        b = conv_part(D_FF + c0, extb_ref)
        act_ref[:, c0:c0 + TF_FFN] = (a * jax.nn.sigmoid(a) * b).astype(BF16)

    x3 = x + _dot(act_ref[...], wdown_ref[...])
    y_ref[0] = _rms(x3, gfin_ref[...]) if final else x3

    @pl.when(t == nt - 1)
    def _():
        convnew_ref[0] = carry_ref[_CARRY_ROW:8, :]


def _conv_ffn(x, conv0, lw, g_final, *, final):
    B, T, D = x.shape
    tm = _tile(T, TM_FFN)
    nt = T // tm
    f2 = 2 * D_FF
    nat = pl.BlockSpec((1, tm, D), lambda b, t: (b, t, 0))
    cv = pl.BlockSpec((1, CONV_W - 1, f2), lambda b, t: (b, 0, 0))
    return pl.pallas_call(
        functools.partial(_ffn_body, tm=tm, nt=nt, final=final),
        grid=(B, nt),
        in_specs=[nat, _full((1, D)), _full((D, f2)), _full((CONV_W, f2)), _full((1, f2)), _full((D_FF, D)),
                  cv, _full((1, D))],
        out_specs=[nat, cv],
        out_shape=[jax.ShapeDtypeStruct((B, T, D), F32), jax.ShapeDtypeStruct((B, CONV_W - 1, f2), F32)],
        scratch_shapes=[pltpu.VMEM((8, f2), F32), pltpu.VMEM((8 + tm, TF_FFN), F32),
                        pltpu.VMEM((8 + tm, TF_FFN), F32), pltpu.VMEM((tm, D_FF), BF16)],
        compiler_params=_cparams(2), name="conv_ffn",
    )(x, lw["g_ffn"], lw["w_ffn_up"], lw["w_ffn_conv"], lw["b_ffn_conv"], lw["w_ffn_down"], conv0, g_final)


def _pad_cols(w, width):
    return jnp.pad(w, ((0, 0), (0, width - w.shape[1])))


def _prep_layer(l, g_mix, w_in, g_q_lora, w_q_up, g_kv_lora, w_kv_up, g_ret_norm, w_out,
                g_mem_q, g_mem_kv, w_mem_q, w_mem_k, w_mem_v, w_mem_o,
                g_ffn, w_ffn_up, w_ffn_conv, b_ffn_conv, w_ffn_down):
    w = w_in[l]
    parts, c0 = [], 0
    for n in IN_SIZES:
        parts.append(w[:, c0:c0 + n])
        c0 += n
    cq, ckv, kpe, rq, rk, rv, rg, sq, sk, sv = parts
    w1 = jnp.concatenate([cq, ckv, _pad_cols(kpe, LANES), rq, rk, rv, rg, sq, sk, sv], axis=1).astype(BF16)
    kq = w_q_up.shape[1]
    wq3 = w_q_up[l].reshape(kq, MLA_HEADS, MLA_NOPE + MLA_ROPE)
    tail = LANES - MLA_NOPE - MLA_ROPE
    q_pad = jnp.pad(wq3, ((0, 0), (0, 0), (0, tail))).reshape(kq, MLA_HEADS * LANES)
    wkv3 = w_kv_up[l].reshape(MLA_KV_LORA, MLA_HEADS, MLA_NOPE + MLA_V)
    wk = jnp.pad(wkv3[:, :, :MLA_NOPE], ((0, 0), (0, 0), (0, LANES - MLA_NOPE))).reshape(MLA_KV_LORA, MLA_HEADS * LANES)
    wv = wkv3[:, :, MLA_NOPE:].reshape(MLA_KV_LORA, MLA_HEADS * MLA_V)
    r = jnp.arange(LANES)[:, None]
    c = jnp.arange(MLA_HEADS * LANES)[None, :]
    we = ((r < MLA_ROPE) & (c % LANES == r + MLA_NOPE)).astype(BF16)
    row = lambda g: g[l].reshape(1, -1).astype(F32)
    return dict(
        g_mix=row(g_mix), w1=w1, g_q=row(g_q_lora), wq=q_pad.astype(BF16),
        g_kv=row(g_kv_lora), wke=jnp.concatenate([wk.astype(BF16), we], axis=0), wv=wv.astype(BF16),
        g_ret=g_ret_norm[l].astype(F32).reshape(RET_HEADS // 2, 1, LANES),
        w_out=w_out[l].astype(BF16), g_mem_q=row(g_mem_q), g_mem_kv=row(g_mem_kv),
        w_mem_q=w_mem_q[l].astype(BF16), w_mem_k=w_mem_k[l].astype(BF16), w_mem_v=w_mem_v[l].astype(BF16),
        w_mem_o=w_mem_o[l].astype(BF16), g_ffn=row(g_ffn), w_ffn_up=w_ffn_up[l].astype(BF16),
        w_ffn_conv=w_ffn_conv[l].astype(F32), b_ffn_conv=row(b_ffn_conv), w_ffn_down=w_ffn_down[l].astype(BF16))


def _rope_tables(pos):
    t = pos.shape[0]
    posf = pos.astype(F32)[:, None]

    def cos_sin(d):
        inv = ROPE_BASE ** (-jnp.arange(0, d, 2, dtype=F32) / d)
        ang = posf * inv[None, :]
        return jnp.cos(ang), jnp.sin(ang)

    c16, s16 = cos_sin(MLA_ROPE)
    c32, s32 = cos_sin(RET_DK)
    z = lambda n: jnp.zeros((t, n), F32)
    tail = LANES - MLA_NOPE - MLA_ROPE
    return dict(
        cslot=jnp.concatenate([jnp.ones((t, MLA_NOPE), F32), c16, c16, z(tail)], axis=1),
        slo_slot=jnp.concatenate([z(MLA_NOPE), -s16, z(MLA_ROPE // 2 + tail)], axis=1),
        shi_slot=jnp.concatenate([z(MLA_NOPE + MLA_ROPE // 2), s16, z(tail)], axis=1),
        ck=jnp.concatenate([c16, c16, z(LANES - MLA_ROPE)], axis=1),
        slo_k=jnp.concatenate([-s16, z(LANES - MLA_ROPE // 2)], axis=1),
        shi_k=jnp.concatenate([z(MLA_ROPE // 2), s16, z(LANES - MLA_ROPE)], axis=1),
        cr=jnp.tile(jnp.concatenate([c32, c32], axis=1), (1, LANES // RET_DK)),
        slo_r=jnp.tile(jnp.concatenate([-s32, z(RET_DK // 2)], axis=1), (1, LANES // RET_DK)),
        shi_r=jnp.tile(jnp.concatenate([z(RET_DK // 2), s32], axis=1), (1, LANES // RET_DK)))


def _pad_rows(a, axis, n):
    pad = [(0, 0)] * a.ndim
    pad[axis] = (0, n - a.shape[axis])
    return jnp.pad(a, pad)


def _to_pairs(a):
    B, T, _ = a.shape
    return jnp.transpose(a.reshape(B, T, 2, LANES), (0, 2, 1, 3))


def _layer(x, tabs, lw, mem_k, mem_v, past, g_final, *, q0, final):
    B, T, _ = x.shape
    outs = _in_proj(x, tabs, lw, fuse_kv=past is None)
    (ckv, kpe, kpe_b, qcat, rq, rk, rv, rg, sq, sk, sv, sk_b, sv_b) = outs[:13]
    if past is None:
        kcat, vmla = outs[13:]
        sk_all, sv_all = sk_b, sv_b
        s0 = jnp.zeros((B, RET_HEADS // 2, LANES, LANES), F32)
        conv0 = jnp.zeros((B, CONV_W - 1, 2 * D_FF), F32)
        kv_len = T
        lc = _tile(T, L_RET)
        tq_mla, tk_mla = _tile(T, TQ_MLA), _tile(T, TK_MLA)
        tq_sb, tk_sb = _tile(T, TQ_SB), _tile(T, TK_SB)
    else:
        ckv_c, kpe_c, s0, sk_c, sv_c, conv0 = past
        P = ckv_c.shape[1]
        kv_len = P + T
        tk_sb = TK_SB
        tk_pad = -(-kv_len // tk_sb) * tk_sb
        ckv_all = _pad_rows(jnp.concatenate([ckv_c.astype(F32), ckv], axis=1), 1, tk_pad)
        kpe_cb = _pad_cols(kpe_c.reshape(B * P, MLA_ROPE), LANES).reshape(B, P, LANES).astype(BF16)
        kpe_all = _pad_rows(jnp.concatenate([kpe_cb, kpe_b], axis=1), 1, tk_pad)
        sk_all = _pad_rows(jnp.concatenate([_to_pairs(sk_c.reshape(B, P, -1).astype(BF16)), sk_b], axis=2), 2, tk_pad)
        sv_all = _pad_rows(jnp.concatenate([_to_pairs(sv_c.reshape(B, P, -1).astype(BF16)), sv_b], axis=2), 2, tk_pad)
        s0 = _state_to_pairs(s0)
        conv0 = conv0.astype(F32)
        lc = T
        tq_mla, tk_mla = T, tk_pad
        tq_sb = T
        kcat, vmla = _kv_up(ckv_all, kpe_all, lw)
    o_mla = _mla_attention(qcat, kcat, vmla, q0=q0, kv_len=kv_len, tq=tq_mla, tk=tk_mla)
    o_sb = _sb_attention(sq, sk_all, sv_all, q0=q0, tq=tq_sb, tk=tk_sb)
    ret_y, s_new = _retention(rq, rk, rv, rg, s0, lw["g_ret"], lc=lc)
    x = _out_mem(x, o_mla, ret_y, o_sb, mem_k, mem_v, lw)
    x, conv_new = _conv_ffn(x, conv0, lw, g_final, final=final)
    state = (ckv, kpe, _pairs_to_state(s_new), sk.reshape(B, T, SB_HEADS, SB_HD), sv.reshape(B, T, SB_HEADS, SB_HD),
             conv_new)
    return x, state


def kernel(x_prompt, x_sample, cache_mla_ckv, cache_mla_kpe, state_ret, cache_sb_k, cache_sb_v, cache_mem_k, cache_mem_v, state_ffn_conv, mem_prompt, g_mix, w_in, g_q_lora, w_q_up, g_kv_lora, w_kv_up, g_ret_norm, w_out, g_mem_q, g_mem_kv, w_mem_q, w_mem_k, w_mem_v, w_mem_o, g_ffn, w_ffn_up, w_ffn_conv, b_ffn_conv, w_ffn_down, g_final):
    depth = w_in.shape[0]
    layers = [_prep_layer(l, g_mix, w_in, g_q_lora, w_q_up, g_kv_lora, w_kv_up, g_ret_norm, w_out,
                          g_mem_q, g_mem_kv, w_mem_q, w_mem_k, w_mem_v, w_mem_o,
                          g_ffn, w_ffn_up, w_ffn_conv, b_ffn_conv, w_ffn_down) for l in range(depth)]
    gfin = g_final.reshape(1, -1).astype(F32)
    mw = MEM_HEADS * MEM_HD

    xp = x_prompt
    Bp, Tp, _ = xp.shape
    tabs_p = _rope_tables(jnp.arange(Tp))
    p_states, p_mem_k, p_mem_v = [], [], []
    for l in range(depth):
        mk, mv, mk_b, mv_b = _mem_kv(mem_prompt, layers[l])
        xp, st = _layer(xp, tabs_p, layers[l], mk_b, mv_b, None, gfin, q0=0, final=(l == depth - 1))
        p_states.append(st)
        p_mem_k.append(mk.reshape(Bp, -1, MEM_HEADS, MEM_HD))
        p_mem_v.append(mv.reshape(Bp, -1, MEM_HEADS, MEM_HD))

    xs = x_sample
    Bs, Ts, _ = xs.shape
    past_len = cache_mla_ckv.shape[2]
    tabs_s = _rope_tables(past_len + jnp.arange(Ts))
    s_states = []
    for l in range(depth):
        past = (cache_mla_ckv[l], cache_mla_kpe[l], state_ret[l], cache_sb_k[l], cache_sb_v[l], state_ffn_conv[l])
        mk_b = cache_mem_k[l].reshape(Bs, -1, mw).astype(BF16)
        mv_b = cache_mem_v[l].reshape(Bs, -1, mw).astype(BF16)
        xs, st = _layer(xs, tabs_s, layers[l], mk_b, mv_b, past, gfin, q0=past_len, final=(l == depth - 1))
        s_states.append(st)

    p_ckv, p_kpe, p_ret, p_sbk, p_sbv, p_conv = [jnp.stack(f) for f in zip(*p_states)]
    s_ckv, s_kpe, s_ret, s_sbk, s_sbv, s_conv = [jnp.stack(f) for f in zip(*s_states)]
    return (xp, xs, p_ckv, p_kpe, p_ret, p_sbk, p_sbv, jnp.stack(p_mem_k), jnp.stack(p_mem_v), p_conv,
            s_ckv, s_kpe, s_ret, s_sbk, s_sbv, s_conv)
```

```python
import functools

import jax
import jax.numpy as jnp
from jax import lax
from jax.experimental import pallas as pl
from jax.experimental.pallas import tpu as pltpu

F32 = jnp.float32
BF16 = jnp.bfloat16

CHUNK = 64
EPS = 1e-6
ROPE_BASE = 10000.0

MLA_HEADS = 8
MLA_NOPE = 64
MLA_ROPE = 32
MLA_V = 64
MLA_Q_LORA = 256
MLA_KV_LORA = 128
MLA_SCALE = (MLA_NOPE + MLA_ROPE) ** -0.5
_LOG2E = 1.4426950408889634
_MLA_EXP2_SCALE = MLA_SCALE * _LOG2E
_SB_UNDERFLOW_LOG2 = -160.0
RET_HEADS = 4
RET_DK = 64
RET_DV = 64
RET_SCALE = RET_DK ** -0.5
SB_HEADS = 4
SB_HD = 64
SB_SCALE = SB_HD ** -0.5
MEM_HEADS = 4
MEM_HD = 128
MEM_SCALE = MEM_HD ** -0.5
D_FF = 2816
CONV_W = 3

IN_SIZES = (MLA_Q_LORA, MLA_KV_LORA, MLA_ROPE, RET_HEADS * RET_DK, RET_HEADS * RET_DK,
            RET_HEADS * RET_DV, RET_HEADS * RET_DV, SB_HEADS * SB_HD, SB_HEADS * SB_HD, SB_HEADS * SB_HD)

LANES = 128
HALF = LANES // 2
VMEM_LIMIT = 56 * 1024 * 1024

TM_PROJ = 512
TM_KVUP = 512
TQ_MLA = 512
TK_MLA = 512
TQ_SB = 256
TK_SB = 256
TC_RET = 512
L_RET = 256
TM_OUT = 512
TM_FFN = 512
TF_FFN = 256


def _tile(n, target):
    t = min(n, target)
    while n % t:
        t -= 1
    return t


def _cparams(n_axes):
    return pltpu.CompilerParams(dimension_semantics=("arbitrary",) * n_axes, vmem_limit_bytes=VMEM_LIMIT)


def _rms(x, g):
    return x * lax.rsqrt(jnp.mean(x * x, axis=-1, keepdims=True) + EPS) * g


def _lane_tile(x, reps):
    return jnp.concatenate([x] * reps, axis=1) if reps > 1 else x


def _full(shape):
    n = len(shape)
    return pl.BlockSpec(shape, lambda *_: (0,) * n)


def _dot(a, b):
    return jnp.dot(a, b, preferred_element_type=F32)


def _dot_nt(a, b):
    return lax.dot_general(a, b, (((1,), (1,)), ((), ())), preferred_element_type=F32)


_C_CQ = 0
_C_CKV = 256
_C_KPE = 384
_C_RQ = 512
_C_RK = 768
_C_RV = 1024
_C_RG = 1280
_C_SQ = 1536
_C_SK = 1792
_C_SV = 2048
_W1_COLS = 2304


def _rope(x, cos, sin_lo, sin_hi, half):
    return x * cos + pltpu.roll(x, half, 1) * sin_hi + pltpu.roll(x, LANES - half, 1) * sin_lo


def _kv_slots(c, kp, wke_ref, wv_ref, kcat_ref, v_ref):
    ck = jnp.concatenate([c, kp], axis=1)
    for p in range(MLA_HEADS // 2):
        k2 = _dot(ck, wke_ref[:, 2 * LANES * p:2 * LANES * (p + 1)])
        kcat_ref[0, 2 * p] = k2[:, :LANES].astype(BF16)
        kcat_ref[0, 2 * p + 1] = k2[:, LANES:].astype(BF16)
    for p in range(MLA_HEADS // 4):
        v2 = _dot(c, wv_ref[:, 2 * LANES * p:2 * LANES * (p + 1)])
        v_ref[0, 2 * p] = v2[:, :LANES].astype(BF16)
        v_ref[0, 2 * p + 1] = v2[:, LANES:].astype(BF16)


def _in_proj_body(x_ref, gmix_ref, w1_ref, gq_ref, wq_ref, gkv_ref, wke_ref, wv_ref,
                  cslot_ref, slo_slot_ref, shi_slot_ref, ck_ref, slo_k_ref, shi_k_ref, cr_ref, slo_r_ref, shi_r_ref,
                  ckv_ref, kpe_ref, kpeb_ref, qcat_ref, rq_ref, rk_ref, rv_ref, rg_ref,
                  sq_ref, sk_ref, sv_ref, skb_ref, svb_ref, *kv_refs):
    h = _rms(x_ref[0], gmix_ref[...]).astype(BF16)

    def proj(c0, width):
        return _dot(h, w1_ref[:, c0:c0 + width])

    cqn = _rms(proj(_C_CQ, 256), gq_ref[...]).astype(BF16)
    cslot, slo_slot, shi_slot = cslot_ref[...], slo_slot_ref[...], shi_slot_ref[...]
    for p in range(MLA_HEADS // 2):
        q2 = _dot(cqn, wq_ref[:, 2 * LANES * p:2 * LANES * (p + 1)])
        for a in range(2):
            qs = q2[:, a * LANES:(a + 1) * LANES]
            qcat_ref[0, 2 * p + a] = _rope(qs, cslot, slo_slot, shi_slot, MLA_ROPE // 2).astype(BF16)

    ckv = _rms(proj(_C_CKV, 128), gkv_ref[...])
    ckv_ref[0] = ckv
    kpe = _rope(proj(_C_KPE, 128), ck_ref[...], slo_k_ref[...], shi_k_ref[...], MLA_ROPE // 2)
    kpe_ref[0] = kpe[:, :MLA_ROPE]
    kpe_b = kpe.astype(BF16)
    kpeb_ref[0] = kpe_b
    if kv_refs:
        _kv_slots(ckv.astype(BF16), kpe_b, wke_ref, wv_ref, *kv_refs)

    cr, slo_r, shi_r = cr_ref[...], slo_r_ref[...], shi_r_ref[...]
    rq = proj(_C_RQ, 256)
    rk = proj(_C_RK, 256)
    rv = proj(_C_RV, 256)
    rg = proj(_C_RG, 256)
    sq = proj(_C_SQ, 256)
    sk = proj(_C_SK, 256)
    sv = proj(_C_SV, 256)
    sk_ref[0] = sk
    sv_ref[0] = sv
    for p in range(2):
        sl = slice(p * LANES, (p + 1) * LANES)
        rq_ref[0, p] = _rope(rq[:, sl], cr, slo_r, shi_r, RET_DK // 2).astype(BF16)
        rk_ref[0, p] = (_rope(rk[:, sl], cr, slo_r, shi_r, RET_DK // 2) * RET_SCALE).astype(BF16)
        rv_ref[0, p] = rv[:, sl].astype(BF16)
        rg_ref[0, p] = rg[:, sl]
        sq_ref[0, p] = (sq[:, sl] * SB_SCALE).astype(BF16)
        skb_ref[0, p] = sk[:, sl].astype(BF16)
        svb_ref[0, p] = sv[:, sl].astype(BF16)


def _in_proj(x, tabs, lw, *, fuse_kv):
    B, T, D = x.shape
    tm = _tile(T, TM_PROJ)
    grid = (B, T // tm)
    row = pl.BlockSpec((tm, LANES), lambda b, t: (t, 0))
    pair = lambda n: pl.BlockSpec((1, n, tm, LANES), lambda b, t: (b, 0, t, 0))
    nat = lambda w: pl.BlockSpec((1, tm, w), lambda b, t: (b, t, 0))
    in_specs = [nat(D), _full((1, D)), _full((D, _W1_COLS)), _full((1, MLA_Q_LORA)),
                _full((MLA_Q_LORA, MLA_HEADS * LANES)), _full((1, MLA_KV_LORA)),
                _full((MLA_KV_LORA + LANES, MLA_HEADS * LANES)), _full((MLA_KV_LORA, MLA_HEADS * MLA_V))] + [row] * 9
    out_shape = [
        jax.ShapeDtypeStruct((B, T, MLA_KV_LORA), F32),
        jax.ShapeDtypeStruct((B, T, MLA_ROPE), F32),
        jax.ShapeDtypeStruct((B, T, LANES), BF16),
        jax.ShapeDtypeStruct((B, MLA_HEADS, T, LANES), BF16),
        jax.ShapeDtypeStruct((B, 2, T, LANES), BF16),
        jax.ShapeDtypeStruct((B, 2, T, LANES), BF16),
        jax.ShapeDtypeStruct((B, 2, T, LANES), BF16),
        jax.ShapeDtypeStruct((B, 2, T, LANES), F32),
        jax.ShapeDtypeStruct((B, 2, T, LANES), BF16),
        jax.ShapeDtypeStruct((B, T, 256), F32),
        jax.ShapeDtypeStruct((B, T, 256), F32),
        jax.ShapeDtypeStruct((B, 2, T, LANES), BF16),
        jax.ShapeDtypeStruct((B, 2, T, LANES), BF16),
    ]
    out_specs = [nat(MLA_KV_LORA), nat(MLA_ROPE), nat(LANES), pair(MLA_HEADS), pair(2), pair(2), pair(2),
                 pair(2), pair(2), nat(256), nat(256), pair(2), pair(2)]
    if fuse_kv:
        out_shape += [jax.ShapeDtypeStruct((B, MLA_HEADS, T, LANES), BF16),
                      jax.ShapeDtypeStruct((B, MLA_HEADS // 2, T, LANES), BF16)]
        out_specs += [pair(MLA_HEADS), pair(MLA_HEADS // 2)]
    return pl.pallas_call(
        _in_proj_body, grid=grid, in_specs=in_specs, out_specs=out_specs, out_shape=out_shape,
        compiler_params=_cparams(2), name="in_proj",
    )(x, lw["g_mix"], lw["w1"], lw["g_q"], lw["wq"], lw["g_kv"], lw["wke"], lw["wv"],
      tabs["cslot"], tabs["slo_slot"], tabs["shi_slot"], tabs["ck"], tabs["slo_k"], tabs["shi_k"],
      tabs["cr"], tabs["slo_r"], tabs["shi_r"])


def _kv_up_body(ckv_ref, kpe_ref, wke_ref, wv_ref, kcat_ref, v_ref):
    _kv_slots(ckv_ref[0].astype(BF16), kpe_ref[0], wke_ref, wv_ref, kcat_ref, v_ref)


def _kv_up(ckv, kpe_pad, lw):
    B, T, _ = ckv.shape
    tm = _tile(T, TM_KVUP)
    nat = lambda w: pl.BlockSpec((1, tm, w), lambda b, t: (b, t, 0))
    pair = lambda n: pl.BlockSpec((1, n, tm, LANES), lambda b, t: (b, 0, t, 0))
    return pl.pallas_call(
        _kv_up_body, grid=(B, T // tm),
        in_specs=[nat(MLA_KV_LORA), nat(LANES), _full((MLA_KV_LORA + LANES, MLA_HEADS * LANES)),
                  _full((MLA_KV_LORA, MLA_HEADS * MLA_V))],
        out_specs=[pair(MLA_HEADS), pair(MLA_HEADS // 2)],
        out_shape=[jax.ShapeDtypeStruct((B, MLA_HEADS, T, LANES), BF16),
                   jax.ShapeDtypeStruct((B, MLA_HEADS // 2, T, LANES), BF16)],
        compiler_params=_cparams(2), name="kv_up",
    )(ckv, kpe_pad, lw["wke"], lw["wv"])


def _mla_last_block(i, *, q0, tq, tk, nk):
    last_q = q0 + i * tq + (tq - 1)
    last_key = (last_q // CHUNK) * CHUNK + (CHUNK - 1)
    return jnp.minimum(last_key // tk, nk - 1)


_MLA_QK_AHEAD = 2


def _mla_body(q_ref, k_ref, v_ref, o_ref, m_ref, l_ref, acc_ref, *, q0, tq, tk, nk, kv_len):
    i = pl.program_id(1)
    j = pl.program_id(2)

    @pl.when(j == 0)
    def _():
        m_ref[...] = jnp.full(m_ref.shape, -jnp.inf, F32)
        l_ref[...] = jnp.zeros(l_ref.shape, F32)
        acc_ref[...] = jnp.zeros(acc_ref.shape, F32)

    def step(masked):
        if masked:
            qpos = q0 + i * tq + lax.broadcasted_iota(jnp.int32, (1, tq), 1)
            limit = jnp.minimum((jnp.right_shift(qpos, 6) + 1) * CHUNK, kv_len)
            kpos = j * tk + lax.broadcasted_iota(jnp.int32, (tk, tq), 0)
            bias = jnp.where(kpos < limit, 0.0, -jnp.inf).astype(F32)

        def scores(hh):
            s = _dot_nt(k_ref[0, hh], q_ref[0, hh])
            return s + bias if masked else s

        pending = [scores(hh) for hh in range(_MLA_QK_AHEAD)]
        halves = []
        for hh in range(MLA_HEADS):
            p, a = divmod(hh, 2)
            s = pending.pop(0)
            if hh + _MLA_QK_AHEAD < MLA_HEADS:
                pending.append(scores(hh + _MLA_QK_AHEAD))
            m_prev = m_ref[hh]
            m_new = jnp.maximum(m_prev, jnp.max(s, axis=0, keepdims=True))
            e = jnp.exp2((s - m_new) * _MLA_EXP2_SCALE).astype(BF16)
            alpha = jnp.exp2((m_prev - m_new) * _MLA_EXP2_SCALE)
            l_ref[hh] = alpha * l_ref[hh] + jnp.sum(e.astype(F32), axis=0, keepdims=True)
            m_ref[hh] = m_new
            pv = lax.dot_general(v_ref[0, p], e, (((0,), (0,)), ((), ())), preferred_element_type=F32)
            rows = slice(a * HALF, (a + 1) * HALF)
            halves.append(acc_ref[p, rows, :] * alpha + pv[rows])
            if a == 1:
                acc_ref[p] = jnp.concatenate(halves, axis=0)
                halves = []

    all_visible = jnp.minimum(((q0 + i * tq) // CHUNK + 1) * CHUNK, kv_len)
    fully_visible = (j + 1) * tk <= all_visible
    needed = j <= _mla_last_block(i, q0=q0, tq=tq, tk=tk, nk=nk)

    @pl.when(jnp.logical_and(needed, fully_visible))
    def _():
        step(False)

    @pl.when(jnp.logical_and(needed, jnp.logical_not(fully_visible)))
    def _():
        step(True)

    @pl.when(j == nk - 1)
    def _():
        for p in range(MLA_HEADS // 2):
            inv = jnp.concatenate([jnp.broadcast_to(1.0 / l_ref[2 * p], (HALF, tq)),
                                   jnp.broadcast_to(1.0 / l_ref[2 * p + 1], (HALF, tq))], axis=0)
            o_ref[0, p] = jnp.transpose(acc_ref[p] * inv).astype(BF16)


def _mla_attention(qcat, kcat, v, *, q0, kv_len, tq, tk):
    B, H, Tq, _ = qcat.shape
    Tk = kcat.shape[2]
    nq, nk = Tq // tq, Tk // tk
    kw = dict(q0=q0, tq=tq, tk=tk, nk=nk)

    def kv_map(b, i, j):
        return (b, 0, jnp.minimum(j, _mla_last_block(i, **kw)), 0)

    return pl.pallas_call(
        functools.partial(_mla_body, kv_len=kv_len, **kw),
        grid=(B, nq, nk),
        in_specs=[pl.BlockSpec((1, H, tq, LANES), lambda b, i, j: (b, 0, i, 0)),
                  pl.BlockSpec((1, H, tk, LANES), kv_map),
                  pl.BlockSpec((1, H // 2, tk, LANES), kv_map)],
        out_specs=pl.BlockSpec((1, H // 2, tq, LANES), lambda b, i, j: (b, 0, i, 0)),
        out_shape=jax.ShapeDtypeStruct((B, H // 2, Tq, LANES), BF16),
        scratch_shapes=[pltpu.VMEM((H, 1, tq), F32), pltpu.VMEM((H, 1, tq), F32),
                        pltpu.VMEM((H // 2, LANES, tq), F32)],
        compiler_params=_cparams(3), name="mla_attn",
    )(qcat, kcat, v)


def _sb_last_block(i, *, q0, tq, tk, nk):
    last_key = q0 + i * tq + (tq - 1) - 1
    return jnp.clip(last_key // tk, 0, nk - 1)


def _sb_body(q_ref, k_ref, v_ref, u_ref, o_ref, run_ref, acc_ref, *, q0, tq, tk, nk):
    i = pl.program_id(1)
    q_first = q0 + i * tq
    last = _sb_last_block(i, q0=q0, tq=tq, tk=tk, nk=nk)
    first_unmasked = jnp.minimum(q_first // tk - 1, last)
    run_ref[...] = jnp.zeros(run_ref.shape, F32)
    acc_ref[...] = jnp.zeros(acc_ref.shape, F32)

    def step(kb, masked):
        first_k = lax.broadcasted_iota(jnp.int32, (tk, LANES), 1) < HALF
        tri = u_ref[...]
        reps = tk // LANES
        rows = pl.ds(pl.multiple_of(kb * tk, tk), tk)
        if masked:
            qpos = q_first + lax.broadcasted_iota(jnp.int32, (tq, 1), 0)
            kpos = kb * tk + lax.broadcasted_iota(jnp.int32, (tq, tk), 1)
            before1 = kpos < qpos
            before = jnp.concatenate([before1, before1], axis=1)

        def scores(p):
            kp = k_ref[0, p, rows, :]
            zero = jnp.zeros_like(kp)
            k2 = jnp.concatenate([jnp.where(first_k, kp, zero), jnp.where(first_k, zero, kp)], axis=0)
            return _dot_nt(q_ref[0, p], k2) * _LOG2E

        npair = SB_HEADS // 2
        zs = [scores(p) for p in range(npair)]
        keeps, betas, betweens = [], [], []
        for p in range(npair):
            z = zs[p]
            nz = -z
            sp = jnp.log(1.0 + jnp.exp2(jnp.minimum(z, nz))) * _LOG2E
            log_keep = jnp.minimum(nz, 0.0) - sp
            betas.append(log_keep + z)
            if masked:
                log_keep = jnp.where(before, log_keep, 0.0)
            keeps.append(log_keep)
            hi = log_keep.astype(BF16)
            lo = (log_keep - hi.astype(F32)).astype(BF16)
            betweens.append(jnp.concatenate(
                [_dot(hi[:, :tk], tri) + _dot(lo[:, :tk], tri), _dot(hi[:, tk:], tri) + _dot(lo[:, tk:], tri)], axis=1))
        for p in range(npair):
            vp = v_ref[0, p, rows, :]
            zero = jnp.zeros_like(vp)
            v2 = jnp.concatenate([jnp.where(first_k, vp, zero), jnp.where(first_k, zero, vp)], axis=0)
            run_a = run_ref[2 * p]
            run_b = run_ref[2 * p + 1]
            run = jnp.concatenate([_lane_tile(run_a, reps), _lane_tile(run_b, reps)], axis=1)
            att = jnp.exp2(betas[p] + betweens[p] + run)
            if masked:
                att = jnp.where(before, att, 0.0)
            run_ref[2 * p] = run_a + jnp.sum(keeps[p][:, :tk], axis=1, keepdims=True)
            run_ref[2 * p + 1] = run_b + jnp.sum(keeps[p][:, tk:], axis=1, keepdims=True)
            acc_ref[p] = acc_ref[p] + _dot(att.astype(BF16), v2)

    def masked_step(t, carry):
        step(last - t, True)
        return carry

    lax.fori_loop(0, last - first_unmasked, masked_step, 0)

    def more(carry):
        kb, live = carry
        return jnp.logical_and(kb >= 0, live > 0)

    def unmasked_step(carry):
        kb, _ = carry
        step(kb, False)
        live = (jnp.max(run_ref[...]) >= _SB_UNDERFLOW_LOG2).astype(jnp.int32)
        return kb - 1, live

    lax.while_loop(more, unmasked_step, (first_unmasked, jnp.int32(1)))
    o_ref[0] = acc_ref[...].astype(BF16)


def _sb_attention(q, k, v, *, q0, tq, tk):
    B, P, Tq, _ = q.shape
    Tk = k.shape[2]
    nq, nk = Tq // tq, Tk // tk
    kw = dict(q0=q0, tq=tq, tk=tk, nk=nk)
    r = lax.broadcasted_iota(jnp.int32, (tk, tk), 0)
    c = lax.broadcasted_iota(jnp.int32, (tk, tk), 1)
    tri = (r > c).astype(BF16)

    kv_spec = pl.BlockSpec((1, P, Tk, LANES), lambda b, i: (b, 0, 0, 0))
    return pl.pallas_call(
        functools.partial(_sb_body, **kw),
        grid=(B, nq),
        in_specs=[pl.BlockSpec((1, P, tq, LANES), lambda b, i: (b, 0, i, 0)), kv_spec, kv_spec, _full((tk, tk))],
        out_specs=pl.BlockSpec((1, P, tq, LANES), lambda b, i: (b, 0, i, 0)),
        out_shape=jax.ShapeDtypeStruct((B, P, Tq, LANES), BF16),
        scratch_shapes=[pltpu.VMEM((2 * P, tq, LANES), F32), pltpu.VMEM((P, tq, LANES), F32)],
        compiler_params=_cparams(2), name="sb_attn",
    )(q, k, v, tri)


def _ret_body(q_ref, k_ref, v_ref, g_ref, s0_ref, dmask_ref, dq_ref, dk_ref, cd_ref, bd_ref, gn_ref,
              y_ref, snew_ref, state_ref, *, tc, lc, nc):
    c = pl.program_id(1)

    @pl.when(c == 0)
    def _():
        state_ref[...] = s0_ref[0]

    lane = lax.broadcasted_iota(jnp.int32, (lc, LANES), 1)
    first = lane < HALF
    for p in range(RET_HEADS // 2):
        for ch in range(tc // lc):
            rows = slice(ch * lc, (ch + 1) * lc)
            q = q_ref[0, p, rows, :]
            k = k_ref[0, p, rows, :]
            v = v_ref[0, p, rows, :]
            state = state_ref[p]
            cross = _dot((q.astype(F32) * dq_ref[p]).astype(BF16), state.astype(BF16))
            zero = jnp.zeros_like(q)
            inner = []
            for a in range(2):
                qa = jnp.where(first, q, zero) if a == 0 else jnp.where(first, zero, q)
                scores = _dot_nt(qa, k) * dmask_ref[2 * p + a]
                inner.append(_dot(scores.astype(BF16), v))
            o = jnp.where(first, inner[0], inner[1]) + cross
            kd = (k.astype(F32) * dk_ref[p]).astype(BF16)
            kv = lax.dot_general(kd, v, (((0,), (0,)), ((), ())), preferred_element_type=F32)
            state_ref[p] = cd_ref[p] * state + bd_ref[...] * kv
            zf = jnp.zeros_like(o)
            mu = jnp.where(first, jnp.sum(jnp.where(first, o, zf), axis=1, keepdims=True),
                           jnp.sum(jnp.where(first, zf, o), axis=1, keepdims=True)) * (1.0 / RET_DV)
            d = o - mu
            d2 = d * d
            var = jnp.where(first, jnp.sum(jnp.where(first, d2, zf), axis=1, keepdims=True),
                            jnp.sum(jnp.where(first, zf, d2), axis=1, keepdims=True)) * (1.0 / RET_DV)
            yn = d * lax.rsqrt(var + EPS) * gn_ref[p]
            gate = g_ref[0, p, rows, :]
            y_ref[0, p, rows, :] = (yn * (gate * jax.nn.sigmoid(gate))).astype(BF16)

    @pl.when(c == nc - 1)
    def _():
        snew_ref[0] = state_ref[...]


def _ret_tables(lc):
    log_gamma = jnp.log(1.0 - 2.0 ** (-5.0 - jnp.arange(RET_HEADS, dtype=F32)))
    idx = jnp.arange(lc, dtype=F32)
    diff = idx[:, None] - idx[None, :]
    dmask = jnp.where(diff[None] >= 0, jnp.exp(jnp.maximum(diff, 0.0)[None] * log_gamma[:, None, None]), 0.0)
    dk = jnp.exp((lc - 1.0 - idx)[:, None] * log_gamma[None, :])
    dq = jnp.exp((idx + 1.0)[:, None] * log_gamma[None, :])
    cd = jnp.exp(lc * log_gamma)

    def lanes(t):
        t = jnp.repeat(t[:, :, None], HALF, axis=2).reshape(t.shape[0], RET_HEADS // 2, LANES)
        return jnp.transpose(t, (1, 0, 2))

    blk = jnp.arange(LANES) // HALF
    bd = (blk[:, None] == blk[None, :]).astype(F32)
    cd_rows = jnp.repeat(cd, HALF).reshape(RET_HEADS // 2, LANES, 1)
    return dmask, lanes(dq), lanes(dk), cd_rows * bd[None], bd


def _state_to_pairs(s):
    B = s.shape[0]
    s = s.astype(F32).reshape(B, RET_HEADS // 2, 2, RET_DK, RET_DV)
    z = jnp.zeros_like(s[:, :, 0])
    top = jnp.concatenate([s[:, :, 0], z], axis=-1)
    bot = jnp.concatenate([z, s[:, :, 1]], axis=-1)
    return jnp.concatenate([top, bot], axis=-2)


def _pairs_to_state(sp):
    a = sp[:, :, :HALF, :HALF]
    b = sp[:, :, HALF:, HALF:]
    B = sp.shape[0]
    return jnp.stack([a, b], axis=2).reshape(B, RET_HEADS, RET_DK, RET_DV)


def _retention(rq, rk, rv, rg, s0_pairs, g_norm_pairs, *, lc):
    B, P, T, _ = rq.shape
    tc = _tile(T, TC_RET)
    nc = T // tc
    dmask, dq, dk, cd, bd = _ret_tables(lc)
    pair = pl.BlockSpec((1, P, tc, LANES), lambda b, c: (b, 0, c, 0))
    st = pl.BlockSpec((1, P, LANES, LANES), lambda b, c: (b, 0, 0, 0))
    return pl.pallas_call(
        functools.partial(_ret_body, tc=tc, lc=lc, nc=nc),
        grid=(B, nc),
        in_specs=[pair, pair, pair, pair, st, _full(dmask.shape), _full(dq.shape), _full(dk.shape),
                  _full(cd.shape), _full(bd.shape), _full(g_norm_pairs.shape)],
        out_specs=[pair, st],
        out_shape=[jax.ShapeDtypeStruct((B, P, T, LANES), BF16),
                   jax.ShapeDtypeStruct((B, P, LANES, LANES), F32)],
        scratch_shapes=[pltpu.VMEM((P, LANES, LANES), F32)],
        compiler_params=_cparams(2), name="retention",
    )(rq, rk, rv, rg, s0_pairs, dmask, dq, dk, cd, bd, g_norm_pairs)


def _out_mem_body(x_ref, omla_ref, ret_ref, osb_ref, wout_ref, gq_ref, wq_ref, mk_ref, mv_ref, wo_ref, y_ref):
    mixed = jnp.concatenate([ref[0, p] for ref, n in ((omla_ref, MLA_HEADS // 2), (ret_ref, RET_HEADS // 2),
                                                      (osb_ref, SB_HEADS // 2)) for p in range(n)], axis=1)
    x1 = x_ref[0] + _dot(mixed, wout_ref[...])
    hq = _rms(x1, gq_ref[...]).astype(BF16)
    q = _dot(hq, wq_ref[...]).astype(BF16)
    heads = []
    for hh in range(MEM_HEADS):
        sl = slice(hh * MEM_HD, (hh + 1) * MEM_HD)
        s = _dot_nt(q[:, sl], mk_ref[0, :, sl]) * MEM_SCALE
        e = jnp.exp(s - jnp.max(s, axis=1, keepdims=True))
        prob = e / jnp.sum(e, axis=1, keepdims=True)
        heads.append(_dot(prob.astype(BF16), mv_ref[0, :, sl]).astype(BF16))
    o = jnp.concatenate(heads, axis=1)
    y_ref[0] = x1 + _dot(o, wo_ref[...])


def _out_mem(x, o_mla, ret_y, o_sb, mem_k, mem_v, lw):
    B, T, D = x.shape
    tm = _tile(T, TM_OUT)
    M = mem_k.shape[1]
    mw = MEM_HEADS * MEM_HD
    nat = pl.BlockSpec((1, tm, D), lambda b, t: (b, t, 0))
    pair = lambda n: pl.BlockSpec((1, n, tm, LANES), lambda b, t: (b, 0, t, 0))
    mem = pl.BlockSpec((1, M, mw), lambda b, t: (b, 0, 0))
    return pl.pallas_call(
        _out_mem_body, grid=(B, T // tm),
        in_specs=[nat, pair(MLA_HEADS // 2), pair(RET_HEADS // 2), pair(SB_HEADS // 2), _full((D, D)),
                  _full((1, D)), _full((D, mw)), mem, mem, _full((mw, D))],
        out_specs=nat, out_shape=jax.ShapeDtypeStruct((B, T, D), F32),
        compiler_params=_cparams(2), name="out_mem",
    )(x, o_mla, ret_y, o_sb, lw["w_out"], lw["g_mem_q"], lw["w_mem_q"], mem_k, mem_v, lw["w_mem_o"])


def _mem_kv_body(mem_ref, g_ref, wk_ref, wv_ref, k_ref, v_ref, kb_ref, vb_ref):
    m = _rms(mem_ref[0], g_ref[...]).astype(BF16)
    k = _dot(m, wk_ref[...])
    v = _dot(m, wv_ref[...])
    k_ref[0] = k
    v_ref[0] = v
    kb_ref[0] = k.astype(BF16)
    vb_ref[0] = v.astype(BF16)


def _mem_kv(mem, lw):
    B, M, D = mem.shape
    mw = MEM_HEADS * MEM_HD
    o = pl.BlockSpec((1, M, mw), lambda b: (b, 0, 0))
    return pl.pallas_call(
        _mem_kv_body, grid=(B,),
        in_specs=[pl.BlockSpec((1, M, D), lambda b: (b, 0, 0)), _full((1, D)), _full((D, mw)), _full((D, mw))],
        out_specs=[o, o, o, o],
        out_shape=[jax.ShapeDtypeStruct((B, M, mw), F32), jax.ShapeDtypeStruct((B, M, mw), F32),
                   jax.ShapeDtypeStruct((B, M, mw), BF16), jax.ShapeDtypeStruct((B, M, mw), BF16)],
        compiler_params=_cparams(1), name="mem_kv",
    )(mem, lw["g_mem_kv"], lw["w_mem_k"], lw["w_mem_v"])


_CARRY_ROW = 8 - (CONV_W - 1)


def _ffn_body(x_ref, g_ref, wup_ref, wconv_ref, bconv_ref, wdown_ref, conv0_ref, gfin_ref,
              y_ref, convnew_ref, carry_ref, exta_ref, extb_ref, act_ref, *, tm, nt, final):
    t = pl.program_id(1)

    @pl.when(t == 0)
    def _():
        carry_ref[_CARRY_ROW:8, :] = conv0_ref[0]

    x = x_ref[0]
    h = _rms(x, g_ref[...]).astype(BF16)

    def conv_part(c0, ext_ref):
        cols = slice(c0, c0 + TF_FFN)
        u = _dot(h, wup_ref[:, cols])
        ext_ref[_CARRY_ROW:8, :] = carry_ref[_CARRY_ROW:8, cols]
        ext_ref[8:8 + tm, :] = u
        carry_ref[_CARRY_ROW:8, cols] = u[tm - (CONV_W - 1):, :]
        w = wconv_ref[:, cols]
        c = bconv_ref[:, cols]
        c = c + ext_ref[_CARRY_ROW:_CARRY_ROW + tm, :] * w[0:1]
        c = c + ext_ref[_CARRY_ROW + 1:_CARRY_ROW + 1 + tm, :] * w[1:2]
        return c + u * w[2:3]

    for ci in range(D_FF // TF_FFN):
        c0 = ci * TF_FFN
        a = conv_part(c0, exta_ref)
        b = conv_part(D_FF + c0, extb_ref)
        act_ref[:, c0:c0 + TF_FFN] = (a * jax.nn.sigmoid(a) * b).astype(BF16)

    x3 = x + _dot(act_ref[...], wdown_ref[...])
    y_ref[0] = _rms(x3, gfin_ref[...]) if final else x3

    @pl.when(t == nt - 1)
    def _():
        convnew_ref[0] = carry_ref[_CARRY_ROW:8, :]


def _conv_ffn(x, conv0, lw, g_final, *, final):
    B, T, D = x.shape
    tm = _tile(T, TM_FFN)
    nt = T // tm
    f2 = 2 * D_FF
    nat = pl.BlockSpec((1, tm, D), lambda b, t: (b, t, 0))
    cv = pl.BlockSpec((1, CONV_W - 1, f2), lambda b, t: (b, 0, 0))
    return pl.pallas_call(
        functools.partial(_ffn_body, tm=tm, nt=nt, final=final),
        grid=(B, nt),
        in_specs=[nat, _full((1, D)), _full((D, f2)), _full((CONV_W, f2)), _full((1, f2)), _full((D_FF, D)),
                  cv, _full((1, D))],
        out_specs=[nat, cv],
        out_shape=[jax.ShapeDtypeStruct((B, T, D), F32), jax.ShapeDtypeStruct((B, CONV_W - 1, f2), F32)],
        scratch_shapes=[pltpu.VMEM((8, f2), F32), pltpu.VMEM((8 + tm, TF_FFN), F32),
                        pltpu.VMEM((8 + tm, TF_FFN), F32), pltpu.VMEM((tm, D_FF), BF16)],
        compiler_params=_cparams(2), name="conv_ffn",
    )(x, lw["g_ffn"], lw["w_ffn_up"], lw["w_ffn_conv"], lw["b_ffn_conv"], lw["w_ffn_down"], conv0, g_final)


def _pad_cols(w, width):
    return jnp.pad(w, ((0, 0), (0, width - w.shape[1])))


def _prep_layer(l, g_mix, w_in, g_q_lora, w_q_up, g_kv_lora, w_kv_up, g_ret_norm, w_out,
                g_mem_q, g_mem_kv, w_mem_q, w_mem_k, w_mem_v, w_mem_o,
                g_ffn, w_ffn_up, w_ffn_conv, b_ffn_conv, w_ffn_down):
    w = w_in[l]
    parts, c0 = [], 0
    for n in IN_SIZES:
        parts.append(w[:, c0:c0 + n])
        c0 += n
    cq, ckv, kpe, rq, rk, rv, rg, sq, sk, sv = parts
    w1 = jnp.concatenate([cq, ckv, _pad_cols(kpe, LANES), rq, rk, rv, rg, sq, sk, sv], axis=1).astype(BF16)
    kq = w_q_up.shape[1]
    wq3 = w_q_up[l].reshape(kq, MLA_HEADS, MLA_NOPE + MLA_ROPE)
    tail = LANES - MLA_NOPE - MLA_ROPE
    q_pad = jnp.pad(wq3, ((0, 0), (0, 0), (0, tail))).reshape(kq, MLA_HEADS * LANES)
    wkv3 = w_kv_up[l].reshape(MLA_KV_LORA, MLA_HEADS, MLA_NOPE + MLA_V)
    wk = jnp.pad(wkv3[:, :, :MLA_NOPE], ((0, 0), (0, 0), (0, LANES - MLA_NOPE))).reshape(MLA_KV_LORA, MLA_HEADS * LANES)
    wv = wkv3[:, :, MLA_NOPE:].reshape(MLA_KV_LORA, MLA_HEADS * MLA_V)
    r = jnp.arange(LANES)[:, None]
    c = jnp.arange(MLA_HEADS * LANES)[None, :]
    we = ((r < MLA_ROPE) & (c % LANES == r + MLA_NOPE)).astype(BF16)
    row = lambda g: g[l].reshape(1, -1).astype(F32)
    return dict(
        g_mix=row(g_mix), w1=w1, g_q=row(g_q_lora), wq=q_pad.astype(BF16),
        g_kv=row(g_kv_lora), wke=jnp.concatenate([wk.astype(BF16), we], axis=0), wv=wv.astype(BF16),
        g_ret=g_ret_norm[l].astype(F32).reshape(RET_HEADS // 2, 1, LANES),
        w_out=w_out[l].astype(BF16), g_mem_q=row(g_mem_q), g_mem_kv=row(g_mem_kv),
        w_mem_q=w_mem_q[l].astype(BF16), w_mem_k=w_mem_k[l].astype(BF16), w_mem_v=w_mem_v[l].astype(BF16),
        w_mem_o=w_mem_o[l].astype(BF16), g_ffn=row(g_ffn), w_ffn_up=w_ffn_up[l].astype(BF16),
        w_ffn_conv=w_ffn_conv[l].astype(F32), b_ffn_conv=row(b_ffn_conv), w_ffn_down=w_ffn_down[l].astype(BF16))


def _rope_tables(pos):
    t = pos.shape[0]
    posf = pos.astype(F32)[:, None]

    def cos_sin(d):
        inv = ROPE_BASE ** (-jnp.arange(0, d, 2, dtype=F32) / d)
        ang = posf * inv[None, :]
        return jnp.cos(ang), jnp.sin(ang)

    c16, s16 = cos_sin(MLA_ROPE)
    c32, s32 = cos_sin(RET_DK)
    z = lambda n: jnp.zeros((t, n), F32)
    tail = LANES - MLA_NOPE - MLA_ROPE
    return dict(
        cslot=jnp.concatenate([jnp.ones((t, MLA_NOPE), F32), c16, c16, z(tail)], axis=1),
        slo_slot=jnp.concatenate([z(MLA_NOPE), -s16, z(MLA_ROPE // 2 + tail)], axis=1),
        shi_slot=jnp.concatenate([z(MLA_NOPE + MLA_ROPE // 2), s16, z(tail)], axis=1),
        ck=jnp.concatenate([c16, c16, z(LANES - MLA_ROPE)], axis=1),
        slo_k=jnp.concatenate([-s16, z(LANES - MLA_ROPE // 2)], axis=1),
        shi_k=jnp.concatenate([z(MLA_ROPE // 2), s16, z(LANES - MLA_ROPE)], axis=1),
        cr=jnp.tile(jnp.concatenate([c32, c32], axis=1), (1, LANES // RET_DK)),
        slo_r=jnp.tile(jnp.concatenate([-s32, z(RET_DK // 2)], axis=1), (1, LANES // RET_DK)),
        shi_r=jnp.tile(jnp.concatenate([z(RET_DK // 2), s32], axis=1), (1, LANES // RET_DK)))


def _pad_rows(a, axis, n):
    pad = [(0, 0)] * a.ndim
    pad[axis] = (0, n - a.shape[axis])
    return jnp.pad(a, pad)


def _to_pairs(a):
    B, T, _ = a.shape
    return jnp.transpose(a.reshape(B, T, 2, LANES), (0, 2, 1, 3))


def _layer(x, tabs, lw, mem_k, mem_v, past, g_final, *, q0, final):
    B, T, _ = x.shape
    outs = _in_proj(x, tabs, lw, fuse_kv=past is None)
    (ckv, kpe, kpe_b, qcat, rq, rk, rv, rg, sq, sk, sv, sk_b, sv_b) = outs[:13]
    if past is None:
        kcat, vmla = outs[13:]
        sk_all, sv_all = sk_b, sv_b
        s0 = jnp.zeros((B, RET_HEADS // 2, LANES, LANES), F32)
        conv0 = jnp.zeros((B, CONV_W - 1, 2 * D_FF), F32)
        kv_len = T
        lc = _tile(T, L_RET)
        tq_mla, tk_mla = _tile(T, TQ_MLA), _tile(T, TK_MLA)
        tq_sb, tk_sb = _tile(T, TQ_SB), _tile(T, TK_SB)
    else:
        ckv_c, kpe_c, s0, sk_c, sv_c, conv0 = past
        P = ckv_c.shape[1]
        kv_len = P + T
        tk_sb = TK_SB
        tk_pad = -(-kv_len // tk_sb) * tk_sb
        ckv_all = _pad_rows(jnp.concatenate([ckv_c.astype(F32), ckv], axis=1), 1, tk_pad)
        kpe_cb = _pad_cols(kpe_c.reshape(B * P, MLA_ROPE), LANES).reshape(B, P, LANES).astype(BF16)
        kpe_all = _pad_rows(jnp.concatenate([kpe_cb, kpe_b], axis=1), 1, tk_pad)
        sk_all = _pad_rows(jnp.concatenate([_to_pairs(sk_c.reshape(B, P, -1).astype(BF16)), sk_b], axis=2), 2, tk_pad)
        sv_all = _pad_rows(jnp.concatenate([_to_pairs(sv_c.reshape(B, P, -1).astype(BF16)), sv_b], axis=2), 2, tk_pad)
        s0 = _state_to_pairs(s0)
        conv0 = conv0.astype(F32)
        lc = T
        tq_mla, tk_mla = T, tk_pad
        tq_sb = T
        kcat, vmla = _kv_up(ckv_all, kpe_all, lw)
    o_mla = _mla_attention(qcat, kcat, vmla, q0=q0, kv_len=kv_len, tq=tq_mla, tk=tk_mla)
    o_sb = _sb_attention(sq, sk_all, sv_all, q0=q0, tq=tq_sb, tk=tk_sb)
    ret_y, s_new = _retention(rq, rk, rv, rg, s0, lw["g_ret"], lc=lc)
    x = _out_mem(x, o_mla, ret_y, o_sb, mem_k, mem_v, lw)
    x, conv_new = _conv_ffn(x, conv0, lw, g_final, final=final)
    state = (ckv, kpe, _pairs_to_state(s_new), sk.reshape(B, T, SB_HEADS, SB_HD), sv.reshape(B, T, SB_HEADS, SB_HD),
             conv_new)
    return x, state


def kernel(x_prompt, x_sample, cache_mla_ckv, cache_mla_kpe, state_ret, cache_sb_k, cache_sb_v, cache_mem_k, cache_mem_v, state_ffn_conv, mem_prompt, g_mix, w_in, g_q_lora, w_q_up, g_kv_lora, w_kv_up, g_ret_norm, w_out, g_mem_q, g_mem_kv, w_mem_q, w_mem_k, w_mem_v, w_mem_o, g_ffn, w_ffn_up, w_ffn_conv, b_ffn_conv, w_ffn_down, g_final):
    depth = w_in.shape[0]
    layers = [_prep_layer(l, g_mix, w_in, g_q_lora, w_q_up, g_kv_lora, w_kv_up, g_ret_norm, w_out,
                          g_mem_q, g_mem_kv, w_mem_q, w_mem_k, w_mem_v, w_mem_o,
                          g_ffn, w_ffn_up, w_ffn_conv, b_ffn_conv, w_ffn_down) for l in range(depth)]
    gfin = g_final.reshape(1, -1).astype(F32)
    mw = MEM_HEADS * MEM_HD

    xp = x_prompt
    Bp, Tp, _ = xp.shape
    tabs_p = _rope_tables(jnp.arange(Tp))
    p_states, p_mem_k, p_mem_v = [], [], []
    for l in range(depth):
        mk, mv, mk_b, mv_b = _mem_kv(mem_prompt, layers[l])
        xp, st = _layer(xp, tabs_p, layers[l], mk_b, mv_b, None, gfin, q0=0, final=(l == depth - 1))
        p_states.append(st)
        p_mem_k.append(mk.reshape(Bp, -1, MEM_HEADS, MEM_HD))
        p_mem_v.append(mv.reshape(Bp, -1, MEM_HEADS, MEM_HD))

    xs = x_sample
    Bs, Ts, _ = xs.shape
    past_len = cache_mla_ckv.shape[2]
    tabs_s = _rope_tables(past_len + jnp.arange(Ts))
    s_states = []
    for l in range(depth):
        past = (cache_mla_ckv[l], cache_mla_kpe[l], state_ret[l], cache_sb_k[l], cache_sb_v[l], state_ffn_conv[l])
        mk_b = cache_mem_k[l].reshape(Bs, -1, mw).astype(BF16)
        mv_b = cache_mem_v[l].reshape(Bs, -1, mw).astype(BF16)
        xs, st = _layer(xs, tabs_s, layers[l], mk_b, mv_b, past, gfin, q0=past_len, final=(l == depth - 1))
        s_states.append(st)

    p_ckv, p_kpe, p_ret, p_sbk, p_sbv, p_conv = [jnp.stack(f) for f in zip(*p_states)]
    s_ckv, s_kpe, s_ret, s_sbk, s_sbv, s_conv = [jnp.stack(f) for f in zip(*s_states)]
    return (xp, xs, p_ckv, p_kpe, p_ret, p_sbk, p_sbv, jnp.stack(p_mem_k), jnp.stack(p_mem_v), p_conv,
            s_ckv, s_kpe, s_ret, s_sbk, s_sbv, s_conv)
```

```python
import functools

import jax
import jax.numpy as jnp
from jax import lax
from jax.experimental import pallas as pl
from jax.experimental.pallas import tpu as pltpu

F32 = jnp.float32
BF16 = jnp.bfloat16

CHUNK = 64
EPS = 1e-6
ROPE_BASE = 10000.0

MLA_HEADS = 8
MLA_NOPE = 64
MLA_ROPE = 32
MLA_V = 64
MLA_Q_LORA = 256
MLA_KV_LORA = 128
MLA_SCALE = (MLA_NOPE + MLA_ROPE) ** -0.5
_LOG2E = 1.4426950408889634
_MLA_EXP2_SCALE = MLA_SCALE * _LOG2E
_SB_UNDERFLOW_LOG2 = -160.0
RET_HEADS = 4
RET_DK = 64
RET_DV = 64
RET_SCALE = RET_DK ** -0.5
SB_HEADS = 4
SB_HD = 64
SB_SCALE = SB_HD ** -0.5
MEM_HEADS = 4
MEM_HD = 128
MEM_SCALE = MEM_HD ** -0.5
D_FF = 2816
CONV_W = 3

IN_SIZES = (MLA_Q_LORA, MLA_KV_LORA, MLA_ROPE, RET_HEADS * RET_DK, RET_HEADS * RET_DK,
            RET_HEADS * RET_DV, RET_HEADS * RET_DV, SB_HEADS * SB_HD, SB_HEADS * SB_HD, SB_HEADS * SB_HD)

LANES = 128
HALF = LANES // 2
VMEM_LIMIT = 56 * 1024 * 1024

TM_PROJ = 512
TM_KVUP = 512
TQ_MLA = 512
TK_MLA = 512
TQ_SB = 256
TK_SB = 256
TC_RET = 512
L_RET = 256
TM_OUT = 512
TM_FFN = 512
TF_FFN = 256


def _tile(n, target):
    t = min(n, target)
    while n % t:
        t -= 1
    return t


def _cparams(n_axes):
    return pltpu.CompilerParams(dimension_semantics=("arbitrary",) * n_axes, vmem_limit_bytes=VMEM_LIMIT)


def _rms(x, g):
    return x * lax.rsqrt(jnp.mean(x * x, axis=-1, keepdims=True) + EPS) * g


def _lane_tile(x, reps):
    return jnp.concatenate([x] * reps, axis=1) if reps > 1 else x


def _full(shape):
    n = len(shape)
    return pl.BlockSpec(shape, lambda *_: (0,) * n)


def _dot(a, b):
    return jnp.dot(a, b, preferred_element_type=F32)


def _dot_nt(a, b):
    return lax.dot_general(a, b, (((1,), (1,)), ((), ())), preferred_element_type=F32)


_C_CQ = 0
_C_CKV = 256
_C_KPE = 384
_C_RQ = 512
_C_RK = 768
_C_RV = 1024
_C_RG = 1280
_C_SQ = 1536
_C_SK = 1792
_C_SV = 2048
_W1_COLS = 2304


def _rope(x, cos, sin_lo, sin_hi, half):
    return x * cos + pltpu.roll(x, half, 1) * sin_hi + pltpu.roll(x, LANES - half, 1) * sin_lo


def _kv_slots(c, kp, wke_ref, wv_ref, kcat_ref, v_ref):
    ck = jnp.concatenate([c, kp], axis=1)
    for p in range(MLA_HEADS // 2):
        k2 = _dot(ck, wke_ref[:, 2 * LANES * p:2 * LANES * (p + 1)])
        kcat_ref[0, 2 * p] = k2[:, :LANES].astype(BF16)
        kcat_ref[0, 2 * p + 1] = k2[:, LANES:].astype(BF16)
    for p in range(MLA_HEADS // 4):
        v2 = _dot(c, wv_ref[:, 2 * LANES * p:2 * LANES * (p + 1)])
        v_ref[0, 2 * p] = v2[:, :LANES].astype(BF16)
        v_ref[0, 2 * p + 1] = v2[:, LANES:].astype(BF16)


def _in_proj_body(x_ref, gmix_ref, w1_ref, gq_ref, wq_ref, gkv_ref, wke_ref, wv_ref,
                  cslot_ref, slo_slot_ref, shi_slot_ref, ck_ref, slo_k_ref, shi_k_ref, cr_ref, slo_r_ref, shi_r_ref,
                  ckv_ref, kpe_ref, kpeb_ref, qcat_ref, rq_ref, rk_ref, rv_ref, rg_ref,
                  sq_ref, sk_ref, sv_ref, skb_ref, svb_ref, *kv_refs):
    h = _rms(x_ref[0], gmix_ref[...]).astype(BF16)

    def proj(c0, width):
        return _dot(h, w1_ref[:, c0:c0 + width])

    cq = proj(_C_CQ, 256)
    ckv_raw = proj(_C_CKV, 128)
    kpe_raw = proj(_C_KPE, 128)
    rq = proj(_C_RQ, 256)
    rk = proj(_C_RK, 256)
    rv = proj(_C_RV, 256)
    rg = proj(_C_RG, 256)
    sq = proj(_C_SQ, 256)
    sk = proj(_C_SK, 256)
    sv = proj(_C_SV, 256)

    ckv = _rms(ckv_raw, gkv_ref[...])
    ckv_ref[0] = ckv
    kpe = _rope(kpe_raw, ck_ref[...], slo_k_ref[...], shi_k_ref[...], MLA_ROPE // 2)
    kpe_ref[0] = kpe[:, :MLA_ROPE]
    kpe_b = kpe.astype(BF16)
    kpeb_ref[0] = kpe_b
    cqn = _rms(cq, gq_ref[...]).astype(BF16)
    cslot, slo_slot, shi_slot = cslot_ref[...], slo_slot_ref[...], shi_slot_ref[...]
    for p in range(MLA_HEADS // 2):
        q2 = _dot(cqn, wq_ref[:, 2 * LANES * p:2 * LANES * (p + 1)])
        for a in range(2):
            qs = q2[:, a * LANES:(a + 1) * LANES]
            qcat_ref[0, 2 * p + a] = _rope(qs, cslot, slo_slot, shi_slot, MLA_ROPE // 2).astype(BF16)
    if kv_refs:
        _kv_slots(ckv.astype(BF16), kpe_b, wke_ref, wv_ref, *kv_refs)

    cr, slo_r, shi_r = cr_ref[...], slo_r_ref[...], shi_r_ref[...]
    sk_ref[0] = sk
    sv_ref[0] = sv
    for p in range(2):
        sl = slice(p * LANES, (p + 1) * LANES)
        rq_ref[0, p] = _rope(rq[:, sl], cr, slo_r, shi_r, RET_DK // 2).astype(BF16)
        rk_ref[0, p] = (_rope(rk[:, sl], cr, slo_r, shi_r, RET_DK // 2) * RET_SCALE).astype(BF16)
        rv_ref[0, p] = rv[:, sl].astype(BF16)
        rg_ref[0, p] = rg[:, sl]
        sq_ref[0, p] = (sq[:, sl] * SB_SCALE).astype(BF16)
        skb_ref[0, p] = sk[:, sl].astype(BF16)
        svb_ref[0, p] = sv[:, sl].astype(BF16)


def _in_proj(x, tabs, lw, *, fuse_kv):
    B, T, D = x.shape
    tm = _tile(T, TM_PROJ)
    grid = (B, T // tm)
    row = pl.BlockSpec((tm, LANES), lambda b, t: (t, 0))
    pair = lambda n: pl.BlockSpec((1, n, tm, LANES), lambda b, t: (b, 0, t, 0))
    nat = lambda w: pl.BlockSpec((1, tm, w), lambda b, t: (b, t, 0))
    in_specs = [nat(D), _full((1, D)), _full((D, _W1_COLS)), _full((1, MLA_Q_LORA)),
                _full((MLA_Q_LORA, MLA_HEADS * LANES)), _full((1, MLA_KV_LORA)),
                _full((MLA_KV_LORA + LANES, MLA_HEADS * LANES)), _full((MLA_KV_LORA, MLA_HEADS * MLA_V))] + [row] * 9
    out_shape = [
        jax.ShapeDtypeStruct((B, T, MLA_KV_LORA), F32),
        jax.ShapeDtypeStruct((B, T, MLA_ROPE), F32),
        jax.ShapeDtypeStruct((B, T, LANES), BF16),
        jax.ShapeDtypeStruct((B, MLA_HEADS, T, LANES), BF16),
        jax.ShapeDtypeStruct((B, 2, T, LANES), BF16),
        jax.ShapeDtypeStruct((B, 2, T, LANES), BF16),
        jax.ShapeDtypeStruct((B, 2, T, LANES), BF16),
        jax.ShapeDtypeStruct((B, 2, T, LANES), F32),
        jax.ShapeDtypeStruct((B, 2, T, LANES), BF16),
        jax.ShapeDtypeStruct((B, T, 256), F32),
        jax.ShapeDtypeStruct((B, T, 256), F32),
        jax.ShapeDtypeStruct((B, 2, T, LANES), BF16),
        jax.ShapeDtypeStruct((B, 2, T, LANES), BF16),
    ]
    out_specs = [nat(MLA_KV_LORA), nat(MLA_ROPE), nat(LANES), pair(MLA_HEADS), pair(2), pair(2), pair(2),
                 pair(2), pair(2), nat(256), nat(256), pair(2), pair(2)]
    if fuse_kv:
        out_shape += [jax.ShapeDtypeStruct((B, MLA_HEADS, T, LANES), BF16),
                      jax.ShapeDtypeStruct((B, MLA_HEADS // 2, T, LANES), BF16)]
        out_specs += [pair(MLA_HEADS), pair(MLA_HEADS // 2)]
    return pl.pallas_call(
        _in_proj_body, grid=grid, in_specs=in_specs, out_specs=out_specs, out_shape=out_shape,
        compiler_params=_cparams(2), name="in_proj",
    )(x, lw["g_mix"], lw["w1"], lw["g_q"], lw["wq"], lw["g_kv"], lw["wke"], lw["wv"],
      tabs["cslot"], tabs["slo_slot"], tabs["shi_slot"], tabs["ck"], tabs["slo_k"], tabs["shi_k"],
      tabs["cr"], tabs["slo_r"], tabs["shi_r"])


def _kv_up_body(ckv_ref, kpe_ref, wke_ref, wv_ref, kcat_ref, v_ref):
    _kv_slots(ckv_ref[0].astype(BF16), kpe_ref[0], wke_ref, wv_ref, kcat_ref, v_ref)


def _kv_up(ckv, kpe_pad, lw):
    B, T, _ = ckv.shape
    tm = _tile(T, TM_KVUP)
    nat = lambda w: pl.BlockSpec((1, tm, w), lambda b, t: (b, t, 0))
    pair = lambda n: pl.BlockSpec((1, n, tm, LANES), lambda b, t: (b, 0, t, 0))
    return pl.pallas_call(
        _kv_up_body, grid=(B, T // tm),
        in_specs=[nat(MLA_KV_LORA), nat(LANES), _full((MLA_KV_LORA + LANES, MLA_HEADS * LANES)),
                  _full((MLA_KV_LORA, MLA_HEADS * MLA_V))],
        out_specs=[pair(MLA_HEADS), pair(MLA_HEADS // 2)],
        out_shape=[jax.ShapeDtypeStruct((B, MLA_HEADS, T, LANES), BF16),
                   jax.ShapeDtypeStruct((B, MLA_HEADS // 2, T, LANES), BF16)],
        compiler_params=_cparams(2), name="kv_up",
    )(ckv, kpe_pad, lw["wke"], lw["wv"])


def _mla_last_block(i, *, q0, tq, tk, nk):
    last_q = q0 + i * tq + (tq - 1)
    last_key = (last_q // CHUNK) * CHUNK + (CHUNK - 1)
    return jnp.minimum(last_key // tk, nk - 1)


def _mla_body(q_ref, k_ref, v_ref, o_ref, m_ref, l_ref, acc_ref, *, q0, tq, tk, nk, kv_len):
    i = pl.program_id(1)
    j = pl.program_id(2)

    @pl.when(j == 0)
    def _():
        m_ref[...] = jnp.full(m_ref.shape, -jnp.inf, F32)
        l_ref[...] = jnp.zeros(l_ref.shape, F32)
        acc_ref[...] = jnp.zeros(acc_ref.shape, F32)

    def step(masked):
        lane = lax.broadcasted_iota(jnp.int32, (tq, LANES), 1)
        first = lane < HALF
        if masked:
            qpos = q0 + i * tq + lax.broadcasted_iota(jnp.int32, (tq, 1), 0)
            limit = jnp.minimum((jnp.right_shift(qpos, 6) + 1) * CHUNK, kv_len)
            kpos = j * tk + lax.broadcasted_iota(jnp.int32, (tq, tk), 1)
            bias = jnp.where(kpos < limit, 0.0, -jnp.inf).astype(F32)

        def scores(hh):
            s = _dot_nt(q_ref[0, hh], k_ref[0, hh])
            return s + bias if masked else s

        s_next = scores(0)
        alphas, pvs = [], []
        for hh in range(MLA_HEADS):
            p, a = divmod(hh, 2)
            s = s_next
            if hh + 1 < MLA_HEADS:
                s_next = scores(hh + 1)
            m_prev = m_ref[hh]
            m_new = jnp.maximum(m_prev, jnp.max(s, axis=1, keepdims=True))
            e = jnp.exp2((s - _lane_tile(m_new, tk // LANES)) * _MLA_EXP2_SCALE)
            alpha = jnp.exp2((m_prev - m_new) * _MLA_EXP2_SCALE)
            l_ref[hh] = alpha * l_ref[hh] + jnp.sum(e, axis=1, keepdims=True)
            m_ref[hh] = m_new
            alphas.append(alpha)
            pvs.append(_dot(e.astype(BF16), v_ref[0, p]))
            if a == 1:
                acc_ref[p] = acc_ref[p] * jnp.where(first, alphas[0], alphas[1]) + jnp.where(first, pvs[0], pvs[1])
                alphas, pvs = [], []

    all_visible = jnp.minimum(((q0 + i * tq) // CHUNK + 1) * CHUNK, kv_len)
    fully_visible = (j + 1) * tk <= all_visible
    needed = j <= _mla_last_block(i, q0=q0, tq=tq, tk=tk, nk=nk)

    @pl.when(jnp.logical_and(needed, fully_visible))
    def _():
        step(False)

    @pl.when(jnp.logical_and(needed, jnp.logical_not(fully_visible)))
    def _():
        step(True)

    @pl.when(j == nk - 1)
    def _():
        lane = lax.broadcasted_iota(jnp.int32, (tq, LANES), 1)
        for p in range(MLA_HEADS // 2):
            inv = jnp.where(lane < HALF, 1.0 / l_ref[2 * p], 1.0 / l_ref[2 * p + 1])
            o_ref[0, p] = (acc_ref[p] * inv).astype(BF16)


def _mla_attention(qcat, kcat, v, *, q0, kv_len, tq, tk):
    B, H, Tq, _ = qcat.shape
    Tk = kcat.shape[2]
    nq, nk = Tq // tq, Tk // tk
    kw = dict(q0=q0, tq=tq, tk=tk, nk=nk)

    def kv_map(b, i, j):
        return (b, 0, jnp.minimum(j, _mla_last_block(i, **kw)), 0)

    return pl.pallas_call(
        functools.partial(_mla_body, kv_len=kv_len, **kw),
        grid=(B, nq, nk),
        in_specs=[pl.BlockSpec((1, H, tq, LANES), lambda b, i, j: (b, 0, i, 0)),
                  pl.BlockSpec((1, H, tk, LANES), kv_map),
                  pl.BlockSpec((1, H // 2, tk, LANES), kv_map)],
        out_specs=pl.BlockSpec((1, H // 2, tq, LANES), lambda b, i, j: (b, 0, i, 0)),
        out_shape=jax.ShapeDtypeStruct((B, H // 2, Tq, LANES), BF16),
        scratch_shapes=[pltpu.VMEM((H, tq, LANES), F32), pltpu.VMEM((H, tq, LANES), F32),
                        pltpu.VMEM((H // 2, tq, LANES), F32)],
        compiler_params=_cparams(3), name="mla_attn",
    )(qcat, kcat, v)


def _sb_last_block(i, *, q0, tq, tk, nk):
    last_key = q0 + i * tq + (tq - 1) - 1
    return jnp.clip(last_key // tk, 0, nk - 1)


def _sb_body(q_ref, k_ref, v_ref, u_ref, o_ref, run_ref, acc_ref, *, q0, tq, tk, nk):
    i = pl.program_id(1)
    q_first = q0 + i * tq
    last = _sb_last_block(i, q0=q0, tq=tq, tk=tk, nk=nk)
    first_unmasked = jnp.minimum(q_first // tk - 1, last)
    run_ref[...] = jnp.zeros(run_ref.shape, F32)
    acc_ref[...] = jnp.zeros(acc_ref.shape, F32)

    def step(kb, masked):
        first_k = lax.broadcasted_iota(jnp.int32, (tk, LANES), 1) < HALF
        tri = u_ref[...]
        reps = tk // LANES
        rows = pl.ds(pl.multiple_of(kb * tk, tk), tk)
        if masked:
            qpos = q_first + lax.broadcasted_iota(jnp.int32, (tq, 1), 0)
            kpos = kb * tk + lax.broadcasted_iota(jnp.int32, (tq, tk), 1)
            before1 = kpos < qpos
            before = jnp.concatenate([before1, before1], axis=1)

        def scores(p):
            kp = k_ref[0, p, rows, :]
            zero = jnp.zeros_like(kp)
            k2 = jnp.concatenate([jnp.where(first_k, kp, zero), jnp.where(first_k, zero, kp)], axis=0)
            return _dot_nt(q_ref[0, p], k2) * _LOG2E

        npair = SB_HEADS // 2
        zs = [scores(p) for p in range(npair)]
        keeps, betas, betweens = [], [], []
        for p in range(npair):
            z = zs[p]
            nz = -z
            sp = jnp.log(1.0 + jnp.exp2(jnp.minimum(z, nz))) * _LOG2E
            log_keep = jnp.minimum(nz, 0.0) - sp
            betas.append(log_keep + z)
            if masked:
                log_keep = jnp.where(before, log_keep, 0.0)
            keeps.append(log_keep)
            hi = log_keep.astype(BF16)
            lo = (log_keep - hi.astype(F32)).astype(BF16)
            betweens.append(jnp.concatenate(
                [_dot(hi[:, :tk], tri) + _dot(lo[:, :tk], tri), _dot(hi[:, tk:], tri) + _dot(lo[:, tk:], tri)], axis=1))
        for p in range(npair):
            vp = v_ref[0, p, rows, :]
            zero = jnp.zeros_like(vp)
            v2 = jnp.concatenate([jnp.where(first_k, vp, zero), jnp.where(first_k, zero, vp)], axis=0)
            run_a = run_ref[2 * p]
            run_b = run_ref[2 * p + 1]
            run = jnp.concatenate([_lane_tile(run_a, reps), _lane_tile(run_b, reps)], axis=1)
            att = jnp.exp2(betas[p] + betweens[p] + run)
            if masked:
                att = jnp.where(before, att, 0.0)
            run_ref[2 * p] = run_a + jnp.sum(keeps[p][:, :tk], axis=1, keepdims=True)
            run_ref[2 * p + 1] = run_b + jnp.sum(keeps[p][:, tk:], axis=1, keepdims=True)
            acc_ref[p] = acc_ref[p] + _dot(att.astype(BF16), v2)

    def masked_step(t, carry):
        step(last - t, True)
        return carry

    lax.fori_loop(0, last - first_unmasked, masked_step, 0)

    def more(carry):
        kb, live = carry
        return jnp.logical_and(kb >= 0, live > 0)

    def unmasked_step(carry):
        kb, _ = carry
        step(kb, False)
        live = (jnp.max(run_ref[...]) >= _SB_UNDERFLOW_LOG2).astype(jnp.int32)
        return kb - 1, live

    lax.while_loop(more, unmasked_step, (first_unmasked, jnp.int32(1)))
    o_ref[0] = acc_ref[...].astype(BF16)


def _sb_attention(q, k, v, *, q0, tq, tk):
    B, P, Tq, _ = q.shape
    Tk = k.shape[2]
    nq, nk = Tq // tq, Tk // tk
    kw = dict(q0=q0, tq=tq, tk=tk, nk=nk)
    r = lax.broadcasted_iota(jnp.int32, (tk, tk), 0)
    c = lax.broadcasted_iota(jnp.int32, (tk, tk), 1)
    tri = (r > c).astype(BF16)

    kv_spec = pl.BlockSpec((1, P, Tk, LANES), lambda b, i: (b, 0, 0, 0))
    return pl.pallas_call(
        functools.partial(_sb_body, **kw),
        grid=(B, nq),
        in_specs=[pl.BlockSpec((1, P, tq, LANES), lambda b, i: (b, 0, i, 0)), kv_spec, kv_spec, _full((tk, tk))],
        out_specs=pl.BlockSpec((1, P, tq, LANES), lambda b, i: (b, 0, i, 0)),
        out_shape=jax.ShapeDtypeStruct((B, P, Tq, LANES), BF16),
        scratch_shapes=[pltpu.VMEM((2 * P, tq, LANES), F32), pltpu.VMEM((P, tq, LANES), F32)],
        compiler_params=_cparams(2), name="sb_attn",
    )(q, k, v, tri)


def _ret_body(q_ref, k_ref, v_ref, g_ref, s0_ref, dmask_ref, dq_ref, dk_ref, cd_ref, bd_ref, gn_ref,
              y_ref, snew_ref, state_ref, *, tc, lc, nc):
    c = pl.program_id(1)

    @pl.when(c == 0)
    def _():
        state_ref[...] = s0_ref[0]

    lane = lax.broadcasted_iota(jnp.int32, (lc, LANES), 1)
    first = lane < HALF
    for p in range(RET_HEADS // 2):
        for ch in range(tc // lc):
            rows = slice(ch * lc, (ch + 1) * lc)
            q = q_ref[0, p, rows, :]
            k = k_ref[0, p, rows, :]
            v = v_ref[0, p, rows, :]
            state = state_ref[p]
            cross = _dot((q.astype(F32) * dq_ref[p]).astype(BF16), state.astype(BF16))
            zero = jnp.zeros_like(q)
            inner = []
            for a in range(2):
                qa = jnp.where(first, q, zero) if a == 0 else jnp.where(first, zero, q)
                scores = _dot_nt(qa, k) * dmask_ref[2 * p + a]
                inner.append(_dot(scores.astype(BF16), v))
            o = jnp.where(first, inner[0], inner[1]) + cross
            kd = (k.astype(F32) * dk_ref[p]).astype(BF16)
            kv = lax.dot_general(kd, v, (((0,), (0,)), ((), ())), preferred_element_type=F32)
            state_ref[p] = cd_ref[p] * state + bd_ref[...] * kv
            zf = jnp.zeros_like(o)
            mu = jnp.where(first, jnp.sum(jnp.where(first, o, zf), axis=1, keepdims=True),
                           jnp.sum(jnp.where(first, zf, o), axis=1, keepdims=True)) * (1.0 / RET_DV)
            d = o - mu
            d2 = d * d
            var = jnp.where(first, jnp.sum(jnp.where(first, d2, zf), axis=1, keepdims=True),
                            jnp.sum(jnp.where(first, zf, d2), axis=1, keepdims=True)) * (1.0 / RET_DV)
            yn = d * lax.rsqrt(var + EPS) * gn_ref[p]
            gate = g_ref[0, p, rows, :]
            y_ref[0, p, rows, :] = (yn * (gate * jax.nn.sigmoid(gate))).astype(BF16)

    @pl.when(c == nc - 1)
    def _():
        snew_ref[0] = state_ref[...]


def _ret_tables(lc):
    log_gamma = jnp.log(1.0 - 2.0 ** (-5.0 - jnp.arange(RET_HEADS, dtype=F32)))
    idx = jnp.arange(lc, dtype=F32)
    diff = idx[:, None] - idx[None, :]
    dmask = jnp.where(diff[None] >= 0, jnp.exp(jnp.maximum(diff, 0.0)[None] * log_gamma[:, None, None]), 0.0)
    dk = jnp.exp((lc - 1.0 - idx)[:, None] * log_gamma[None, :])
    dq = jnp.exp((idx + 1.0)[:, None] * log_gamma[None, :])
    cd = jnp.exp(lc * log_gamma)

    def lanes(t):
        t = jnp.repeat(t[:, :, None], HALF, axis=2).reshape(t.shape[0], RET_HEADS // 2, LANES)
        return jnp.transpose(t, (1, 0, 2))

    blk = jnp.arange(LANES) // HALF
    bd = (blk[:, None] == blk[None, :]).astype(F32)
    cd_rows = jnp.repeat(cd, HALF).reshape(RET_HEADS // 2, LANES, 1)
    return dmask, lanes(dq), lanes(dk), cd_rows * bd[None], bd


def _state_to_pairs(s):
    B = s.shape[0]
    s = s.astype(F32).reshape(B, RET_HEADS // 2, 2, RET_DK, RET_DV)
    z = jnp.zeros_like(s[:, :, 0])
    top = jnp.concatenate([s[:, :, 0], z], axis=-1)
    bot = jnp.concatenate([z, s[:, :, 1]], axis=-1)
    return jnp.concatenate([top, bot], axis=-2)


def _pairs_to_state(sp):
    a = sp[:, :, :HALF, :HALF]
    b = sp[:, :, HALF:, HALF:]
    B = sp.shape[0]
    return jnp.stack([a, b], axis=2).reshape(B, RET_HEADS, RET_DK, RET_DV)


def _retention(rq, rk, rv, rg, s0_pairs, g_norm_pairs, *, lc):
    B, P, T, _ = rq.shape
    tc = _tile(T, TC_RET)
    nc = T // tc
    dmask, dq, dk, cd, bd = _ret_tables(lc)
    pair = pl.BlockSpec((1, P, tc, LANES), lambda b, c: (b, 0, c, 0))
    st = pl.BlockSpec((1, P, LANES, LANES), lambda b, c: (b, 0, 0, 0))
    return pl.pallas_call(
        functools.partial(_ret_body, tc=tc, lc=lc, nc=nc),
        grid=(B, nc),
        in_specs=[pair, pair, pair, pair, st, _full(dmask.shape), _full(dq.shape), _full(dk.shape),
                  _full(cd.shape), _full(bd.shape), _full(g_norm_pairs.shape)],
        out_specs=[pair, st],
        out_shape=[jax.ShapeDtypeStruct((B, P, T, LANES), BF16),
                   jax.ShapeDtypeStruct((B, P, LANES, LANES), F32)],
        scratch_shapes=[pltpu.VMEM((P, LANES, LANES), F32)],
        compiler_params=_cparams(2), name="retention",
    )(rq, rk, rv, rg, s0_pairs, dmask, dq, dk, cd, bd, g_norm_pairs)


def _out_mem_body(x_ref, omla_ref, ret_ref, osb_ref, wout_ref, gq_ref, wq_ref, mk_ref, mv_ref, wo_ref, y_ref):
    mixed = jnp.concatenate([ref[0, p] for ref, n in ((omla_ref, MLA_HEADS // 2), (ret_ref, RET_HEADS // 2),
                                                      (osb_ref, SB_HEADS // 2)) for p in range(n)], axis=1)
    x1 = x_ref[0] + _dot(mixed, wout_ref[...])
    hq = _rms(x1, gq_ref[...]).astype(BF16)
    q = _dot(hq, wq_ref[...]).astype(BF16)
    cols = [slice(hh * MEM_HD, (hh + 1) * MEM_HD) for hh in range(MEM_HEADS)]
    scores = [_dot_nt(q[:, sl], mk_ref[0, :, sl]) * MEM_SCALE for sl in cols]
    probs = []
    for s in scores:
        e = jnp.exp(s - jnp.max(s, axis=1, keepdims=True))
        probs.append((e / jnp.sum(e, axis=1, keepdims=True)).astype(BF16))
    heads = [_dot(pr, mv_ref[0, :, sl]).astype(BF16) for pr, sl in zip(probs, cols)]
    o = jnp.concatenate(heads, axis=1)
    y_ref[0] = x1 + _dot(o, wo_ref[...])


def _out_mem(x, o_mla, ret_y, o_sb, mem_k, mem_v, lw):
    B, T, D = x.shape
    tm = _tile(T, TM_OUT)
    M = mem_k.shape[1]
    mw = MEM_HEADS * MEM_HD
    nat = pl.BlockSpec((1, tm, D), lambda b, t: (b, t, 0))
    pair = lambda n: pl.BlockSpec((1, n, tm, LANES), lambda b, t: (b, 0, t, 0))
    mem = pl.BlockSpec((1, M, mw), lambda b, t: (b, 0, 0))
    return pl.pallas_call(
        _out_mem_body, grid=(B, T // tm),
        in_specs=[nat, pair(MLA_HEADS // 2), pair(RET_HEADS // 2), pair(SB_HEADS // 2), _full((D, D)),
                  _full((1, D)), _full((D, mw)), mem, mem, _full((mw, D))],
        out_specs=nat, out_shape=jax.ShapeDtypeStruct((B, T, D), F32),
        compiler_params=_cparams(2), name="out_mem",
    )(x, o_mla, ret_y, o_sb, lw["w_out"], lw["g_mem_q"], lw["w_mem_q"], mem_k, mem_v, lw["w_mem_o"])


def _mem_kv_body(mem_ref, g_ref, wk_ref, wv_ref, k_ref, v_ref, kb_ref, vb_ref):
    m = _rms(mem_ref[0], g_ref[...]).astype(BF16)
    k = _dot(m, wk_ref[...])
    v = _dot(m, wv_ref[...])
    k_ref[0] = k
    v_ref[0] = v
    kb_ref[0] = k.astype(BF16)
    vb_ref[0] = v.astype(BF16)


def _mem_kv(mem, lw):
    B, M, D = mem.shape
    mw = MEM_HEADS * MEM_HD
    o = pl.BlockSpec((1, M, mw), lambda b: (b, 0, 0))
    return pl.pallas_call(
        _mem_kv_body, grid=(B,),
        in_specs=[pl.BlockSpec((1, M, D), lambda b: (b, 0, 0)), _full((1, D)), _full((D, mw)), _full((D, mw))],
        out_specs=[o, o, o, o],
        out_shape=[jax.ShapeDtypeStruct((B, M, mw), F32), jax.ShapeDtypeStruct((B, M, mw), F32),
                   jax.ShapeDtypeStruct((B, M, mw), BF16), jax.ShapeDtypeStruct((B, M, mw), BF16)],
        compiler_params=_cparams(1), name="mem_kv",
    )(mem, lw["g_mem_kv"], lw["w_mem_k"], lw["w_mem_v"])


_CARRY_ROW = 8 - (CONV_W - 1)


def _ffn_body(x_ref, g_ref, wup_ref, wconv_ref, bconv_ref, wdown_ref, conv0_ref, gfin_ref,
              y_ref, convnew_ref, carry_ref, exta_ref, extb_ref, act_ref, *, tm, nt, final):
    t = pl.program_id(1)

    @pl.when(t == 0)
    def _():
        carry_ref[_CARRY_ROW:8, :] = conv0_ref[0]

    x = x_ref[0]
    h = _rms(x, g_ref[...]).astype(BF16)

    def conv_part(c0, ext_ref):
        cols = slice(c0, c0 + TF_FFN)
        u = _dot(h, wup_ref[:, cols])
        ext_ref[_CARRY_ROW:8, :] = carry_ref[_CARRY_ROW:8, cols]
        ext_ref[8:8 + tm, :] = u
        carry_ref[_CARRY_ROW:8, cols] = u[tm - (CONV_W - 1):, :]
        w = wconv_ref[:, cols]
        c = bconv_ref[:, cols]
        c = c + ext_ref[_CARRY_ROW:_CARRY_ROW + tm, :] * w[0:1]
        c = c + ext_ref[_CARRY_ROW + 1:_CARRY_ROW + 1 + tm, :] * w[1:2]
        return c + u * w[2:3]

    for ci in range(D_FF // TF_FFN):
        c0 = ci * TF_FFN
        a = conv_part(c0, exta_ref)
        b = conv_part(D_FF + c0, extb_ref)
        act_ref[:, c0:c0 + TF_FFN] = (a * jax.nn.sigmoid(a) * b).astype(BF16)

    x3 = x + _dot(act_ref[...], wdown_ref[...])
    y_ref[0] = _rms(x3, gfin_ref[...]) if final else x3

    @pl.when(t == nt - 1)
    def _():
        convnew_ref[0] = carry_ref[_CARRY_ROW:8, :]


def _conv_ffn(x, conv0, lw, g_final, *, final):
    B, T, D = x.shape
    tm = _tile(T, TM_FFN)
    nt = T // tm
    f2 = 2 * D_FF
    nat = pl.BlockSpec((1, tm, D), lambda b, t: (b, t, 0))
    cv = pl.BlockSpec((1, CONV_W - 1, f2), lambda b, t: (b, 0, 0))
    return pl.pallas_call(
        functools.partial(_ffn_body, tm=tm, nt=nt, final=final),
        grid=(B, nt),
        in_specs=[nat, _full((1, D)), _full((D, f2)), _full((CONV_W, f2)), _full((1, f2)), _full((D_FF, D)),
                  cv, _full((1, D))],
        out_specs=[nat, cv],
        out_shape=[jax.ShapeDtypeStruct((B, T, D), F32), jax.ShapeDtypeStruct((B, CONV_W - 1, f2), F32)],
        scratch_shapes=[pltpu.VMEM((8, f2), F32), pltpu.VMEM((8 + tm, TF_FFN), F32),
                        pltpu.VMEM((8 + tm, TF_FFN), F32), pltpu.VMEM((tm, D_FF), BF16)],
        compiler_params=_cparams(2), name="conv_ffn",
    )(x, lw["g_ffn"], lw["w_ffn_up"], lw["w_ffn_conv"], lw["b_ffn_conv"], lw["w_ffn_down"], conv0, g_final)


def _pad_cols(w, width):
    return jnp.pad(w, ((0, 0), (0, width - w.shape[1])))


def _prep_layer(l, g_mix, w_in, g_q_lora, w_q_up, g_kv_lora, w_kv_up, g_ret_norm, w_out,
                g_mem_q, g_mem_kv, w_mem_q, w_mem_k, w_mem_v, w_mem_o,
                g_ffn, w_ffn_up, w_ffn_conv, b_ffn_conv, w_ffn_down):
    w = w_in[l]
    parts, c0 = [], 0
    for n in IN_SIZES:
        parts.append(w[:, c0:c0 + n])
        c0 += n
    cq, ckv, kpe, rq, rk, rv, rg, sq, sk, sv = parts
    w1 = jnp.concatenate([cq, ckv, _pad_cols(kpe, LANES), rq, rk, rv, rg, sq, sk, sv], axis=1).astype(BF16)
    kq = w_q_up.shape[1]
    wq3 = w_q_up[l].reshape(kq, MLA_HEADS, MLA_NOPE + MLA_ROPE)
    tail = LANES - MLA_NOPE - MLA_ROPE
    q_pad = jnp.pad(wq3, ((0, 0), (0, 0), (0, tail))).reshape(kq, MLA_HEADS * LANES)
    wkv3 = w_kv_up[l].reshape(MLA_KV_LORA, MLA_HEADS, MLA_NOPE + MLA_V)
    wk = jnp.pad(wkv3[:, :, :MLA_NOPE], ((0, 0), (0, 0), (0, LANES - MLA_NOPE))).reshape(MLA_KV_LORA, MLA_HEADS * LANES)
    wv = wkv3[:, :, MLA_NOPE:].reshape(MLA_KV_LORA, MLA_HEADS * MLA_V)
    r = jnp.arange(LANES)[:, None]
    c = jnp.arange(MLA_HEADS * LANES)[None, :]
    we = ((r < MLA_ROPE) & (c % LANES == r + MLA_NOPE)).astype(BF16)
    row = lambda g: g[l].reshape(1, -1).astype(F32)
    return dict(
        g_mix=row(g_mix), w1=w1, g_q=row(g_q_lora), wq=q_pad.astype(BF16),
        g_kv=row(g_kv_lora), wke=jnp.concatenate([wk.astype(BF16), we], axis=0), wv=wv.astype(BF16),
        g_ret=g_ret_norm[l].astype(F32).reshape(RET_HEADS // 2, 1, LANES),
        w_out=w_out[l].astype(BF16), g_mem_q=row(g_mem_q), g_mem_kv=row(g_mem_kv),
        w_mem_q=w_mem_q[l].astype(BF16), w_mem_k=w_mem_k[l].astype(BF16), w_mem_v=w_mem_v[l].astype(BF16),
        w_mem_o=w_mem_o[l].astype(BF16), g_ffn=row(g_ffn), w_ffn_up=w_ffn_up[l].astype(BF16),
        w_ffn_conv=w_ffn_conv[l].astype(F32), b_ffn_conv=row(b_ffn_conv), w_ffn_down=w_ffn_down[l].astype(BF16))


def _rope_tables(pos):
    t = pos.shape[0]
    posf = pos.astype(F32)[:, None]

    def cos_sin(d):
        inv = ROPE_BASE ** (-jnp.arange(0, d, 2, dtype=F32) / d)
        ang = posf * inv[None, :]
        return jnp.cos(ang), jnp.sin(ang)

    c16, s16 = cos_sin(MLA_ROPE)
    c32, s32 = cos_sin(RET_DK)
    z = lambda n: jnp.zeros((t, n), F32)
    tail = LANES - MLA_NOPE - MLA_ROPE
    return dict(
        cslot=jnp.concatenate([jnp.ones((t, MLA_NOPE), F32), c16, c16, z(tail)], axis=1),
        slo_slot=jnp.concatenate([z(MLA_NOPE), -s16, z(MLA_ROPE // 2 + tail)], axis=1),
        shi_slot=jnp.concatenate([z(MLA_NOPE + MLA_ROPE // 2), s16, z(tail)], axis=1),
        ck=jnp.concatenate([c16, c16, z(LANES - MLA_ROPE)], axis=1),
        slo_k=jnp.concatenate([-s16, z(LANES - MLA_ROPE // 2)], axis=1),
        shi_k=jnp.concatenate([z(MLA_ROPE // 2), s16, z(LANES - MLA_ROPE)], axis=1),
        cr=jnp.tile(jnp.concatenate([c32, c32], axis=1), (1, LANES // RET_DK)),
        slo_r=jnp.tile(jnp.concatenate([-s32, z(RET_DK // 2)], axis=1), (1, LANES // RET_DK)),
        shi_r=jnp.tile(jnp.concatenate([z(RET_DK // 2), s32], axis=1), (1, LANES // RET_DK)))


def _pad_rows(a, axis, n):
    pad = [(0, 0)] * a.ndim
    pad[axis] = (0, n - a.shape[axis])
    return jnp.pad(a, pad)


def _to_pairs(a):
    B, T, _ = a.shape
    return jnp.transpose(a.reshape(B, T, 2, LANES), (0, 2, 1, 3))


def _layer(x, tabs, lw, mem_k, mem_v, past, g_final, *, q0, final):
    B, T, _ = x.shape
    outs = _in_proj(x, tabs, lw, fuse_kv=past is None)
    (ckv, kpe, kpe_b, qcat, rq, rk, rv, rg, sq, sk, sv, sk_b, sv_b) = outs[:13]
    if past is None:
        kcat, vmla = outs[13:]
        sk_all, sv_all = sk_b, sv_b
        s0 = jnp.zeros((B, RET_HEADS // 2, LANES, LANES), F32)
        conv0 = jnp.zeros((B, CONV_W - 1, 2 * D_FF), F32)
        kv_len = T
        lc = _tile(T, L_RET)
        tq_mla, tk_mla = _tile(T, TQ_MLA), _tile(T, TK_MLA)
        tq_sb, tk_sb = _tile(T, TQ_SB), _tile(T, TK_SB)
    else:
        ckv_c, kpe_c, s0, sk_c, sv_c, conv0 = past
        P = ckv_c.shape[1]
        kv_len = P + T
        tk_sb = TK_SB
        tk_pad = -(-kv_len // tk_sb) * tk_sb
        ckv_all = _pad_rows(jnp.concatenate([ckv_c.astype(F32), ckv], axis=1), 1, tk_pad)
        kpe_cb = _pad_cols(kpe_c.reshape(B * P, MLA_ROPE), LANES).reshape(B, P, LANES).astype(BF16)
        kpe_all = _pad_rows(jnp.concatenate([kpe_cb, kpe_b], axis=1), 1, tk_pad)
        sk_all = _pad_rows(jnp.concatenate([_to_pairs(sk_c.reshape(B, P, -1).astype(BF16)), sk_b], axis=2), 2, tk_pad)
        sv_all = _pad_rows(jnp.concatenate([_to_pairs(sv_c.reshape(B, P, -1).astype(BF16)), sv_b], axis=2), 2, tk_pad)
        s0 = _state_to_pairs(s0)
        conv0 = conv0.astype(F32)
        lc = T
        tq_mla, tk_mla = T, tk_pad
        tq_sb = T
        kcat, vmla = _kv_up(ckv_all, kpe_all, lw)
    o_mla = _mla_attention(qcat, kcat, vmla, q0=q0, kv_len=kv_len, tq=tq_mla, tk=tk_mla)
    o_sb = _sb_attention(sq, sk_all, sv_all, q0=q0, tq=tq_sb, tk=tk_sb)
    ret_y, s_new = _retention(rq, rk, rv, rg, s0, lw["g_ret"], lc=lc)
    x = _out_mem(x, o_mla, ret_y, o_sb, mem_k, mem_v, lw)
    x, conv_new = _conv_ffn(x, conv0, lw, g_final, final=final)
    state = (ckv, kpe, _pairs_to_state(s_new), sk.reshape(B, T, SB_HEADS, SB_HD), sv.reshape(B, T, SB_HEADS, SB_HD),
             conv_new)
    return x, state


def kernel(x_prompt, x_sample, cache_mla_ckv, cache_mla_kpe, state_ret, cache_sb_k, cache_sb_v, cache_mem_k, cache_mem_v, state_ffn_conv, mem_prompt, g_mix, w_in, g_q_lora, w_q_up, g_kv_lora, w_kv_up, g_ret_norm, w_out, g_mem_q, g_mem_kv, w_mem_q, w_mem_k, w_mem_v, w_mem_o, g_ffn, w_ffn_up, w_ffn_conv, b_ffn_conv, w_ffn_down, g_final):
    depth = w_in.shape[0]
    layers = [_prep_layer(l, g_mix, w_in, g_q_lora, w_q_up, g_kv_lora, w_kv_up, g_ret_norm, w_out,
                          g_mem_q, g_mem_kv, w_mem_q, w_mem_k, w_mem_v, w_mem_o,
                          g_ffn, w_ffn_up, w_ffn_conv, b_ffn_conv, w_ffn_down) for l in range(depth)]
    gfin = g_final.reshape(1, -1).astype(F32)
    mw = MEM_HEADS * MEM_HD

    xp = x_prompt
    Bp, Tp, _ = xp.shape
    tabs_p = _rope_tables(jnp.arange(Tp))
    p_states, p_mem_k, p_mem_v = [], [], []
    for l in range(depth):
        mk, mv, mk_b, mv_b = _mem_kv(mem_prompt, layers[l])
        xp, st = _layer(xp, tabs_p, layers[l], mk_b, mv_b, None, gfin, q0=0, final=(l == depth - 1))
        p_states.append(st)
        p_mem_k.append(mk.reshape(Bp, -1, MEM_HEADS, MEM_HD))
        p_mem_v.append(mv.reshape(Bp, -1, MEM_HEADS, MEM_HD))

    xs = x_sample
    Bs, Ts, _ = xs.shape
    past_len = cache_mla_ckv.shape[2]
    tabs_s = _rope_tables(past_len + jnp.arange(Ts))
    s_states = []
    for l in range(depth):
        past = (cache_mla_ckv[l], cache_mla_kpe[l], state_ret[l], cache_sb_k[l], cache_sb_v[l], state_ffn_conv[l])
        mk_b = cache_mem_k[l].reshape(Bs, -1, mw).astype(BF16)
        mv_b = cache_mem_v[l].reshape(Bs, -1, mw).astype(BF16)
        xs, st = _layer(xs, tabs_s, layers[l], mk_b, mv_b, past, gfin, q0=past_len, final=(l == depth - 1))
        s_states.append(st)

    p_ckv, p_kpe, p_ret, p_sbk, p_sbv, p_conv = [jnp.stack(f) for f in zip(*p_states)]
    s_ckv, s_kpe, s_ret, s_sbk, s_sbv, s_conv = [jnp.stack(f) for f in zip(*s_states)]
    return (xp, xs, p_ckv, p_kpe, p_ret, p_sbk, p_sbv, jnp.stack(p_mem_k), jnp.stack(p_mem_v), p_conv,
            s_ckv, s_kpe, s_ret, s_sbk, s_sbv, s_conv)
```

```python
import functools

import jax
import jax.numpy as jnp
from jax import lax
from jax.experimental import pallas as pl
from jax.experimental.pallas import tpu as pltpu

F32 = jnp.float32
BF16 = jnp.bfloat16

CHUNK = 64
EPS = 1e-6
ROPE_BASE = 10000.0

MLA_HEADS = 8
MLA_NOPE = 64
MLA_ROPE = 32
MLA_V = 64
MLA_Q_LORA = 256
MLA_KV_LORA = 128
MLA_SCALE = (MLA_NOPE + MLA_ROPE) ** -0.5
_LOG2E = 1.4426950408889634
_MLA_EXP2_SCALE = MLA_SCALE * _LOG2E
_SB_UNDERFLOW_LOG2 = -160.0
RET_HEADS = 4
RET_DK = 64
RET_DV = 64
RET_SCALE = RET_DK ** -0.5
SB_HEADS = 4
SB_HD = 64
SB_SCALE = SB_HD ** -0.5
MEM_HEADS = 4
MEM_HD = 128
MEM_SCALE = MEM_HD ** -0.5
D_FF = 2816
CONV_W = 3

IN_SIZES = (MLA_Q_LORA, MLA_KV_LORA, MLA_ROPE, RET_HEADS * RET_DK, RET_HEADS * RET_DK,
            RET_HEADS * RET_DV, RET_HEADS * RET_DV, SB_HEADS * SB_HD, SB_HEADS * SB_HD, SB_HEADS * SB_HD)

LANES = 128
HALF = LANES // 2
VMEM_LIMIT = 56 * 1024 * 1024

TM_PROJ = 512
TM_KVUP = 512
TQ_MLA = 512
TK_MLA = 512
TQ_SB = 256
TK_SB = 256
TC_RET = 512
L_RET = 256
TM_OUT = 512
TM_FFN = 512
TF_FFN = 256


def _tile(n, target):
    t = min(n, target)
    while n % t:
        t -= 1
    return t


def _cparams(n_axes):
    return pltpu.CompilerParams(dimension_semantics=("arbitrary",) * n_axes, vmem_limit_bytes=VMEM_LIMIT)


def _rms(x, g):
    return x * lax.rsqrt(jnp.mean(x * x, axis=-1, keepdims=True) + EPS) * g


def _lane_tile(x, reps):
    return jnp.concatenate([x] * reps, axis=1) if reps > 1 else x


def _full(shape):
    n = len(shape)
    return pl.BlockSpec(shape, lambda *_: (0,) * n)


def _dot(a, b):
    return jnp.dot(a, b, preferred_element_type=F32)


def _dot_nt(a, b):
    return lax.dot_general(a, b, (((1,), (1,)), ((), ())), preferred_element_type=F32)


_C_CQ = 0
_C_CKV = 256
_C_KPE = 384
_C_RQ = 512
_C_RK = 768
_C_RV = 1024
_C_RG = 1280
_C_SQ = 1536
_C_SK = 1792
_C_SV = 2048
_W1_COLS = 2304


def _rope(x, cos, sin_lo, sin_hi, half):
    return x * cos + pltpu.roll(x, half, 1) * sin_hi + pltpu.roll(x, LANES - half, 1) * sin_lo


def _kv_slots(c, kp, wke_ref, wv_ref, kcat_ref, v_ref):
    ck = jnp.concatenate([c, kp], axis=1)
    for p in range(MLA_HEADS // 2):
        k2 = _dot(ck, wke_ref[:, 2 * LANES * p:2 * LANES * (p + 1)])
        kcat_ref[0, 2 * p] = k2[:, :LANES].astype(BF16)
        kcat_ref[0, 2 * p + 1] = k2[:, LANES:].astype(BF16)
    for p in range(MLA_HEADS // 4):
        v2 = _dot(c, wv_ref[:, 2 * LANES * p:2 * LANES * (p + 1)])
        v_ref[0, 2 * p] = v2[:, :LANES].astype(BF16)
        v_ref[0, 2 * p + 1] = v2[:, LANES:].astype(BF16)


_IN_PROJ_INPUTS = 17


def _in_proj_body(*refs, n_prev):
    (x_ref, gmix_ref, w1_ref, gq_ref, wq_ref, gkv_ref, wke_ref, wv_ref,
     cslot_ref, slo_slot_ref, shi_slot_ref, ck_ref, slo_k_ref, shi_k_ref, cr_ref, slo_r_ref,
     shi_r_ref) = refs[:_IN_PROJ_INPUTS]
    outs = refs[_IN_PROJ_INPUTS + n_prev:]
    (ckv_ref, kpe_ref, kpeb_ref, qcat_ref, rq_ref, rk_ref, rv_ref, rg_ref,
     sq_ref, sk_ref, sv_ref, skb_ref, svb_ref) = outs[:13]
    kv_refs = outs[13:]
    h = _rms(x_ref[0], gmix_ref[...]).astype(BF16)

    def proj(c0, width):
        return _dot(h, w1_ref[:, c0:c0 + width])

    cq = proj(_C_CQ, 256)
    ckv_raw = proj(_C_CKV, 128)
    kpe_raw = proj(_C_KPE, 128)
    rq = proj(_C_RQ, 256)
    rk = proj(_C_RK, 256)
    rv = proj(_C_RV, 256)
    rg = proj(_C_RG, 256)
    sq = proj(_C_SQ, 256)
    sk = proj(_C_SK, 256)
    sv = proj(_C_SV, 256)

    ckv = _rms(ckv_raw, gkv_ref[...])
    ckv_ref[0, 0] = ckv
    kpe = _rope(kpe_raw, ck_ref[...], slo_k_ref[...], shi_k_ref[...], MLA_ROPE // 2)
    kpe_ref[0, 0] = kpe[:, :MLA_ROPE]
    kpe_b = kpe.astype(BF16)
    kpeb_ref[0] = kpe_b
    cqn = _rms(cq, gq_ref[...]).astype(BF16)
    cslot, slo_slot, shi_slot = cslot_ref[...], slo_slot_ref[...], shi_slot_ref[...]
    for p in range(MLA_HEADS // 2):
        q2 = _dot(cqn, wq_ref[:, 2 * LANES * p:2 * LANES * (p + 1)])
        for a in range(2):
            qs = q2[:, a * LANES:(a + 1) * LANES]
            qcat_ref[0, 2 * p + a] = _rope(qs, cslot, slo_slot, shi_slot, MLA_ROPE // 2).astype(BF16)
    if kv_refs:
        _kv_slots(ckv.astype(BF16), kpe_b, wke_ref, wv_ref, *kv_refs)

    cr, slo_r, shi_r = cr_ref[...], slo_r_ref[...], shi_r_ref[...]
    sk_ref[0, 0] = sk
    sv_ref[0, 0] = sv
    for p in range(2):
        sl = slice(p * LANES, (p + 1) * LANES)
        rq_ref[0, p] = _rope(rq[:, sl], cr, slo_r, shi_r, RET_DK // 2).astype(BF16)
        rk_ref[0, p] = (_rope(rk[:, sl], cr, slo_r, shi_r, RET_DK // 2) * RET_SCALE).astype(BF16)
        rv_ref[0, p] = rv[:, sl].astype(BF16)
        rg_ref[0, p] = rg[:, sl]
        sq_ref[0, p] = (sq[:, sl] * SB_SCALE).astype(BF16)
        skb_ref[0, p] = sk[:, sl].astype(BF16)
        svb_ref[0, p] = sv[:, sl].astype(BF16)


_STACKED_OUTS = (0, 1, 9, 10)


def _in_proj(x, tabs, lw, *, fuse_kv, layer, depth, prev):
    B, T, D = x.shape
    tm = _tile(T, TM_PROJ)
    grid = (B, T // tm)
    row = pl.BlockSpec((tm, LANES), lambda b, t: (t, 0))
    pair = lambda n: pl.BlockSpec((1, n, tm, LANES), lambda b, t: (b, 0, t, 0))
    nat = lambda w: pl.BlockSpec((1, tm, w), lambda b, t: (b, t, 0))
    stacked = lambda w: pl.BlockSpec((1, 1, tm, w), lambda b, t: (layer, b, t, 0))
    in_specs = [nat(D), _full((1, D)), _full((D, _W1_COLS)), _full((1, MLA_Q_LORA)),
                _full((MLA_Q_LORA, MLA_HEADS * LANES)), _full((1, MLA_KV_LORA)),
                _full((MLA_KV_LORA + LANES, MLA_HEADS * LANES)), _full((MLA_KV_LORA, MLA_HEADS * MLA_V))] + [row] * 9
    assert len(in_specs) == _IN_PROJ_INPUTS
    out_shape = [
        jax.ShapeDtypeStruct((depth, B, T, MLA_KV_LORA), F32),
        jax.ShapeDtypeStruct((depth, B, T, MLA_ROPE), F32),
        jax.ShapeDtypeStruct((B, T, LANES), BF16),
        jax.ShapeDtypeStruct((B, MLA_HEADS, T, LANES), BF16),
        jax.ShapeDtypeStruct((B, 2, T, LANES), BF16),
        jax.ShapeDtypeStruct((B, 2, T, LANES), BF16),
        jax.ShapeDtypeStruct((B, 2, T, LANES), BF16),
        jax.ShapeDtypeStruct((B, 2, T, LANES), F32),
        jax.ShapeDtypeStruct((B, 2, T, LANES), BF16),
        jax.ShapeDtypeStruct((depth, B, T, 256), F32),
        jax.ShapeDtypeStruct((depth, B, T, 256), F32),
        jax.ShapeDtypeStruct((B, 2, T, LANES), BF16),
        jax.ShapeDtypeStruct((B, 2, T, LANES), BF16),
    ]
    out_specs = [stacked(MLA_KV_LORA), stacked(MLA_ROPE), nat(LANES), pair(MLA_HEADS), pair(2), pair(2), pair(2),
                 pair(2), pair(2), stacked(256), stacked(256), pair(2), pair(2)]
    if fuse_kv:
        out_shape += [jax.ShapeDtypeStruct((B, MLA_HEADS, T, LANES), BF16),
                      jax.ShapeDtypeStruct((B, MLA_HEADS // 2, T, LANES), BF16)]
        out_specs += [pair(MLA_HEADS), pair(MLA_HEADS // 2)]
    prev = () if prev is None else tuple(prev)
    aliases = {_IN_PROJ_INPUTS + n: o for n, o in enumerate(_STACKED_OUTS[:len(prev)])}
    return pl.pallas_call(
        functools.partial(_in_proj_body, n_prev=len(prev)), grid=grid,
        in_specs=in_specs + [pl.BlockSpec(memory_space=pl.ANY)] * len(prev),
        out_specs=out_specs, out_shape=out_shape, input_output_aliases=aliases,
        compiler_params=_cparams(2), name="in_proj",
    )(x, lw["g_mix"], lw["w1"], lw["g_q"], lw["wq"], lw["g_kv"], lw["wke"], lw["wv"],
      tabs["cslot"], tabs["slo_slot"], tabs["shi_slot"], tabs["ck"], tabs["slo_k"], tabs["shi_k"],
      tabs["cr"], tabs["slo_r"], tabs["shi_r"], *prev)


def _kv_up_body(ckv_ref, kpe_ref, wke_ref, wv_ref, kcat_ref, v_ref):
    _kv_slots(ckv_ref[0].astype(BF16), kpe_ref[0], wke_ref, wv_ref, kcat_ref, v_ref)


def _kv_up(ckv, kpe_pad, lw):
    B, T, _ = ckv.shape
    tm = _tile(T, TM_KVUP)
    nat = lambda w: pl.BlockSpec((1, tm, w), lambda b, t: (b, t, 0))
    pair = lambda n: pl.BlockSpec((1, n, tm, LANES), lambda b, t: (b, 0, t, 0))
    return pl.pallas_call(
        _kv_up_body, grid=(B, T // tm),
        in_specs=[nat(MLA_KV_LORA), nat(LANES), _full((MLA_KV_LORA + LANES, MLA_HEADS * LANES)),
                  _full((MLA_KV_LORA, MLA_HEADS * MLA_V))],
        out_specs=[pair(MLA_HEADS), pair(MLA_HEADS // 2)],
        out_shape=[jax.ShapeDtypeStruct((B, MLA_HEADS, T, LANES), BF16),
                   jax.ShapeDtypeStruct((B, MLA_HEADS // 2, T, LANES), BF16)],
        compiler_params=_cparams(2), name="kv_up",
    )(ckv, kpe_pad, lw["wke"], lw["wv"])


def _mla_last_block(i, *, q0, tq, tk, nk):
    last_q = q0 + i * tq + (tq - 1)
    last_key = (last_q // CHUNK) * CHUNK + (CHUNK - 1)
    return jnp.minimum(last_key // tk, nk - 1)


def _mla_body(q_ref, k_ref, v_ref, o_ref, m_ref, l_ref, acc_ref, *, q0, tq, tk, nk, kv_len):
    i = pl.program_id(1)
    j = pl.program_id(2)

    @pl.when(j == 0)
    def _():
        m_ref[...] = jnp.full(m_ref.shape, -jnp.inf, F32)
        l_ref[...] = jnp.zeros(l_ref.shape, F32)
        acc_ref[...] = jnp.zeros(acc_ref.shape, F32)

    def step(masked):
        lane = lax.broadcasted_iota(jnp.int32, (tq, LANES), 1)
        first = lane < HALF
        if masked:
            qpos = q0 + i * tq + lax.broadcasted_iota(jnp.int32, (tq, 1), 0)
            limit = jnp.minimum((jnp.right_shift(qpos, 6) + 1) * CHUNK, kv_len)
            kpos = j * tk + lax.broadcasted_iota(jnp.int32, (tq, tk), 1)
            bias = jnp.where(kpos < limit, 0.0, -jnp.inf).astype(F32)

        def scores(hh):
            s = _dot_nt(q_ref[0, hh], k_ref[0, hh])
            return s + bias if masked else s

        s_next = scores(0)
        alphas, pvs = [], []
        for hh in range(MLA_HEADS):
            p, a = divmod(hh, 2)
            s = s_next
            if hh + 1 < MLA_HEADS:
                s_next = scores(hh + 1)
            m_prev = m_ref[hh]
            m_new = jnp.maximum(m_prev, jnp.max(s, axis=1, keepdims=True))
            e = jnp.exp2((s - _lane_tile(m_new, tk // LANES)) * _MLA_EXP2_SCALE)
            alpha = jnp.exp2((m_prev - m_new) * _MLA_EXP2_SCALE)
            l_ref[hh] = alpha * l_ref[hh] + jnp.sum(e, axis=1, keepdims=True)
            m_ref[hh] = m_new
            alphas.append(alpha)
            pvs.append(_dot(e.astype(BF16), v_ref[0, p]))
            if a == 1:
                acc_ref[p] = acc_ref[p] * jnp.where(first, alphas[0], alphas[1]) + jnp.where(first, pvs[0], pvs[1])
                alphas, pvs = [], []

    all_visible = jnp.minimum(((q0 + i * tq) // CHUNK + 1) * CHUNK, kv_len)
    fully_visible = (j + 1) * tk <= all_visible
    needed = j <= _mla_last_block(i, q0=q0, tq=tq, tk=tk, nk=nk)

    @pl.when(jnp.logical_and(needed, fully_visible))
    def _():
        step(False)

    @pl.when(jnp.logical_and(needed, jnp.logical_not(fully_visible)))
    def _():
        step(True)

    @pl.when(j == nk - 1)
    def _():
        lane = lax.broadcasted_iota(jnp.int32, (tq, LANES), 1)
        for p in range(MLA_HEADS // 2):
            inv = jnp.where(lane < HALF, 1.0 / l_ref[2 * p], 1.0 / l_ref[2 * p + 1])
            o_ref[0, p] = (acc_ref[p] * inv).astype(BF16)


def _mla_attention(qcat, kcat, v, *, q0, kv_len, tq, tk):
    B, H, Tq, _ = qcat.shape
    Tk = kcat.shape[2]
    nq, nk = Tq // tq, Tk // tk
    kw = dict(q0=q0, tq=tq, tk=tk, nk=nk)

    def kv_map(b, i, j):
        return (b, 0, jnp.minimum(j, _mla_last_block(i, **kw)), 0)

    return pl.pallas_call(
        functools.partial(_mla_body, kv_len=kv_len, **kw),
        grid=(B, nq, nk),
        in_specs=[pl.BlockSpec((1, H, tq, LANES), lambda b, i, j: (b, 0, i, 0)),
                  pl.BlockSpec((1, H, tk, LANES), kv_map),
                  pl.BlockSpec((1, H // 2, tk, LANES), kv_map)],
        out_specs=pl.BlockSpec((1, H // 2, tq, LANES), lambda b, i, j: (b, 0, i, 0)),
        out_shape=jax.ShapeDtypeStruct((B, H // 2, Tq, LANES), BF16),
        scratch_shapes=[pltpu.VMEM((H, tq, LANES), F32), pltpu.VMEM((H, tq, LANES), F32),
                        pltpu.VMEM((H // 2, tq, LANES), F32)],
        compiler_params=_cparams(3), name="mla_attn",
    )(qcat, kcat, v)


def _sb_last_block(i, *, q0, tq, tk, nk):
    last_key = q0 + i * tq + (tq - 1) - 1
    return jnp.clip(last_key // tk, 0, nk - 1)


def _sb_body(q_ref, k_ref, v_ref, u_ref, o_ref, run_ref, acc_ref, *, q0, tq, tk, nk):
    i = pl.program_id(1)
    q_first = q0 + i * tq
    last = _sb_last_block(i, q0=q0, tq=tq, tk=tk, nk=nk)
    first_unmasked = jnp.minimum(q_first // tk - 1, last)
    run_ref[...] = jnp.zeros(run_ref.shape, F32)
    acc_ref[...] = jnp.zeros(acc_ref.shape, F32)

    def step(kb, masked):
        first_k = lax.broadcasted_iota(jnp.int32, (tk, LANES), 1) < HALF
        tri = u_ref[...]
        reps = tk // LANES
        rows = pl.ds(pl.multiple_of(kb * tk, tk), tk)
        if masked:
            qpos = q_first + lax.broadcasted_iota(jnp.int32, (tq, 1), 0)
            kpos = kb * tk + lax.broadcasted_iota(jnp.int32, (tq, tk), 1)
            before1 = kpos < qpos
            before = jnp.concatenate([before1, before1], axis=1)

        def scores(p):
            kp = k_ref[0, p, rows, :]
            zero = jnp.zeros_like(kp)
            k2 = jnp.concatenate([jnp.where(first_k, kp, zero), jnp.where(first_k, zero, kp)], axis=0)
            return _dot_nt(q_ref[0, p], k2) * _LOG2E

        npair = SB_HEADS // 2
        zs = [scores(p) for p in range(npair)]
        keeps, betas, betweens = [], [], []
        for p in range(npair):
            z = zs[p]
            nz = -z
            sp = jnp.log(1.0 + jnp.exp2(jnp.minimum(z, nz))) * _LOG2E
            log_keep = jnp.minimum(nz, 0.0) - sp
            betas.append(log_keep + z)
            if masked:
                log_keep = jnp.where(before, log_keep, 0.0)
            keeps.append(log_keep)
            hi = log_keep.astype(BF16)
            lo = (log_keep - hi.astype(F32)).astype(BF16)
            betweens.append(jnp.concatenate(
                [_dot(hi[:, :tk], tri) + _dot(lo[:, :tk], tri), _dot(hi[:, tk:], tri) + _dot(lo[:, tk:], tri)], axis=1))
        for p in range(npair):
            vp = v_ref[0, p, rows, :]
            zero = jnp.zeros_like(vp)
            v2 = jnp.concatenate([jnp.where(first_k, vp, zero), jnp.where(first_k, zero, vp)], axis=0)
            run_a = run_ref[2 * p]
            run_b = run_ref[2 * p + 1]
            run = jnp.concatenate([_lane_tile(run_a, reps), _lane_tile(run_b, reps)], axis=1)
            att = jnp.exp2(betas[p] + betweens[p] + run)
            if masked:
                att = jnp.where(before, att, 0.0)
            run_ref[2 * p] = run_a + jnp.sum(keeps[p][:, :tk], axis=1, keepdims=True)
            run_ref[2 * p + 1] = run_b + jnp.sum(keeps[p][:, tk:], axis=1, keepdims=True)
            acc_ref[p] = acc_ref[p] + _dot(att.astype(BF16), v2)

    def masked_step(t, carry):
        step(last - t, True)
        return carry

    lax.fori_loop(0, last - first_unmasked, masked_step, 0)

    def more(carry):
        kb, live = carry
        return jnp.logical_and(kb >= 0, live > 0)

    def unmasked_step(carry):
        kb, _ = carry
        step(kb, False)
        live = (jnp.max(run_ref[...]) >= _SB_UNDERFLOW_LOG2).astype(jnp.int32)
        return kb - 1, live

    lax.while_loop(more, unmasked_step, (first_unmasked, jnp.int32(1)))
    o_ref[0] = acc_ref[...].astype(BF16)


def _sb_attention(q, k, v, *, q0, tq, tk):
    B, P, Tq, _ = q.shape
    Tk = k.shape[2]
    nq, nk = Tq // tq, Tk // tk
    kw = dict(q0=q0, tq=tq, tk=tk, nk=nk)
    r = lax.broadcasted_iota(jnp.int32, (tk, tk), 0)
    c = lax.broadcasted_iota(jnp.int32, (tk, tk), 1)
    tri = (r > c).astype(BF16)

    kv_spec = pl.BlockSpec((1, P, Tk, LANES), lambda b, i: (b, 0, 0, 0))
    return pl.pallas_call(
        functools.partial(_sb_body, **kw),
        grid=(B, nq),
        in_specs=[pl.BlockSpec((1, P, tq, LANES), lambda b, i: (b, 0, i, 0)), kv_spec, kv_spec, _full((tk, tk))],
        out_specs=pl.BlockSpec((1, P, tq, LANES), lambda b, i: (b, 0, i, 0)),
        out_shape=jax.ShapeDtypeStruct((B, P, Tq, LANES), BF16),
        scratch_shapes=[pltpu.VMEM((2 * P, tq, LANES), F32), pltpu.VMEM((P, tq, LANES), F32)],
        compiler_params=_cparams(2), name="sb_attn",
    )(q, k, v, tri)


def _ret_body(q_ref, k_ref, v_ref, g_ref, s0_ref, dmask_ref, dq_ref, dk_ref, cd_ref, bd_ref, gn_ref,
              y_ref, snew_ref, state_ref, *, tc, lc, nc):
    c = pl.program_id(1)

    @pl.when(c == 0)
    def _():
        state_ref[...] = s0_ref[0]

    lane = lax.broadcasted_iota(jnp.int32, (lc, LANES), 1)
    first = lane < HALF
    for p in range(RET_HEADS // 2):
        for ch in range(tc // lc):
            rows = slice(ch * lc, (ch + 1) * lc)
            q = q_ref[0, p, rows, :]
            k = k_ref[0, p, rows, :]
            v = v_ref[0, p, rows, :]
            state = state_ref[p]
            cross = _dot((q.astype(F32) * dq_ref[p]).astype(BF16), state.astype(BF16))
            zero = jnp.zeros_like(q)
            inner = []
            for a in range(2):
                qa = jnp.where(first, q, zero) if a == 0 else jnp.where(first, zero, q)
                scores = _dot_nt(qa, k) * dmask_ref[2 * p + a]
                inner.append(_dot(scores.astype(BF16), v))
            o = jnp.where(first, inner[0], inner[1]) + cross
            kd = (k.astype(F32) * dk_ref[p]).astype(BF16)
            kv = lax.dot_general(kd, v, (((0,), (0,)), ((), ())), preferred_element_type=F32)
            state_ref[p] = cd_ref[p] * state + bd_ref[...] * kv
            zf = jnp.zeros_like(o)
            mu = jnp.where(first, jnp.sum(jnp.where(first, o, zf), axis=1, keepdims=True),
                           jnp.sum(jnp.where(first, zf, o), axis=1, keepdims=True)) * (1.0 / RET_DV)
            d = o - mu
            d2 = d * d
            var = jnp.where(first, jnp.sum(jnp.where(first, d2, zf), axis=1, keepdims=True),
                            jnp.sum(jnp.where(first, zf, d2), axis=1, keepdims=True)) * (1.0 / RET_DV)
            yn = d * lax.rsqrt(var + EPS) * gn_ref[p]
            gate = g_ref[0, p, rows, :]
            y_ref[0, p, rows, :] = (yn * (gate * jax.nn.sigmoid(gate))).astype(BF16)

    @pl.when(c == nc - 1)
    def _():
        snew_ref[0] = state_ref[...]


def _ret_tables(lc):
    log_gamma = jnp.log(1.0 - 2.0 ** (-5.0 - jnp.arange(RET_HEADS, dtype=F32)))
    idx = jnp.arange(lc, dtype=F32)
    diff = idx[:, None] - idx[None, :]
    dmask = jnp.where(diff[None] >= 0, jnp.exp(jnp.maximum(diff, 0.0)[None] * log_gamma[:, None, None]), 0.0)
    dk = jnp.exp((lc - 1.0 - idx)[:, None] * log_gamma[None, :])
    dq = jnp.exp((idx + 1.0)[:, None] * log_gamma[None, :])
    cd = jnp.exp(lc * log_gamma)

    def lanes(t):
        t = jnp.repeat(t[:, :, None], HALF, axis=2).reshape(t.shape[0], RET_HEADS // 2, LANES)
        return jnp.transpose(t, (1, 0, 2))

    blk = jnp.arange(LANES) // HALF
    bd = (blk[:, None] == blk[None, :]).astype(F32)
    cd_rows = jnp.repeat(cd, HALF).reshape(RET_HEADS // 2, LANES, 1)
    return dmask, lanes(dq), lanes(dk), cd_rows * bd[None], bd


def _state_to_pairs(s):
    B = s.shape[0]
    s = s.astype(F32).reshape(B, RET_HEADS // 2, 2, RET_DK, RET_DV)
    z = jnp.zeros_like(s[:, :, 0])
    top = jnp.concatenate([s[:, :, 0], z], axis=-1)
    bot = jnp.concatenate([z, s[:, :, 1]], axis=-1)
    return jnp.concatenate([top, bot], axis=-2)


def _pairs_to_state(sp):
    a = sp[:, :, :HALF, :HALF]
    b = sp[:, :, HALF:, HALF:]
    B = sp.shape[0]
    return jnp.stack([a, b], axis=2).reshape(B, RET_HEADS, RET_DK, RET_DV)


def _retention(rq, rk, rv, rg, s0_pairs, g_norm_pairs, *, lc):
    B, P, T, _ = rq.shape
    tc = _tile(T, TC_RET)
    nc = T // tc
    dmask, dq, dk, cd, bd = _ret_tables(lc)
    pair = pl.BlockSpec((1, P, tc, LANES), lambda b, c: (b, 0, c, 0))
    st = pl.BlockSpec((1, P, LANES, LANES), lambda b, c: (b, 0, 0, 0))
    return pl.pallas_call(
        functools.partial(_ret_body, tc=tc, lc=lc, nc=nc),
        grid=(B, nc),
        in_specs=[pair, pair, pair, pair, st, _full(dmask.shape), _full(dq.shape), _full(dk.shape),
                  _full(cd.shape), _full(bd.shape), _full(g_norm_pairs.shape)],
        out_specs=[pair, st],
        out_shape=[jax.ShapeDtypeStruct((B, P, T, LANES), BF16),
                   jax.ShapeDtypeStruct((B, P, LANES, LANES), F32)],
        scratch_shapes=[pltpu.VMEM((P, LANES, LANES), F32)],
        compiler_params=_cparams(2), name="retention",
    )(rq, rk, rv, rg, s0_pairs, dmask, dq, dk, cd, bd, g_norm_pairs)


def _out_mem_body(x_ref, omla_ref, ret_ref, osb_ref, wout_ref, gq_ref, wq_ref, mk_ref, mv_ref, wo_ref, y_ref):
    mixed = jnp.concatenate([ref[0, p] for ref, n in ((omla_ref, MLA_HEADS // 2), (ret_ref, RET_HEADS // 2),
                                                      (osb_ref, SB_HEADS // 2)) for p in range(n)], axis=1)
    x1 = x_ref[0] + _dot(mixed, wout_ref[...])
    hq = _rms(x1, gq_ref[...]).astype(BF16)
    q = _dot(hq, wq_ref[...]).astype(BF16)
    cols = [slice(hh * MEM_HD, (hh + 1) * MEM_HD) for hh in range(MEM_HEADS)]
    scores = [_dot_nt(q[:, sl], mk_ref[0, :, sl]) * MEM_SCALE for sl in cols]
    probs = []
    for s in scores:
        e = jnp.exp(s - jnp.max(s, axis=1, keepdims=True))
        probs.append((e / jnp.sum(e, axis=1, keepdims=True)).astype(BF16))
    heads = [_dot(pr, mv_ref[0, :, sl]).astype(BF16) for pr, sl in zip(probs, cols)]
    o = jnp.concatenate(heads, axis=1)
    y_ref[0] = x1 + _dot(o, wo_ref[...])


def _out_mem(x, o_mla, ret_y, o_sb, mem_k, mem_v, lw):
    B, T, D = x.shape
    tm = _tile(T, TM_OUT)
    M = mem_k.shape[1]
    mw = MEM_HEADS * MEM_HD
    nat = pl.BlockSpec((1, tm, D), lambda b, t: (b, t, 0))
    pair = lambda n: pl.BlockSpec((1, n, tm, LANES), lambda b, t: (b, 0, t, 0))
    mem = pl.BlockSpec((1, M, mw), lambda b, t: (b, 0, 0))
    return pl.pallas_call(
        _out_mem_body, grid=(B, T // tm),
        in_specs=[nat, pair(MLA_HEADS // 2), pair(RET_HEADS // 2), pair(SB_HEADS // 2), _full((D, D)),
                  _full((1, D)), _full((D, mw)), mem, mem, _full((mw, D))],
        out_specs=nat, out_shape=jax.ShapeDtypeStruct((B, T, D), F32),
        compiler_params=_cparams(2), name="out_mem",
    )(x, o_mla, ret_y, o_sb, lw["w_out"], lw["g_mem_q"], lw["w_mem_q"], mem_k, mem_v, lw["w_mem_o"])


def _mem_kv_body(mem_ref, g_ref, wk_ref, wv_ref, k_ref, v_ref, kb_ref, vb_ref):
    m = _rms(mem_ref[0], g_ref[...]).astype(BF16)
    k = _dot(m, wk_ref[...])
    v = _dot(m, wv_ref[...])
    k_ref[0] = k
    v_ref[0] = v
    kb_ref[0] = k.astype(BF16)
    vb_ref[0] = v.astype(BF16)


def _mem_kv(mem, lw):
    B, M, D = mem.shape
    mw = MEM_HEADS * MEM_HD
    o = pl.BlockSpec((1, M, mw), lambda b: (b, 0, 0))
    return pl.pallas_call(
        _mem_kv_body, grid=(B,),
        in_specs=[pl.BlockSpec((1, M, D), lambda b: (b, 0, 0)), _full((1, D)), _full((D, mw)), _full((D, mw))],
        out_specs=[o, o, o, o],
        out_shape=[jax.ShapeDtypeStruct((B, M, mw), F32), jax.ShapeDtypeStruct((B, M, mw), F32),
                   jax.ShapeDtypeStruct((B, M, mw), BF16), jax.ShapeDtypeStruct((B, M, mw), BF16)],
        compiler_params=_cparams(1), name="mem_kv",
    )(mem, lw["g_mem_kv"], lw["w_mem_k"], lw["w_mem_v"])


_CARRY_ROW = 8 - (CONV_W - 1)


def _ffn_body(x_ref, g_ref, wup_ref, wconv_ref, bconv_ref, wdown_ref, conv0_ref, gfin_ref,
              y_ref, convnew_ref, carry_ref, exta_ref, extb_ref, act_ref, *, tm, nt, final):
    t = pl.program_id(1)

    @pl.when(t == 0)
    def _():
        carry_ref[_CARRY_ROW:8, :] = conv0_ref[0]

    x = x_ref[0]
    h = _rms(x, g_ref[...]).astype(BF16)

    def conv_part(c0, ext_ref):
        cols = slice(c0, c0 + TF_FFN)
        u = _dot(h, wup_ref[:, cols])
        ext_ref[_CARRY_ROW:8, :] = carry_ref[_CARRY_ROW:8, cols]
        ext_ref[8:8 + tm, :] = u
        carry_ref[_CARRY_ROW:8, cols] = u[tm - (CONV_W - 1):, :]
        w = wconv_ref[:, cols]
        c = bconv_ref[:, cols]
        c = c + ext_ref[_CARRY_ROW:_CARRY_ROW + tm, :] * w[0:1]
        c = c + ext_ref[_CARRY_ROW + 1:_CARRY_ROW + 1 + tm, :] * w[1:2]
        return c + u * w[2:3]

    for ci in range(D_FF // TF_FFN):
        c0 = ci * TF_FFN
        a = conv_part(c0, exta_ref)
        b = conv_part(D_FF + c0, extb_ref)
        act_ref[:, c0:c0 + TF_FFN] = (a * jax.nn.sigmoid(a) * b).astype(BF16)

    x3 = x + _dot(act_ref[...], wdown_ref[...])
    y_ref[0] = _rms(x3, gfin_ref[...]) if final else x3

    @pl.when(t == nt - 1)
    def _():
        convnew_ref[0] = carry_ref[_CARRY_ROW:8, :]


def _conv_ffn(x, conv0, lw, g_final, *, final):
    B, T, D = x.shape
    tm = _tile(T, TM_FFN)
    nt = T // tm
    f2 = 2 * D_FF
    nat = pl.BlockSpec((1, tm, D), lambda b, t: (b, t, 0))
    cv = pl.BlockSpec((1, CONV_W - 1, f2), lambda b, t: (b, 0, 0))
    return pl.pallas_call(
        functools.partial(_ffn_body, tm=tm, nt=nt, final=final),
        grid=(B, nt),
        in_specs=[nat, _full((1, D)), _full((D, f2)), _full((CONV_W, f2)), _full((1, f2)), _full((D_FF, D)),
                  cv, _full((1, D))],
        out_specs=[nat, cv],
        out_shape=[jax.ShapeDtypeStruct((B, T, D), F32), jax.ShapeDtypeStruct((B, CONV_W - 1, f2), F32)],
        scratch_shapes=[pltpu.VMEM((8, f2), F32), pltpu.VMEM((8 + tm, TF_FFN), F32),
                        pltpu.VMEM((8 + tm, TF_FFN), F32), pltpu.VMEM((tm, D_FF), BF16)],
        compiler_params=_cparams(2), name="conv_ffn",
    )(x, lw["g_ffn"], lw["w_ffn_up"], lw["w_ffn_conv"], lw["b_ffn_conv"], lw["w_ffn_down"], conv0, g_final)


def _pad_cols(w, width):
    return jnp.pad(w, ((0, 0), (0, width - w.shape[1])))


def _prep_layer(l, g_mix, w_in, g_q_lora, w_q_up, g_kv_lora, w_kv_up, g_ret_norm, w_out,
                g_mem_q, g_mem_kv, w_mem_q, w_mem_k, w_mem_v, w_mem_o,
                g_ffn, w_ffn_up, w_ffn_conv, b_ffn_conv, w_ffn_down):
    w = w_in[l]
    parts, c0 = [], 0
    for n in IN_SIZES:
        parts.append(w[:, c0:c0 + n])
        c0 += n
    cq, ckv, kpe, rq, rk, rv, rg, sq, sk, sv = parts
    w1 = jnp.concatenate([cq, ckv, _pad_cols(kpe, LANES), rq, rk, rv, rg, sq, sk, sv], axis=1).astype(BF16)
    kq = w_q_up.shape[1]
    wq3 = w_q_up[l].reshape(kq, MLA_HEADS, MLA_NOPE + MLA_ROPE)
    tail = LANES - MLA_NOPE - MLA_ROPE
    q_pad = jnp.pad(wq3, ((0, 0), (0, 0), (0, tail))).reshape(kq, MLA_HEADS * LANES)
    wkv3 = w_kv_up[l].reshape(MLA_KV_LORA, MLA_HEADS, MLA_NOPE + MLA_V)
    wk = jnp.pad(wkv3[:, :, :MLA_NOPE], ((0, 0), (0, 0), (0, LANES - MLA_NOPE))).reshape(MLA_KV_LORA, MLA_HEADS * LANES)
    wv = wkv3[:, :, MLA_NOPE:].reshape(MLA_KV_LORA, MLA_HEADS * MLA_V)
    r = jnp.arange(LANES)[:, None]
    c = jnp.arange(MLA_HEADS * LANES)[None, :]
    we = ((r < MLA_ROPE) & (c % LANES == r + MLA_NOPE)).astype(BF16)
    row = lambda g: g[l].reshape(1, -1).astype(F32)
    return dict(
        g_mix=row(g_mix), w1=w1, g_q=row(g_q_lora), wq=q_pad.astype(BF16),
        g_kv=row(g_kv_lora), wke=jnp.concatenate([wk.astype(BF16), we], axis=0), wv=wv.astype(BF16),
        g_ret=g_ret_norm[l].astype(F32).reshape(RET_HEADS // 2, 1, LANES),
        w_out=w_out[l].astype(BF16), g_mem_q=row(g_mem_q), g_mem_kv=row(g_mem_kv),
        w_mem_q=w_mem_q[l].astype(BF16), w_mem_k=w_mem_k[l].astype(BF16), w_mem_v=w_mem_v[l].astype(BF16),
        w_mem_o=w_mem_o[l].astype(BF16), g_ffn=row(g_ffn), w_ffn_up=w_ffn_up[l].astype(BF16),
        w_ffn_conv=w_ffn_conv[l].astype(F32), b_ffn_conv=row(b_ffn_conv), w_ffn_down=w_ffn_down[l].astype(BF16))


def _rope_tables(pos):
    t = pos.shape[0]
    posf = pos.astype(F32)[:, None]

    def cos_sin(d):
        inv = ROPE_BASE ** (-jnp.arange(0, d, 2, dtype=F32) / d)
        ang = posf * inv[None, :]
        return jnp.cos(ang), jnp.sin(ang)

    c16, s16 = cos_sin(MLA_ROPE)
    c32, s32 = cos_sin(RET_DK)
    z = lambda n: jnp.zeros((t, n), F32)
    tail = LANES - MLA_NOPE - MLA_ROPE
    return dict(
        cslot=jnp.concatenate([jnp.ones((t, MLA_NOPE), F32), c16, c16, z(tail)], axis=1),
        slo_slot=jnp.concatenate([z(MLA_NOPE), -s16, z(MLA_ROPE // 2 + tail)], axis=1),
        shi_slot=jnp.concatenate([z(MLA_NOPE + MLA_ROPE // 2), s16, z(tail)], axis=1),
        ck=jnp.concatenate([c16, c16, z(LANES - MLA_ROPE)], axis=1),
        slo_k=jnp.concatenate([-s16, z(LANES - MLA_ROPE // 2)], axis=1),
        shi_k=jnp.concatenate([z(MLA_ROPE // 2), s16, z(LANES - MLA_ROPE)], axis=1),
        cr=jnp.tile(jnp.concatenate([c32, c32], axis=1), (1, LANES // RET_DK)),
        slo_r=jnp.tile(jnp.concatenate([-s32, z(RET_DK // 2)], axis=1), (1, LANES // RET_DK)),
        shi_r=jnp.tile(jnp.concatenate([z(RET_DK // 2), s32], axis=1), (1, LANES // RET_DK)))


def _pad_rows(a, axis, n):
    pad = [(0, 0)] * a.ndim
    pad[axis] = (0, n - a.shape[axis])
    return jnp.pad(a, pad)


def _to_pairs(a):
    B, T, _ = a.shape
    return jnp.transpose(a.reshape(B, T, 2, LANES), (0, 2, 1, 3))


def _layer(x, tabs, lw, mem_k, mem_v, past, g_final, *, q0, layer, depth, prev):
    B, T, _ = x.shape
    final = layer == depth - 1
    outs = _in_proj(x, tabs, lw, fuse_kv=past is None, layer=layer, depth=depth, prev=prev)
    (ckv_st, kpe_st, kpe_b, qcat, rq, rk, rv, rg, sq, sk_st, sv_st, sk_b, sv_b) = outs[:13]
    if past is None:
        kcat, vmla = outs[13:]
        sk_all, sv_all = sk_b, sv_b
        s0 = jnp.zeros((B, RET_HEADS // 2, LANES, LANES), F32)
        conv0 = jnp.zeros((B, CONV_W - 1, 2 * D_FF), F32)
        kv_len = T
        lc = _tile(T, L_RET)
        tq_mla, tk_mla = _tile(T, TQ_MLA), _tile(T, TK_MLA)
        tq_sb, tk_sb = _tile(T, TQ_SB), _tile(T, TK_SB)
    else:
        ckv_c, kpe_c, s0, sk_c, sv_c, conv0 = past
        P = ckv_c.shape[1]
        kv_len = P + T
        tk_sb = TK_SB
        tk_pad = -(-kv_len // tk_sb) * tk_sb
        ckv_all = _pad_rows(jnp.concatenate([ckv_c.astype(F32), ckv_st[layer]], axis=1), 1, tk_pad)
        kpe_cb = _pad_cols(kpe_c.reshape(B * P, MLA_ROPE), LANES).reshape(B, P, LANES).astype(BF16)
        kpe_all = _pad_rows(jnp.concatenate([kpe_cb, kpe_b], axis=1), 1, tk_pad)
        sk_all = _pad_rows(jnp.concatenate([_to_pairs(sk_c.reshape(B, P, -1).astype(BF16)), sk_b], axis=2), 2, tk_pad)
        sv_all = _pad_rows(jnp.concatenate([_to_pairs(sv_c.reshape(B, P, -1).astype(BF16)), sv_b], axis=2), 2, tk_pad)
        s0 = _state_to_pairs(s0)
        conv0 = conv0.astype(F32)
        lc = T
        tq_mla, tk_mla = T, tk_pad
        tq_sb = T
        kcat, vmla = _kv_up(ckv_all, kpe_all, lw)
    o_mla = _mla_attention(qcat, kcat, vmla, q0=q0, kv_len=kv_len, tq=tq_mla, tk=tk_mla)
    o_sb = _sb_attention(sq, sk_all, sv_all, q0=q0, tq=tq_sb, tk=tk_sb)
    ret_y, s_new = _retention(rq, rk, rv, rg, s0, lw["g_ret"], lc=lc)
    x = _out_mem(x, o_mla, ret_y, o_sb, mem_k, mem_v, lw)
    x, conv_new = _conv_ffn(x, conv0, lw, g_final, final=final)
    return x, (ckv_st, kpe_st, sk_st, sv_st), (_pairs_to_state(s_new), conv_new)


def kernel(x_prompt, x_sample, cache_mla_ckv, cache_mla_kpe, state_ret, cache_sb_k, cache_sb_v, cache_mem_k, cache_mem_v, state_ffn_conv, mem_prompt, g_mix, w_in, g_q_lora, w_q_up, g_kv_lora, w_kv_up, g_ret_norm, w_out, g_mem_q, g_mem_kv, w_mem_q, w_mem_k, w_mem_v, w_mem_o, g_ffn, w_ffn_up, w_ffn_conv, b_ffn_conv, w_ffn_down, g_final):
    depth = w_in.shape[0]
    layers = [_prep_layer(l, g_mix, w_in, g_q_lora, w_q_up, g_kv_lora, w_kv_up, g_ret_norm, w_out,
                          g_mem_q, g_mem_kv, w_mem_q, w_mem_k, w_mem_v, w_mem_o,
                          g_ffn, w_ffn_up, w_ffn_conv, b_ffn_conv, w_ffn_down) for l in range(depth)]
    gfin = g_final.reshape(1, -1).astype(F32)
    mw = MEM_HEADS * MEM_HD

    xp = x_prompt
    Bp, Tp, _ = xp.shape
    tabs_p = _rope_tables(jnp.arange(Tp))
    p_states, p_mem_k, p_mem_v = [], [], []
    p_caches = None
    for l in range(depth):
        mk, mv, mk_b, mv_b = _mem_kv(mem_prompt, layers[l])
        xp, p_caches, st = _layer(xp, tabs_p, layers[l], mk_b, mv_b, None, gfin, q0=0, layer=l, depth=depth,
                                  prev=p_caches)
        p_states.append(st)
        p_mem_k.append(mk.reshape(Bp, -1, MEM_HEADS, MEM_HD))
        p_mem_v.append(mv.reshape(Bp, -1, MEM_HEADS, MEM_HD))

    xs = x_sample
    Bs, Ts, _ = xs.shape
    past_len = cache_mla_ckv.shape[2]
    tabs_s = _rope_tables(past_len + jnp.arange(Ts))
    s_states = []
    s_caches = None
    for l in range(depth):
        past = (cache_mla_ckv[l], cache_mla_kpe[l], state_ret[l], cache_sb_k[l], cache_sb_v[l], state_ffn_conv[l])
        mk_b = cache_mem_k[l].reshape(Bs, -1, mw).astype(BF16)
        mv_b = cache_mem_v[l].reshape(Bs, -1, mw).astype(BF16)
        xs, s_caches, st = _layer(xs, tabs_s, layers[l], mk_b, mv_b, past, gfin, q0=past_len, layer=l, depth=depth,
                                  prev=s_caches)
        s_states.append(st)

    def leaves(caches, states, b, t):
        ckv, kpe, sk, sv = caches
        ret, conv = [jnp.stack(f) for f in zip(*states)]
        heads = (depth, b, t, SB_HEADS, SB_HD)
        return ckv, kpe, ret, sk.reshape(heads), sv.reshape(heads), conv

    p_ckv, p_kpe, p_ret, p_sbk, p_sbv, p_conv = leaves(p_caches, p_states, Bp, Tp)
    s_ckv, s_kpe, s_ret, s_sbk, s_sbv, s_conv = leaves(s_caches, s_states, Bs, Ts)
    return (xp, xs, p_ckv, p_kpe, p_ret, p_sbk, p_sbv, jnp.stack(p_mem_k), jnp.stack(p_mem_v), p_conv,
            s_ckv, s_kpe, s_ret, s_sbk, s_sbv, s_conv)
```

```python
import functools

import jax
import jax.numpy as jnp
from jax import lax
from jax.experimental import pallas as pl
from jax.experimental.pallas import tpu as pltpu

F32 = jnp.float32
BF16 = jnp.bfloat16

CHUNK = 64
EPS = 1e-6
ROPE_BASE = 10000.0

MLA_HEADS = 8
MLA_NOPE = 64
MLA_ROPE = 32
MLA_V = 64
MLA_Q_LORA = 256
MLA_KV_LORA = 128
MLA_SCALE = (MLA_NOPE + MLA_ROPE) ** -0.5
_LOG2E = 1.4426950408889634
_MLA_EXP2_SCALE = MLA_SCALE * _LOG2E
_SB_UNDERFLOW_LOG2 = -160.0
RET_HEADS = 4
RET_DK = 64
RET_DV = 64
RET_SCALE = RET_DK ** -0.5
SB_HEADS = 4
SB_HD = 64
SB_SCALE = SB_HD ** -0.5
MEM_HEADS = 4
MEM_HD = 128
MEM_SCALE = MEM_HD ** -0.5
D_FF = 2816
CONV_W = 3

IN_SIZES = (MLA_Q_LORA, MLA_KV_LORA, MLA_ROPE, RET_HEADS * RET_DK, RET_HEADS * RET_DK,
            RET_HEADS * RET_DV, RET_HEADS * RET_DV, SB_HEADS * SB_HD, SB_HEADS * SB_HD, SB_HEADS * SB_HD)

LANES = 128
HALF = LANES // 2
VMEM_LIMIT = 56 * 1024 * 1024

TM_PROJ = 1024
TM_KVUP = 512
TQ_MLA = 512
TK_MLA = 512
TQ_SB = 256
TK_SB = 256
TC_RET = 512
L_RET = 256
TM_OUT = 1024
TM_FFN = 512
TF_FFN = 256


def _tile(n, target):
    t = min(n, target)
    while n % t:
        t -= 1
    return t


def _cparams(n_axes):
    return pltpu.CompilerParams(dimension_semantics=("arbitrary",) * n_axes, vmem_limit_bytes=VMEM_LIMIT)


def _rms(x, g):
    return x * lax.rsqrt(jnp.mean(x * x, axis=-1, keepdims=True) + EPS) * g


def _lane_tile(x, reps):
    return jnp.concatenate([x] * reps, axis=1) if reps > 1 else x


def _full(shape):
    n = len(shape)
    return pl.BlockSpec(shape, lambda *_: (0,) * n)


def _dot(a, b):
    return jnp.dot(a, b, preferred_element_type=F32)


def _dot_nt(a, b):
    return lax.dot_general(a, b, (((1,), (1,)), ((), ())), preferred_element_type=F32)


_C_CQ = 0
_C_CKV = 256
_C_KPE = 384
_C_RQ = 512
_C_RK = 768
_C_RV = 1024
_C_RG = 1280
_C_SQ = 1536
_C_SK = 1792
_C_SV = 2048
_W1_COLS = 2304


def _rope(x, cos, sin_lo, sin_hi, half):
    return x * cos + pltpu.roll(x, half, 1) * sin_hi + pltpu.roll(x, LANES - half, 1) * sin_lo


def _kv_slots(c, kp, wke_ref, wv_ref, kcat_ref, v_ref):
    ck = jnp.concatenate([c, kp], axis=1)
    for p in range(MLA_HEADS // 2):
        k2 = _dot(ck, wke_ref[:, 2 * LANES * p:2 * LANES * (p + 1)])
        kcat_ref[0, 2 * p] = k2[:, :LANES].astype(BF16)
        kcat_ref[0, 2 * p + 1] = k2[:, LANES:].astype(BF16)
    for p in range(MLA_HEADS // 4):
        v2 = _dot(c, wv_ref[:, 2 * LANES * p:2 * LANES * (p + 1)])
        v_ref[0, 2 * p] = v2[:, :LANES].astype(BF16)
        v_ref[0, 2 * p + 1] = v2[:, LANES:].astype(BF16)


_IN_PROJ_INPUTS = 17


def _in_proj_body(*refs, n_prev):
    (x_ref, gmix_ref, w1_ref, gq_ref, wq_ref, gkv_ref, wke_ref, wv_ref,
     cslot_ref, slo_slot_ref, shi_slot_ref, ck_ref, slo_k_ref, shi_k_ref, cr_ref, slo_r_ref,
     shi_r_ref) = refs[:_IN_PROJ_INPUTS]
    outs = refs[_IN_PROJ_INPUTS + n_prev:]
    (ckv_ref, kpe_ref, kpeb_ref, qcat_ref, rq_ref, rk_ref, rv_ref, rg_ref,
     sq_ref, sk_ref, sv_ref, skb_ref, svb_ref) = outs[:13]
    kv_refs = outs[13:]
    h = _rms(x_ref[0], gmix_ref[...]).astype(BF16)

    def proj(c0, width):
        return _dot(h, w1_ref[:, c0:c0 + width])

    cq = proj(_C_CQ, 256)
    ckv_raw = proj(_C_CKV, 128)
    kpe_raw = proj(_C_KPE, 128)
    rq = proj(_C_RQ, 256)
    rk = proj(_C_RK, 256)
    rv = proj(_C_RV, 256)
    rg = proj(_C_RG, 256)
    sq = proj(_C_SQ, 256)
    sk = proj(_C_SK, 256)
    sv = proj(_C_SV, 256)

    ckv = _rms(ckv_raw, gkv_ref[...])
    ckv_ref[0, 0] = ckv
    kpe = _rope(kpe_raw, ck_ref[...], slo_k_ref[...], shi_k_ref[...], MLA_ROPE // 2)
    kpe_ref[0, 0] = kpe[:, :MLA_ROPE]
    kpe_b = kpe.astype(BF16)
    kpeb_ref[0] = kpe_b
    cqn = _rms(cq, gq_ref[...]).astype(BF16)
    cslot, slo_slot, shi_slot = cslot_ref[...], slo_slot_ref[...], shi_slot_ref[...]
    for p in range(MLA_HEADS // 2):
        q2 = _dot(cqn, wq_ref[:, 2 * LANES * p:2 * LANES * (p + 1)])
        for a in range(2):
            qs = q2[:, a * LANES:(a + 1) * LANES]
            qcat_ref[0, 2 * p + a] = _rope(qs, cslot, slo_slot, shi_slot, MLA_ROPE // 2).astype(BF16)
    if kv_refs:
        _kv_slots(ckv.astype(BF16), kpe_b, wke_ref, wv_ref, *kv_refs)

    cr, slo_r, shi_r = cr_ref[...], slo_r_ref[...], shi_r_ref[...]
    sk_ref[0, 0] = sk
    sv_ref[0, 0] = sv
    for p in range(2):
        sl = slice(p * LANES, (p + 1) * LANES)
        rq_ref[0, p] = _rope(rq[:, sl], cr, slo_r, shi_r, RET_DK // 2).astype(BF16)
        rk_ref[0, p] = (_rope(rk[:, sl], cr, slo_r, shi_r, RET_DK // 2) * RET_SCALE).astype(BF16)
        rv_ref[0, p] = rv[:, sl].astype(BF16)
        rg_ref[0, p] = rg[:, sl]
        sq_ref[0, p] = (sq[:, sl] * SB_SCALE).astype(BF16)
        skb_ref[0, p] = sk[:, sl].astype(BF16)
        svb_ref[0, p] = sv[:, sl].astype(BF16)


_STACKED_OUTS = (0, 1, 9, 10)


def _in_proj(x, tabs, lw, *, fuse_kv, layer, depth, prev):
    B, T, D = x.shape
    tm = _tile(T, TM_PROJ)
    grid = (B, T // tm)
    row = pl.BlockSpec((tm, LANES), lambda b, t: (t, 0))
    pair = lambda n: pl.BlockSpec((1, n, tm, LANES), lambda b, t: (b, 0, t, 0))
    nat = lambda w: pl.BlockSpec((1, tm, w), lambda b, t: (b, t, 0))
    stacked = lambda w: pl.BlockSpec((1, 1, tm, w), lambda b, t: (layer, b, t, 0))
    in_specs = [nat(D), _full((1, D)), _full((D, _W1_COLS)), _full((1, MLA_Q_LORA)),
                _full((MLA_Q_LORA, MLA_HEADS * LANES)), _full((1, MLA_KV_LORA)),
                _full((MLA_KV_LORA + LANES, MLA_HEADS * LANES)), _full((MLA_KV_LORA, MLA_HEADS * MLA_V))] + [row] * 9
    assert len(in_specs) == _IN_PROJ_INPUTS
    out_shape = [
        jax.ShapeDtypeStruct((depth, B, T, MLA_KV_LORA), F32),
        jax.ShapeDtypeStruct((depth, B, T, MLA_ROPE), F32),
        jax.ShapeDtypeStruct((B, T, LANES), BF16),
        jax.ShapeDtypeStruct((B, MLA_HEADS, T, LANES), BF16),
        jax.ShapeDtypeStruct((B, 2, T, LANES), BF16),
        jax.ShapeDtypeStruct((B, 2, T, LANES), BF16),
        jax.ShapeDtypeStruct((B, 2, T, LANES), BF16),
        jax.ShapeDtypeStruct((B, 2, T, LANES), F32),
        jax.ShapeDtypeStruct((B, 2, T, LANES), BF16),
        jax.ShapeDtypeStruct((depth, B, T, 256), F32),
        jax.ShapeDtypeStruct((depth, B, T, 256), F32),
        jax.ShapeDtypeStruct((B, 2, T, LANES), BF16),
        jax.ShapeDtypeStruct((B, 2, T, LANES), BF16),
    ]
    out_specs = [stacked(MLA_KV_LORA), stacked(MLA_ROPE), nat(LANES), pair(MLA_HEADS), pair(2), pair(2), pair(2),
                 pair(2), pair(2), stacked(256), stacked(256), pair(2), pair(2)]
    if fuse_kv:
        out_shape += [jax.ShapeDtypeStruct((B, MLA_HEADS, T, LANES), BF16),
                      jax.ShapeDtypeStruct((B, MLA_HEADS // 2, T, LANES), BF16)]
        out_specs += [pair(MLA_HEADS), pair(MLA_HEADS // 2)]
    prev = () if prev is None else tuple(prev)
    aliases = {_IN_PROJ_INPUTS + n: o for n, o in enumerate(_STACKED_OUTS[:len(prev)])}
    return pl.pallas_call(
        functools.partial(_in_proj_body, n_prev=len(prev)), grid=grid,
        in_specs=in_specs + [pl.BlockSpec(memory_space=pl.ANY)] * len(prev),
        out_specs=out_specs, out_shape=out_shape, input_output_aliases=aliases,
        compiler_params=_cparams(2), name="in_proj",
    )(x, lw["g_mix"], lw["w1"], lw["g_q"], lw["wq"], lw["g_kv"], lw["wke"], lw["wv"],
      tabs["cslot"], tabs["slo_slot"], tabs["shi_slot"], tabs["ck"], tabs["slo_k"], tabs["shi_k"],
      tabs["cr"], tabs["slo_r"], tabs["shi_r"], *prev)


def _kv_up_body(ckv_ref, kpe_ref, wke_ref, wv_ref, kcat_ref, v_ref):
    _kv_slots(ckv_ref[0].astype(BF16), kpe_ref[0], wke_ref, wv_ref, kcat_ref, v_ref)


def _kv_up(ckv, kpe_pad, lw):
    B, T, _ = ckv.shape
    tm = _tile(T, TM_KVUP)
    nat = lambda w: pl.BlockSpec((1, tm, w), lambda b, t: (b, t, 0))
    pair = lambda n: pl.BlockSpec((1, n, tm, LANES), lambda b, t: (b, 0, t, 0))
    return pl.pallas_call(
        _kv_up_body, grid=(B, T // tm),
        in_specs=[nat(MLA_KV_LORA), nat(LANES), _full((MLA_KV_LORA + LANES, MLA_HEADS * LANES)),
                  _full((MLA_KV_LORA, MLA_HEADS * MLA_V))],
        out_specs=[pair(MLA_HEADS), pair(MLA_HEADS // 2)],
        out_shape=[jax.ShapeDtypeStruct((B, MLA_HEADS, T, LANES), BF16),
                   jax.ShapeDtypeStruct((B, MLA_HEADS // 2, T, LANES), BF16)],
        compiler_params=_cparams(2), name="kv_up",
    )(ckv, kpe_pad, lw["wke"], lw["wv"])


def _mla_last_block(i, *, q0, tq, tk, nk):
    last_q = q0 + i * tq + (tq - 1)
    last_key = (last_q // CHUNK) * CHUNK + (CHUNK - 1)
    return jnp.minimum(last_key // tk, nk - 1)


def _mla_body(q_ref, k_ref, v_ref, o_ref, m_ref, l_ref, acc_ref, *, q0, tq, tk, nk, kv_len):
    i = pl.program_id(1)
    m_ref[...] = jnp.full(m_ref.shape, -jnp.inf, F32)
    l_ref[...] = jnp.zeros(l_ref.shape, F32)
    acc_ref[...] = jnp.zeros(acc_ref.shape, F32)

    def step(j, masked):
        lane = lax.broadcasted_iota(jnp.int32, (tq, LANES), 1)
        first = lane < HALF
        rows = pl.ds(pl.multiple_of(j * tk, tk), tk)
        if masked:
            qpos = q0 + i * tq + lax.broadcasted_iota(jnp.int32, (tq, 1), 0)
            limit = jnp.minimum((jnp.right_shift(qpos, 6) + 1) * CHUNK, kv_len)
            kpos = j * tk + lax.broadcasted_iota(jnp.int32, (tq, tk), 1)
            bias = jnp.where(kpos < limit, 0.0, -jnp.inf).astype(F32)

        def scores(hh):
            s = _dot_nt(q_ref[0, hh], k_ref[0, hh, rows, :])
            return s + bias if masked else s

        s_next = scores(0)
        alphas, pvs = [], []
        for hh in range(MLA_HEADS):
            p, a = divmod(hh, 2)
            s = s_next
            if hh + 1 < MLA_HEADS:
                s_next = scores(hh + 1)
            m_prev = m_ref[hh]
            m_new = jnp.maximum(m_prev, jnp.max(s, axis=1, keepdims=True))
            e = jnp.exp2((s - _lane_tile(m_new, tk // LANES)) * _MLA_EXP2_SCALE)
            alpha = jnp.exp2((m_prev - m_new) * _MLA_EXP2_SCALE)
            l_ref[hh] = alpha * l_ref[hh] + jnp.sum(e, axis=1, keepdims=True)
            m_ref[hh] = m_new
            alphas.append(alpha)
            pvs.append(_dot(e.astype(BF16), v_ref[0, p, rows, :]))
            if a == 1:
                acc_ref[p] = acc_ref[p] * jnp.where(first, alphas[0], alphas[1]) + jnp.where(first, pvs[0], pvs[1])
                alphas, pvs = [], []

    all_visible = jnp.minimum(((q0 + i * tq) // CHUNK + 1) * CHUNK, kv_len)
    last = _mla_last_block(i, q0=q0, tq=tq, tk=tk, nk=nk)
    n_full = jnp.minimum(all_visible // tk, last + 1)

    def full_step(j, carry):
        step(j, False)
        return carry

    def masked_step(j, carry):
        step(j, True)
        return carry

    lax.fori_loop(0, n_full, full_step, 0)
    lax.fori_loop(n_full, last + 1, masked_step, 0)

    lane = lax.broadcasted_iota(jnp.int32, (tq, LANES), 1)
    for p in range(MLA_HEADS // 2):
        inv = jnp.where(lane < HALF, 1.0 / l_ref[2 * p], 1.0 / l_ref[2 * p + 1])
        o_ref[0, p] = (acc_ref[p] * inv).astype(BF16)


def _mla_attention(qcat, kcat, v, *, q0, kv_len, tq, tk):
    B, H, Tq, _ = qcat.shape
    Tk = kcat.shape[2]
    nq, nk = Tq // tq, Tk // tk
    return pl.pallas_call(
        functools.partial(_mla_body, q0=q0, tq=tq, tk=tk, nk=nk, kv_len=kv_len),
        grid=(B, nq),
        in_specs=[pl.BlockSpec((1, H, tq, LANES), lambda b, i: (b, 0, i, 0)),
                  pl.BlockSpec((1, H, Tk, LANES), lambda b, i: (b, 0, 0, 0)),
                  pl.BlockSpec((1, H // 2, Tk, LANES), lambda b, i: (b, 0, 0, 0))],
        out_specs=pl.BlockSpec((1, H // 2, tq, LANES), lambda b, i: (b, 0, i, 0)),
        out_shape=jax.ShapeDtypeStruct((B, H // 2, Tq, LANES), BF16),
        scratch_shapes=[pltpu.VMEM((H, tq, LANES), F32), pltpu.VMEM((H, tq, LANES), F32),
                        pltpu.VMEM((H // 2, tq, LANES), F32)],
        compiler_params=_cparams(2), name="mla_attn",
    )(qcat, kcat, v)


def _sb_last_block(i, *, q0, tq, tk, nk):
    last_key = q0 + i * tq + (tq - 1) - 1
    return jnp.clip(last_key // tk, 0, nk - 1)


def _sb_body(q_ref, k_ref, v_ref, u_ref, o_ref, run_ref, acc_ref, *, q0, tq, tk, nk):
    i = pl.program_id(1)
    q_first = q0 + i * tq
    last = _sb_last_block(i, q0=q0, tq=tq, tk=tk, nk=nk)
    first_unmasked = jnp.minimum(q_first // tk - 1, last)
    run_ref[...] = jnp.zeros(run_ref.shape, F32)
    acc_ref[...] = jnp.zeros(acc_ref.shape, F32)

    def step(kb, masked):
        first_k = lax.broadcasted_iota(jnp.int32, (tk, LANES), 1) < HALF
        tri = u_ref[...]
        reps = tk // LANES
        rows = pl.ds(pl.multiple_of(kb * tk, tk), tk)
        if masked:
            qpos = q_first + lax.broadcasted_iota(jnp.int32, (tq, 1), 0)
            kpos = kb * tk + lax.broadcasted_iota(jnp.int32, (tq, tk), 1)
            before1 = kpos < qpos
            before = jnp.concatenate([before1, before1], axis=1)

        def scores(p):
            kp = k_ref[0, p, rows, :]
            zero = jnp.zeros_like(kp)
            k2 = jnp.concatenate([jnp.where(first_k, kp, zero), jnp.where(first_k, zero, kp)], axis=0)
            return _dot_nt(q_ref[0, p], k2) * _LOG2E

        npair = SB_HEADS // 2
        zs = [scores(p) for p in range(npair)]
        keeps, betas, betweens = [], [], []
        for p in range(npair):
            z = zs[p]
            nz = -z
            sp = jnp.log(1.0 + jnp.exp2(jnp.minimum(z, nz))) * _LOG2E
            log_keep = jnp.minimum(nz, 0.0) - sp
            betas.append(log_keep + z)
            if masked:
                log_keep = jnp.where(before, log_keep, 0.0)
            keeps.append(log_keep)
            hi = log_keep.astype(BF16)
            lo = (log_keep - hi.astype(F32)).astype(BF16)
            betweens.append(jnp.concatenate(
                [_dot(hi[:, :tk], tri) + _dot(lo[:, :tk], tri), _dot(hi[:, tk:], tri) + _dot(lo[:, tk:], tri)], axis=1))
        for p in range(npair):
            vp = v_ref[0, p, rows, :]
            zero = jnp.zeros_like(vp)
            v2 = jnp.concatenate([jnp.where(first_k, vp, zero), jnp.where(first_k, zero, vp)], axis=0)
            run_a = run_ref[2 * p]
            run_b = run_ref[2 * p + 1]
            run = jnp.concatenate([_lane_tile(run_a, reps), _lane_tile(run_b, reps)], axis=1)
            att = jnp.exp2(betas[p] + betweens[p] + run)
            if masked:
                att = jnp.where(before, att, 0.0)
            run_ref[2 * p] = run_a + jnp.sum(keeps[p][:, :tk], axis=1, keepdims=True)
            run_ref[2 * p + 1] = run_b + jnp.sum(keeps[p][:, tk:], axis=1, keepdims=True)
            acc_ref[p] = acc_ref[p] + _dot(att.astype(BF16), v2)

    def masked_step(t, carry):
        step(last - t, True)
        return carry

    lax.fori_loop(0, last - first_unmasked, masked_step, 0)

    def more(carry):
        kb, live = carry
        return jnp.logical_and(kb >= 0, live > 0)

    def unmasked_step(carry):
        kb, _ = carry
        step(kb, False)
        live = (jnp.max(run_ref[...]) >= _SB_UNDERFLOW_LOG2).astype(jnp.int32)
        return kb - 1, live

    lax.while_loop(more, unmasked_step, (first_unmasked, jnp.int32(1)))
    o_ref[0] = acc_ref[...].astype(BF16)


def _sb_attention(q, k, v, *, q0, tq, tk):
    B, P, Tq, _ = q.shape
    Tk = k.shape[2]
    nq, nk = Tq // tq, Tk // tk
    kw = dict(q0=q0, tq=tq, tk=tk, nk=nk)
    r = lax.broadcasted_iota(jnp.int32, (tk, tk), 0)
    c = lax.broadcasted_iota(jnp.int32, (tk, tk), 1)
    tri = (r > c).astype(BF16)

    kv_spec = pl.BlockSpec((1, P, Tk, LANES), lambda b, i: (b, 0, 0, 0))
    return pl.pallas_call(
        functools.partial(_sb_body, **kw),
        grid=(B, nq),
        in_specs=[pl.BlockSpec((1, P, tq, LANES), lambda b, i: (b, 0, i, 0)), kv_spec, kv_spec, _full((tk, tk))],
        out_specs=pl.BlockSpec((1, P, tq, LANES), lambda b, i: (b, 0, i, 0)),
        out_shape=jax.ShapeDtypeStruct((B, P, Tq, LANES), BF16),
        scratch_shapes=[pltpu.VMEM((2 * P, tq, LANES), F32), pltpu.VMEM((P, tq, LANES), F32)],
        compiler_params=_cparams(2), name="sb_attn",
    )(q, k, v, tri)


def _ret_body(q_ref, k_ref, v_ref, g_ref, s0_ref, dmask_ref, dq_ref, dk_ref, cd_ref, bd_ref, gn_ref,
              y_ref, snew_ref, state_ref, *, tc, lc, nc):
    c = pl.program_id(1)

    @pl.when(c == 0)
    def _():
        state_ref[...] = s0_ref[0]

    lane = lax.broadcasted_iota(jnp.int32, (lc, LANES), 1)
    first = lane < HALF
    for p in range(RET_HEADS // 2):
        for ch in range(tc // lc):
            rows = slice(ch * lc, (ch + 1) * lc)
            q = q_ref[0, p, rows, :]
            k = k_ref[0, p, rows, :]
            v = v_ref[0, p, rows, :]
            state = state_ref[p]
            cross = _dot((q.astype(F32) * dq_ref[p]).astype(BF16), state.astype(BF16))
            zero = jnp.zeros_like(q)
            inner = []
            for a in range(2):
                qa = jnp.where(first, q, zero) if a == 0 else jnp.where(first, zero, q)
                scores = _dot_nt(qa, k) * dmask_ref[2 * p + a]
                inner.append(_dot(scores.astype(BF16), v))
            o = jnp.where(first, inner[0], inner[1]) + cross
            kd = (k.astype(F32) * dk_ref[p]).astype(BF16)
            kv = lax.dot_general(kd, v, (((0,), (0,)), ((), ())), preferred_element_type=F32)
            state_ref[p] = cd_ref[p] * state + bd_ref[...] * kv
            zf = jnp.zeros_like(o)
            mu = jnp.where(first, jnp.sum(jnp.where(first, o, zf), axis=1, keepdims=True),
                           jnp.sum(jnp.where(first, zf, o), axis=1, keepdims=True)) * (1.0 / RET_DV)
            d = o - mu
            d2 = d * d
            var = jnp.where(first, jnp.sum(jnp.where(first, d2, zf), axis=1, keepdims=True),
                            jnp.sum(jnp.where(first, zf, d2), axis=1, keepdims=True)) * (1.0 / RET_DV)
            yn = d * lax.rsqrt(var + EPS) * gn_ref[p]
            gate = g_ref[0, p, rows, :]
            y_ref[0, p, rows, :] = (yn * (gate * jax.nn.sigmoid(gate))).astype(BF16)

    @pl.when(c == nc - 1)
    def _():
        snew_ref[0] = state_ref[...]


def _ret_tables(lc):
    log_gamma = jnp.log(1.0 - 2.0 ** (-5.0 - jnp.arange(RET_HEADS, dtype=F32)))
    idx = jnp.arange(lc, dtype=F32)
    diff = idx[:, None] - idx[None, :]
    dmask = jnp.where(diff[None] >= 0, jnp.exp(jnp.maximum(diff, 0.0)[None] * log_gamma[:, None, None]), 0.0)
    dk = jnp.exp((lc - 1.0 - idx)[:, None] * log_gamma[None, :])
    dq = jnp.exp((idx + 1.0)[:, None] * log_gamma[None, :])
    cd = jnp.exp(lc * log_gamma)

    def lanes(t):
        t = jnp.repeat(t[:, :, None], HALF, axis=2).reshape(t.shape[0], RET_HEADS // 2, LANES)
        return jnp.transpose(t, (1, 0, 2))

    blk = jnp.arange(LANES) // HALF
    bd = (blk[:, None] == blk[None, :]).astype(F32)
    cd_rows = jnp.repeat(cd, HALF).reshape(RET_HEADS // 2, LANES, 1)
    return dmask, lanes(dq), lanes(dk), cd_rows * bd[None], bd


def _state_to_pairs(s):
    B = s.shape[0]
    s = s.astype(F32).reshape(B, RET_HEADS // 2, 2, RET_DK, RET_DV)
    z = jnp.zeros_like(s[:, :, 0])
    top = jnp.concatenate([s[:, :, 0], z], axis=-1)
    bot = jnp.concatenate([z, s[:, :, 1]], axis=-1)
    return jnp.concatenate([top, bot], axis=-2)


def _pairs_to_state(sp):
    a = sp[:, :, :HALF, :HALF]
    b = sp[:, :, HALF:, HALF:]
    B = sp.shape[0]
    return jnp.stack([a, b], axis=2).reshape(B, RET_HEADS, RET_DK, RET_DV)


def _retention(rq, rk, rv, rg, s0_pairs, g_norm_pairs, *, lc):
    B, P, T, _ = rq.shape
    tc = _tile(T, TC_RET)
    nc = T // tc
    dmask, dq, dk, cd, bd = _ret_tables(lc)
    pair = pl.BlockSpec((1, P, tc, LANES), lambda b, c: (b, 0, c, 0))
    st = pl.BlockSpec((1, P, LANES, LANES), lambda b, c: (b, 0, 0, 0))
    return pl.pallas_call(
        functools.partial(_ret_body, tc=tc, lc=lc, nc=nc),
        grid=(B, nc),
        in_specs=[pair, pair, pair, pair, st, _full(dmask.shape), _full(dq.shape), _full(dk.shape),
                  _full(cd.shape), _full(bd.shape), _full(g_norm_pairs.shape)],
        out_specs=[pair, st],
        out_shape=[jax.ShapeDtypeStruct((B, P, T, LANES), BF16),
                   jax.ShapeDtypeStruct((B, P, LANES, LANES), F32)],
        scratch_shapes=[pltpu.VMEM((P, LANES, LANES), F32)],
        compiler_params=_cparams(2), name="retention",
    )(rq, rk, rv, rg, s0_pairs, dmask, dq, dk, cd, bd, g_norm_pairs)


def _out_mem_body(x_ref, omla_ref, ret_ref, osb_ref, wout_ref, gq_ref, wq_ref, mk_ref, mv_ref, wo_ref, y_ref):
    mixed = jnp.concatenate([ref[0, p] for ref, n in ((omla_ref, MLA_HEADS // 2), (ret_ref, RET_HEADS // 2),
                                                      (osb_ref, SB_HEADS // 2)) for p in range(n)], axis=1)
    x1 = x_ref[0] + _dot(mixed, wout_ref[...])
    hq = _rms(x1, gq_ref[...]).astype(BF16)
    q = _dot(hq, wq_ref[...]).astype(BF16)
    cols = [slice(hh * MEM_HD, (hh + 1) * MEM_HD) for hh in range(MEM_HEADS)]
    scores = [_dot_nt(q[:, sl], mk_ref[0, :, sl]) * MEM_SCALE for sl in cols]
    probs = []
    for s in scores:
        e = jnp.exp(s - jnp.max(s, axis=1, keepdims=True))
        probs.append((e / jnp.sum(e, axis=1, keepdims=True)).astype(BF16))
    heads = [_dot(pr, mv_ref[0, :, sl]).astype(BF16) for pr, sl in zip(probs, cols)]
    o = jnp.concatenate(heads, axis=1)
    y_ref[0] = x1 + _dot(o, wo_ref[...])


def _out_mem(x, o_mla, ret_y, o_sb, mem_k, mem_v, lw):
    B, T, D = x.shape
    tm = _tile(T, TM_OUT)
    M = mem_k.shape[1]
    mw = MEM_HEADS * MEM_HD
    nat = pl.BlockSpec((1, tm, D), lambda b, t: (b, t, 0))
    pair = lambda n: pl.BlockSpec((1, n, tm, LANES), lambda b, t: (b, 0, t, 0))
    mem = pl.BlockSpec((1, M, mw), lambda b, t: (b, 0, 0))
    return pl.pallas_call(
        _out_mem_body, grid=(B, T // tm),
        in_specs=[nat, pair(MLA_HEADS // 2), pair(RET_HEADS // 2), pair(SB_HEADS // 2), _full((D, D)),
                  _full((1, D)), _full((D, mw)), mem, mem, _full((mw, D))],
        out_specs=nat, out_shape=jax.ShapeDtypeStruct((B, T, D), F32),
        compiler_params=_cparams(2), name="out_mem",
    )(x, o_mla, ret_y, o_sb, lw["w_out"], lw["g_mem_q"], lw["w_mem_q"], mem_k, mem_v, lw["w_mem_o"])


def _mem_kv_body(mem_ref, g_ref, wk_ref, wv_ref, k_ref, v_ref, kb_ref, vb_ref):
    m = _rms(mem_ref[0], g_ref[...]).astype(BF16)
    k = _dot(m, wk_ref[...])
    v = _dot(m, wv_ref[...])
    k_ref[0] = k
    v_ref[0] = v
    kb_ref[0] = k.astype(BF16)
    vb_ref[0] = v.astype(BF16)


def _mem_kv(mem, lw):
    B, M, D = mem.shape
    mw = MEM_HEADS * MEM_HD
    o = pl.BlockSpec((1, M, mw), lambda b: (b, 0, 0))
    return pl.pallas_call(
        _mem_kv_body, grid=(B,),
        in_specs=[pl.BlockSpec((1, M, D), lambda b: (b, 0, 0)), _full((1, D)), _full((D, mw)), _full((D, mw))],
        out_specs=[o, o, o, o],
        out_shape=[jax.ShapeDtypeStruct((B, M, mw), F32), jax.ShapeDtypeStruct((B, M, mw), F32),
                   jax.ShapeDtypeStruct((B, M, mw), BF16), jax.ShapeDtypeStruct((B, M, mw), BF16)],
        compiler_params=_cparams(1), name="mem_kv",
    )(mem, lw["g_mem_kv"], lw["w_mem_k"], lw["w_mem_v"])


_CARRY_ROW = 8 - (CONV_W - 1)


def _ffn_body(x_ref, g_ref, wup_ref, wconv_ref, bconv_ref, wdown_ref, conv0_ref, gfin_ref,
              y_ref, convnew_ref, carry_ref, exta_ref, extb_ref, act_ref, *, tm, nt, final):
    t = pl.program_id(1)

    @pl.when(t == 0)
    def _():
        carry_ref[_CARRY_ROW:8, :] = conv0_ref[0]

    x = x_ref[0]
    h = _rms(x, g_ref[...]).astype(BF16)

    def conv_part(c0, ext_ref):
        cols = slice(c0, c0 + TF_FFN)
        u = _dot(h, wup_ref[:, cols])
        ext_ref[_CARRY_ROW:8, :] = carry_ref[_CARRY_ROW:8, cols]
        ext_ref[8:8 + tm, :] = u
        carry_ref[_CARRY_ROW:8, cols] = u[tm - (CONV_W - 1):, :]
        w = wconv_ref[:, cols]
        c = bconv_ref[:, cols]
        c = c + ext_ref[_CARRY_ROW:_CARRY_ROW + tm, :] * w[0:1]
        c = c + ext_ref[_CARRY_ROW + 1:_CARRY_ROW + 1 + tm, :] * w[1:2]
        return c + u * w[2:3]

    for ci in range(D_FF // TF_FFN):
        c0 = ci * TF_FFN
        a = conv_part(c0, exta_ref)
        b = conv_part(D_FF + c0, extb_ref)
        act_ref[:, c0:c0 + TF_FFN] = (a * jax.nn.sigmoid(a) * b).astype(BF16)

    x3 = x + _dot(act_ref[...], wdown_ref[...])
    y_ref[0] = _rms(x3, gfin_ref[...]) if final else x3

    @pl.when(t == nt - 1)
    def _():
        convnew_ref[0] = carry_ref[_CARRY_ROW:8, :]


def _conv_ffn(x, conv0, lw, g_final, *, final):
    B, T, D = x.shape
    tm = _tile(T, TM_FFN)
    nt = T // tm
    f2 = 2 * D_FF
    nat = pl.BlockSpec((1, tm, D), lambda b, t: (b, t, 0))
    cv = pl.BlockSpec((1, CONV_W - 1, f2), lambda b, t: (b, 0, 0))
    return pl.pallas_call(
        functools.partial(_ffn_body, tm=tm, nt=nt, final=final),
        grid=(B, nt),
        in_specs=[nat, _full((1, D)), _full((D, f2)), _full((CONV_W, f2)), _full((1, f2)), _full((D_FF, D)),
                  cv, _full((1, D))],
        out_specs=[nat, cv],
        out_shape=[jax.ShapeDtypeStruct((B, T, D), F32), jax.ShapeDtypeStruct((B, CONV_W - 1, f2), F32)],
        scratch_shapes=[pltpu.VMEM((8, f2), F32), pltpu.VMEM((8 + tm, TF_FFN), F32),
                        pltpu.VMEM((8 + tm, TF_FFN), F32), pltpu.VMEM((tm, D_FF), BF16)],
        compiler_params=_cparams(2), name="conv_ffn",
    )(x, lw["g_ffn"], lw["w_ffn_up"], lw["w_ffn_conv"], lw["b_ffn_conv"], lw["w_ffn_down"], conv0, g_final)


def _pad_cols(w, width):
    return jnp.pad(w, ((0, 0), (0, width - w.shape[1])))


def _prep_layer(l, g_mix, w_in, g_q_lora, w_q_up, g_kv_lora, w_kv_up, g_ret_norm, w_out,
                g_mem_q, g_mem_kv, w_mem_q, w_mem_k, w_mem_v, w_mem_o,
                g_ffn, w_ffn_up, w_ffn_conv, b_ffn_conv, w_ffn_down):
    w = w_in[l]
    parts, c0 = [], 0
    for n in IN_SIZES:
        parts.append(w[:, c0:c0 + n])
        c0 += n
    cq, ckv, kpe, rq, rk, rv, rg, sq, sk, sv = parts
    w1 = jnp.concatenate([cq, ckv, _pad_cols(kpe, LANES), rq, rk, rv, rg, sq, sk, sv], axis=1).astype(BF16)
    kq = w_q_up.shape[1]
    wq3 = w_q_up[l].reshape(kq, MLA_HEADS, MLA_NOPE + MLA_ROPE)
    tail = LANES - MLA_NOPE - MLA_ROPE
    q_pad = jnp.pad(wq3, ((0, 0), (0, 0), (0, tail))).reshape(kq, MLA_HEADS * LANES)
    wkv3 = w_kv_up[l].reshape(MLA_KV_LORA, MLA_HEADS, MLA_NOPE + MLA_V)
    wk = jnp.pad(wkv3[:, :, :MLA_NOPE], ((0, 0), (0, 0), (0, LANES - MLA_NOPE))).reshape(MLA_KV_LORA, MLA_HEADS * LANES)
    wv = wkv3[:, :, MLA_NOPE:].reshape(MLA_KV_LORA, MLA_HEADS * MLA_V)
    r = jnp.arange(LANES)[:, None]
    c = jnp.arange(MLA_HEADS * LANES)[None, :]
    we = ((r < MLA_ROPE) & (c % LANES == r + MLA_NOPE)).astype(BF16)
    row = lambda g: g[l].reshape(1, -1).astype(F32)
    return dict(
        g_mix=row(g_mix), w1=w1, g_q=row(g_q_lora), wq=q_pad.astype(BF16),
        g_kv=row(g_kv_lora), wke=jnp.concatenate([wk.astype(BF16), we], axis=0), wv=wv.astype(BF16),
        g_ret=g_ret_norm[l].astype(F32).reshape(RET_HEADS // 2, 1, LANES),
        w_out=w_out[l].astype(BF16), g_mem_q=row(g_mem_q), g_mem_kv=row(g_mem_kv),
        w_mem_q=w_mem_q[l].astype(BF16), w_mem_k=w_mem_k[l].astype(BF16), w_mem_v=w_mem_v[l].astype(BF16),
        w_mem_o=w_mem_o[l].astype(BF16), g_ffn=row(g_ffn), w_ffn_up=w_ffn_up[l].astype(BF16),
        w_ffn_conv=w_ffn_conv[l].astype(F32), b_ffn_conv=row(b_ffn_conv), w_ffn_down=w_ffn_down[l].astype(BF16))


def _rope_tables(pos):
    t = pos.shape[0]
    posf = pos.astype(F32)[:, None]

    def cos_sin(d):
        inv = ROPE_BASE ** (-jnp.arange(0, d, 2, dtype=F32) / d)
        ang = posf * inv[None, :]
        return jnp.cos(ang), jnp.sin(ang)

    c16, s16 = cos_sin(MLA_ROPE)
    c32, s32 = cos_sin(RET_DK)
    z = lambda n: jnp.zeros((t, n), F32)
    tail = LANES - MLA_NOPE - MLA_ROPE
    return dict(
        cslot=jnp.concatenate([jnp.ones((t, MLA_NOPE), F32), c16, c16, z(tail)], axis=1),
        slo_slot=jnp.concatenate([z(MLA_NOPE), -s16, z(MLA_ROPE // 2 + tail)], axis=1),
        shi_slot=jnp.concatenate([z(MLA_NOPE + MLA_ROPE // 2), s16, z(tail)], axis=1),
        ck=jnp.concatenate([c16, c16, z(LANES - MLA_ROPE)], axis=1),
        slo_k=jnp.concatenate([-s16, z(LANES - MLA_ROPE // 2)], axis=1),
        shi_k=jnp.concatenate([z(MLA_ROPE // 2), s16, z(LANES - MLA_ROPE)], axis=1),
        cr=jnp.tile(jnp.concatenate([c32, c32], axis=1), (1, LANES // RET_DK)),
        slo_r=jnp.tile(jnp.concatenate([-s32, z(RET_DK // 2)], axis=1), (1, LANES // RET_DK)),
        shi_r=jnp.tile(jnp.concatenate([z(RET_DK // 2), s32], axis=1), (1, LANES // RET_DK)))


def _pad_rows(a, axis, n):
    pad = [(0, 0)] * a.ndim
    pad[axis] = (0, n - a.shape[axis])
    return jnp.pad(a, pad)


def _to_pairs(a):
    B, T, _ = a.shape
    return jnp.transpose(a.reshape(B, T, 2, LANES), (0, 2, 1, 3))


def _layer(x, tabs, lw, mem_k, mem_v, past, g_final, *, q0, layer, depth, prev):
    B, T, _ = x.shape
    final = layer == depth - 1
    outs = _in_proj(x, tabs, lw, fuse_kv=past is None, layer=layer, depth=depth, prev=prev)
    (ckv_st, kpe_st, kpe_b, qcat, rq, rk, rv, rg, sq, sk_st, sv_st, sk_b, sv_b) = outs[:13]
    if past is None:
        kcat, vmla = outs[13:]
        sk_all, sv_all = sk_b, sv_b
        s0 = jnp.zeros((B, RET_HEADS // 2, LANES, LANES), F32)
        conv0 = jnp.zeros((B, CONV_W - 1, 2 * D_FF), F32)
        kv_len = T
        lc = _tile(T, L_RET)
        tq_mla, tk_mla = _tile(T, TQ_MLA), _tile(T, TK_MLA)
        tq_sb, tk_sb = _tile(T, TQ_SB), _tile(T, TK_SB)
    else:
        ckv_c, kpe_c, s0, sk_c, sv_c, conv0 = past
        P = ckv_c.shape[1]
        kv_len = P + T
        tk_sb = TK_SB
        tk_pad = -(-kv_len // tk_sb) * tk_sb
        ckv_all = _pad_rows(jnp.concatenate([ckv_c.astype(F32), ckv_st[layer]], axis=1), 1, tk_pad)
        kpe_cb = _pad_cols(kpe_c.reshape(B * P, MLA_ROPE), LANES).reshape(B, P, LANES).astype(BF16)
        kpe_all = _pad_rows(jnp.concatenate([kpe_cb, kpe_b], axis=1), 1, tk_pad)
        sk_all = _pad_rows(jnp.concatenate([_to_pairs(sk_c.reshape(B, P, -1).astype(BF16)), sk_b], axis=2), 2, tk_pad)
        sv_all = _pad_rows(jnp.concatenate([_to_pairs(sv_c.reshape(B, P, -1).astype(BF16)), sv_b], axis=2), 2, tk_pad)
        s0 = _state_to_pairs(s0)
        conv0 = conv0.astype(F32)
        lc = T
        tq_mla, tk_mla = T, tk_pad
        tq_sb = T
        kcat, vmla = _kv_up(ckv_all, kpe_all, lw)
    o_mla = _mla_attention(qcat, kcat, vmla, q0=q0, kv_len=kv_len, tq=tq_mla, tk=tk_mla)
    o_sb = _sb_attention(sq, sk_all, sv_all, q0=q0, tq=tq_sb, tk=tk_sb)
    ret_y, s_new = _retention(rq, rk, rv, rg, s0, lw["g_ret"], lc=lc)
    x = _out_mem(x, o_mla, ret_y, o_sb, mem_k, mem_v, lw)
    x, conv_new = _conv_ffn(x, conv0, lw, g_final, final=final)
    return x, (ckv_st, kpe_st, sk_st, sv_st), (_pairs_to_state(s_new), conv_new)


def kernel(x_prompt, x_sample, cache_mla_ckv, cache_mla_kpe, state_ret, cache_sb_k, cache_sb_v, cache_mem_k, cache_mem_v, state_ffn_conv, mem_prompt, g_mix, w_in, g_q_lora, w_q_up, g_kv_lora, w_kv_up, g_ret_norm, w_out, g_mem_q, g_mem_kv, w_mem_q, w_mem_k, w_mem_v, w_mem_o, g_ffn, w_ffn_up, w_ffn_conv, b_ffn_conv, w_ffn_down, g_final):
    depth = w_in.shape[0]
    layers = [_prep_layer(l, g_mix, w_in, g_q_lora, w_q_up, g_kv_lora, w_kv_up, g_ret_norm, w_out,
                          g_mem_q, g_mem_kv, w_mem_q, w_mem_k, w_mem_v, w_mem_o,
                          g_ffn, w_ffn_up, w_ffn_conv, b_ffn_conv, w_ffn_down) for l in range(depth)]
    gfin = g_final.reshape(1, -1).astype(F32)
    mw = MEM_HEADS * MEM_HD

    xp = x_prompt
    Bp, Tp, _ = xp.shape
    tabs_p = _rope_tables(jnp.arange(Tp))
    p_states, p_mem_k, p_mem_v = [], [], []
    p_caches = None
    for l in range(depth):
        mk, mv, mk_b, mv_b = _mem_kv(mem_prompt, layers[l])
        xp, p_caches, st = _layer(xp, tabs_p, layers[l], mk_b, mv_b, None, gfin, q0=0, layer=l, depth=depth,
                                  prev=p_caches)
        p_states.append(st)
        p_mem_k.append(mk.reshape(Bp, -1, MEM_HEADS, MEM_HD))
        p_mem_v.append(mv.reshape(Bp, -1, MEM_HEADS, MEM_HD))

    xs = x_sample
    Bs, Ts, _ = xs.shape
    past_len = cache_mla_ckv.shape[2]
    tabs_s = _rope_tables(past_len + jnp.arange(Ts))
    s_states = []
    s_caches = None
    for l in range(depth):
        past = (cache_mla_ckv[l], cache_mla_kpe[l], state_ret[l], cache_sb_k[l], cache_sb_v[l], state_ffn_conv[l])
        mk_b = cache_mem_k[l].reshape(Bs, -1, mw).astype(BF16)
        mv_b = cache_mem_v[l].reshape(Bs, -1, mw).astype(BF16)
        xs, s_caches, st = _layer(xs, tabs_s, layers[l], mk_b, mv_b, past, gfin, q0=past_len, layer=l, depth=depth,
                                  prev=s_caches)
        s_states.append(st)

    def leaves(caches, states, b, t):
        ckv, kpe, sk, sv = caches
        ret, conv = [jnp.stack(f) for f in zip(*states)]
        heads = (depth, b, t, SB_HEADS, SB_HD)
        return ckv, kpe, ret, sk.reshape(heads), sv.reshape(heads), conv

    p_ckv, p_kpe, p_ret, p_sbk, p_sbv, p_conv = leaves(p_caches, p_states, Bp, Tp)
    s_ckv, s_kpe, s_ret, s_sbk, s_sbv, s_conv = leaves(s_caches, s_states, Bs, Ts)
    return (xp, xs, p_ckv, p_kpe, p_ret, p_sbk, p_sbv, jnp.stack(p_mem_k), jnp.stack(p_mem_v), p_conv,
            s_ckv, s_kpe, s_ret, s_sbk, s_sbv, s_conv)
```

```python
import functools

import jax
import jax.numpy as jnp
from jax import lax
from jax.experimental import pallas as pl
from jax.experimental.pallas import tpu as pltpu

F32 = jnp.float32
BF16 = jnp.bfloat16

CHUNK = 64
EPS = 1e-6
ROPE_BASE = 10000.0

MLA_HEADS = 8
MLA_NOPE = 64
MLA_ROPE = 32
MLA_V = 64
MLA_Q_LORA = 256
MLA_KV_LORA = 128
MLA_SCALE = (MLA_NOPE + MLA_ROPE) ** -0.5
_LOG2E = 1.4426950408889634
_MLA_EXP2_SCALE = MLA_SCALE * _LOG2E
_SB_UNDERFLOW_LOG2 = -160.0
RET_HEADS = 4
RET_DK = 64
RET_DV = 64
RET_SCALE = RET_DK ** -0.5
SB_HEADS = 4
SB_HD = 64
SB_SCALE = SB_HD ** -0.5
MEM_HEADS = 4
MEM_HD = 128
MEM_SCALE = MEM_HD ** -0.5
D_FF = 2816
CONV_W = 3

IN_SIZES = (MLA_Q_LORA, MLA_KV_LORA, MLA_ROPE, RET_HEADS * RET_DK, RET_HEADS * RET_DK,
            RET_HEADS * RET_DV, RET_HEADS * RET_DV, SB_HEADS * SB_HD, SB_HEADS * SB_HD, SB_HEADS * SB_HD)

LANES = 128
HALF = LANES // 2
VMEM_LIMIT = 56 * 1024 * 1024

TM_PROJ = 1024
TM_KVUP = 2304
TQ_MLA = 512
TK_MLA = 512
TQ_SB = 256
TK_SB = 256
TC_RET = 512
L_RET = 256
TM_OUT = 1024
TM_FFN = 512
TF_FFN = 256


def _tile(n, target):
    t = min(n, target)
    while n % t:
        t -= 1
    return t


def _cparams(n_axes):
    return pltpu.CompilerParams(dimension_semantics=("arbitrary",) * n_axes, vmem_limit_bytes=VMEM_LIMIT)


def _rms(x, g):
    return x * lax.rsqrt(jnp.mean(x * x, axis=-1, keepdims=True) + EPS) * g


def _lane_tile(x, reps):
    return jnp.concatenate([x] * reps, axis=1) if reps > 1 else x


def _full(shape):
    n = len(shape)
    return pl.BlockSpec(shape, lambda *_: (0,) * n)


def _dot(a, b):
    return jnp.dot(a, b, preferred_element_type=F32)


def _dot_nt(a, b):
    return lax.dot_general(a, b, (((1,), (1,)), ((), ())), preferred_element_type=F32)


_C_CQ = 0
_C_CKV = 256
_C_KPE = 384
_C_RQ = 512
_C_RK = 768
_C_RV = 1024
_C_RG = 1280
_C_SQ = 1536
_C_SK = 1792
_C_SV = 2048
_W1_COLS = 2304


def _rope(x, cos, sin_lo, sin_hi, half):
    return x * cos + pltpu.roll(x, half, 1) * sin_hi + pltpu.roll(x, LANES - half, 1) * sin_lo


def _kv_slots(c, kp, wke_ref, wv_ref, kcat_ref, v_ref):
    ck = jnp.concatenate([c, kp], axis=1)
    for p in range(MLA_HEADS // 2):
        k2 = _dot(ck, wke_ref[:, 2 * LANES * p:2 * LANES * (p + 1)])
        kcat_ref[0, 2 * p] = k2[:, :LANES].astype(BF16)
        kcat_ref[0, 2 * p + 1] = k2[:, LANES:].astype(BF16)
    for p in range(MLA_HEADS // 4):
        v2 = _dot(c, wv_ref[:, 2 * LANES * p:2 * LANES * (p + 1)])
        v_ref[0, 2 * p] = v2[:, :LANES].astype(BF16)
        v_ref[0, 2 * p + 1] = v2[:, LANES:].astype(BF16)


_IN_PROJ_INPUTS = 17


def _in_proj_body(*refs, n_prev):
    (x_ref, gmix_ref, w1_ref, gq_ref, wq_ref, gkv_ref, wke_ref, wv_ref,
     cslot_ref, slo_slot_ref, shi_slot_ref, ck_ref, slo_k_ref, shi_k_ref, cr_ref, slo_r_ref,
     shi_r_ref) = refs[:_IN_PROJ_INPUTS]
    outs = refs[_IN_PROJ_INPUTS + n_prev:]
    (ckv_ref, kpe_ref, kpeb_ref, qcat_ref, rq_ref, rk_ref, rv_ref, rg_ref,
     sq_ref, sk_ref, sv_ref, skb_ref, svb_ref) = outs[:13]
    kv_refs = outs[13:]
    h = _rms(x_ref[0], gmix_ref[...]).astype(BF16)

    def proj(c0, width):
        return _dot(h, w1_ref[:, c0:c0 + width])

    cq = proj(_C_CQ, 256)
    ckv_raw = proj(_C_CKV, 128)
    kpe_raw = proj(_C_KPE, 128)
    rq = proj(_C_RQ, 256)
    rk = proj(_C_RK, 256)
    rv = proj(_C_RV, 256)
    rg = proj(_C_RG, 256)
    sq = proj(_C_SQ, 256)
    sk = proj(_C_SK, 256)
    sv = proj(_C_SV, 256)

    ckv = _rms(ckv_raw, gkv_ref[...])
    ckv_ref[0, 0] = ckv
    kpe = _rope(kpe_raw, ck_ref[...], slo_k_ref[...], shi_k_ref[...], MLA_ROPE // 2)
    kpe_ref[0, 0] = kpe[:, :MLA_ROPE]
    kpe_b = kpe.astype(BF16)
    kpeb_ref[0] = kpe_b
    cqn = _rms(cq, gq_ref[...]).astype(BF16)
    cslot, slo_slot, shi_slot = cslot_ref[...], slo_slot_ref[...], shi_slot_ref[...]
    for p in range(MLA_HEADS // 2):
        q2 = _dot(cqn, wq_ref[:, 2 * LANES * p:2 * LANES * (p + 1)])
        for a in range(2):
            qs = q2[:, a * LANES:(a + 1) * LANES]
            qcat_ref[0, 2 * p + a] = _rope(qs, cslot, slo_slot, shi_slot, MLA_ROPE // 2).astype(BF16)
    if kv_refs:
        _kv_slots(ckv.astype(BF16), kpe_b, wke_ref, wv_ref, *kv_refs)

    cr, slo_r, shi_r = cr_ref[...], slo_r_ref[...], shi_r_ref[...]
    sk_ref[0, 0] = sk
    sv_ref[0, 0] = sv
    for p in range(2):
        sl = slice(p * LANES, (p + 1) * LANES)
        rq_ref[0, p] = _rope(rq[:, sl], cr, slo_r, shi_r, RET_DK // 2).astype(BF16)
        rk_ref[0, p] = (_rope(rk[:, sl], cr, slo_r, shi_r, RET_DK // 2) * RET_SCALE).astype(BF16)
        rv_ref[0, p] = rv[:, sl].astype(BF16)
        rg_ref[0, p] = rg[:, sl]
        sq_ref[0, p] = (sq[:, sl] * SB_SCALE).astype(BF16)
        skb_ref[0, p] = sk[:, sl].astype(BF16)
        svb_ref[0, p] = sv[:, sl].astype(BF16)


_STACKED_OUTS = (0, 1, 9, 10)


def _in_proj(x, tabs, lw, *, fuse_kv, layer, depth, prev):
    B, T, D = x.shape
    tm = _tile(T, TM_PROJ)
    grid = (B, T // tm)
    row = pl.BlockSpec((tm, LANES), lambda b, t: (t, 0))
    pair = lambda n: pl.BlockSpec((1, n, tm, LANES), lambda b, t: (b, 0, t, 0))
    nat = lambda w: pl.BlockSpec((1, tm, w), lambda b, t: (b, t, 0))
    stacked = lambda w: pl.BlockSpec((1, 1, tm, w), lambda b, t: (layer, b, t, 0))
    in_specs = [nat(D), _full((1, D)), _full((D, _W1_COLS)), _full((1, MLA_Q_LORA)),
                _full((MLA_Q_LORA, MLA_HEADS * LANES)), _full((1, MLA_KV_LORA)),
                _full((MLA_KV_LORA + LANES, MLA_HEADS * LANES)), _full((MLA_KV_LORA, MLA_HEADS * MLA_V))] + [row] * 9
    assert len(in_specs) == _IN_PROJ_INPUTS
    out_shape = [
        jax.ShapeDtypeStruct((depth, B, T, MLA_KV_LORA), F32),
        jax.ShapeDtypeStruct((depth, B, T, MLA_ROPE), F32),
        jax.ShapeDtypeStruct((B, T, LANES), BF16),
        jax.ShapeDtypeStruct((B, MLA_HEADS, T, LANES), BF16),
        jax.ShapeDtypeStruct((B, 2, T, LANES), BF16),
        jax.ShapeDtypeStruct((B, 2, T, LANES), BF16),
        jax.ShapeDtypeStruct((B, 2, T, LANES), BF16),
        jax.ShapeDtypeStruct((B, 2, T, LANES), F32),
        jax.ShapeDtypeStruct((B, 2, T, LANES), BF16),
        jax.ShapeDtypeStruct((depth, B, T, 256), F32),
        jax.ShapeDtypeStruct((depth, B, T, 256), F32),
        jax.ShapeDtypeStruct((B, 2, T, LANES), BF16),
        jax.ShapeDtypeStruct((B, 2, T, LANES), BF16),
    ]
    out_specs = [stacked(MLA_KV_LORA), stacked(MLA_ROPE), nat(LANES), pair(MLA_HEADS), pair(2), pair(2), pair(2),
                 pair(2), pair(2), stacked(256), stacked(256), pair(2), pair(2)]
    if fuse_kv:
        out_shape += [jax.ShapeDtypeStruct((B, MLA_HEADS, T, LANES), BF16),
                      jax.ShapeDtypeStruct((B, MLA_HEADS // 2, T, LANES), BF16)]
        out_specs += [pair(MLA_HEADS), pair(MLA_HEADS // 2)]
    prev = () if prev is None else tuple(prev)
    aliases = {_IN_PROJ_INPUTS + n: o for n, o in enumerate(_STACKED_OUTS[:len(prev)])}
    return pl.pallas_call(
        functools.partial(_in_proj_body, n_prev=len(prev)), grid=grid,
        in_specs=in_specs + [pl.BlockSpec(memory_space=pl.ANY)] * len(prev),
        out_specs=out_specs, out_shape=out_shape, input_output_aliases=aliases,
        compiler_params=_cparams(2), name="in_proj",
    )(x, lw["g_mix"], lw["w1"], lw["g_q"], lw["wq"], lw["g_kv"], lw["wke"], lw["wv"],
      tabs["cslot"], tabs["slo_slot"], tabs["shi_slot"], tabs["ck"], tabs["slo_k"], tabs["shi_k"],
      tabs["cr"], tabs["slo_r"], tabs["shi_r"], *prev)


def _kv_up_body(ckv_ref, kpe_ref, wke_ref, wv_ref, kcat_ref, v_ref):
    _kv_slots(ckv_ref[0].astype(BF16), kpe_ref[0], wke_ref, wv_ref, kcat_ref, v_ref)


def _kv_up(ckv, kpe_pad, lw):
    B, T, _ = ckv.shape
    tm = _tile(T, TM_KVUP)
    nat = lambda w: pl.BlockSpec((1, tm, w), lambda b, t: (b, t, 0))
    pair = lambda n: pl.BlockSpec((1, n, tm, LANES), lambda b, t: (b, 0, t, 0))
    return pl.pallas_call(
        _kv_up_body, grid=(B, T // tm),
        in_specs=[nat(MLA_KV_LORA), nat(LANES), _full((MLA_KV_LORA + LANES, MLA_HEADS * LANES)),
                  _full((MLA_KV_LORA, MLA_HEADS * MLA_V))],
        out_specs=[pair(MLA_HEADS), pair(MLA_HEADS // 2)],
        out_shape=[jax.ShapeDtypeStruct((B, MLA_HEADS, T, LANES), BF16),
                   jax.ShapeDtypeStruct((B, MLA_HEADS // 2, T, LANES), BF16)],
        compiler_params=_cparams(2), name="kv_up",
    )(ckv, kpe_pad, lw["wke"], lw["wv"])


def _mla_last_block(i, *, q0, tq, tk, nk):
    last_q = q0 + i * tq + (tq - 1)
    last_key = (last_q // CHUNK) * CHUNK + (CHUNK - 1)
    return jnp.minimum(last_key // tk, nk - 1)


def _mla_body(q_ref, k_ref, v_ref, o_ref, m_ref, l_ref, acc_ref, *, q0, tq, tk, nk, kv_len):
    i = pl.program_id(1)

    def step(j, masked, first_block):
        lane = lax.broadcasted_iota(jnp.int32, (tq, LANES), 1)
        first = lane < HALF
        rows = pl.ds(pl.multiple_of(j * tk, tk), tk)
        if masked:
            qpos = q0 + i * tq + lax.broadcasted_iota(jnp.int32, (tq, 1), 0)
            limit = jnp.minimum((jnp.right_shift(qpos, 6) + 1) * CHUNK, kv_len)
            kpos = j * tk + lax.broadcasted_iota(jnp.int32, (tq, tk), 1)
            bias = jnp.where(kpos < limit, 0.0, -jnp.inf).astype(F32)

        def scores(hh):
            s = _dot_nt(q_ref[0, hh], k_ref[0, hh, rows, :])
            return s + bias if masked else s

        s_next = scores(0)
        alphas, pvs = [], []
        for hh in range(MLA_HEADS):
            p, a = divmod(hh, 2)
            s = s_next
            if hh + 1 < MLA_HEADS:
                s_next = scores(hh + 1)
            m_cur = jnp.max(s, axis=1, keepdims=True)
            if first_block:
                m_new = jnp.broadcast_to(m_cur, (tq, LANES))
                e = jnp.exp2((s - m_cur) * _MLA_EXP2_SCALE)
                l_ref[hh] = jnp.broadcast_to(jnp.sum(e, axis=1, keepdims=True), (tq, LANES))
            else:
                m_prev = m_ref[hh]
                m_new = jnp.maximum(m_prev, m_cur)
                e = jnp.exp2((s - _lane_tile(m_new, tk // LANES)) * _MLA_EXP2_SCALE)
                alpha = jnp.exp2((m_prev - m_new) * _MLA_EXP2_SCALE)
                l_ref[hh] = alpha * l_ref[hh] + jnp.sum(e, axis=1, keepdims=True)
                alphas.append(alpha)
            m_ref[hh] = m_new
            pvs.append(_dot(e.astype(BF16), v_ref[0, p, rows, :]))
            if a == 1:
                new = jnp.where(first, pvs[0], pvs[1])
                if first_block:
                    acc_ref[p] = new
                else:
                    acc_ref[p] = acc_ref[p] * jnp.where(first, alphas[0], alphas[1]) + new
                alphas, pvs = [], []

    all_visible = jnp.minimum(((q0 + i * tq) // CHUNK + 1) * CHUNK, kv_len)
    last = _mla_last_block(i, q0=q0, tq=tq, tk=tk, nk=nk)
    n_full = jnp.minimum(all_visible // tk, last + 1)

    @pl.when(n_full > 0)
    def _():
        step(0, False, True)

    @pl.when(n_full == 0)
    def _():
        step(0, True, True)

    def full_step(j, carry):
        step(j, False, False)
        return carry

    def masked_step(j, carry):
        step(j, True, False)
        return carry

    lax.fori_loop(1, n_full, full_step, 0)
    lax.fori_loop(jnp.maximum(n_full, 1), last + 1, masked_step, 0)

    lane = lax.broadcasted_iota(jnp.int32, (tq, LANES), 1)
    for p in range(MLA_HEADS // 2):
        inv = jnp.where(lane < HALF, 1.0 / l_ref[2 * p], 1.0 / l_ref[2 * p + 1])
        o_ref[0, p] = (acc_ref[p] * inv).astype(BF16)


def _mla_attention(qcat, kcat, v, *, q0, kv_len, tq, tk):
    B, H, Tq, _ = qcat.shape
    Tk = kcat.shape[2]
    nq, nk = Tq // tq, Tk // tk
    return pl.pallas_call(
        functools.partial(_mla_body, q0=q0, tq=tq, tk=tk, nk=nk, kv_len=kv_len),
        grid=(B, nq),
        in_specs=[pl.BlockSpec((1, H, tq, LANES), lambda b, i: (b, 0, i, 0)),
                  pl.BlockSpec((1, H, Tk, LANES), lambda b, i: (b, 0, 0, 0)),
                  pl.BlockSpec((1, H // 2, Tk, LANES), lambda b, i: (b, 0, 0, 0))],
        out_specs=pl.BlockSpec((1, H // 2, tq, LANES), lambda b, i: (b, 0, i, 0)),
        out_shape=jax.ShapeDtypeStruct((B, H // 2, Tq, LANES), BF16),
        scratch_shapes=[pltpu.VMEM((H, tq, LANES), F32), pltpu.VMEM((H, tq, LANES), F32),
                        pltpu.VMEM((H // 2, tq, LANES), F32)],
        compiler_params=_cparams(2), name="mla_attn",
    )(qcat, kcat, v)


def _sb_last_block(i, *, q0, tq, tk, nk):
    last_key = q0 + i * tq + (tq - 1) - 1
    return jnp.clip(last_key // tk, 0, nk - 1)


def _sb_body(q_ref, k_ref, v_ref, u_ref, o_ref, run_ref, acc_ref, *, q0, tq, tk, nk):
    i = pl.program_id(1)
    q_first = q0 + i * tq
    last = _sb_last_block(i, q0=q0, tq=tq, tk=tk, nk=nk)
    first_unmasked = jnp.minimum(q_first // tk - 1, last)
    run_ref[...] = jnp.zeros(run_ref.shape, F32)
    acc_ref[...] = jnp.zeros(acc_ref.shape, F32)

    def step(kb, masked):
        first_k = lax.broadcasted_iota(jnp.int32, (tk, LANES), 1) < HALF
        tri = u_ref[...]
        reps = tk // LANES
        rows = pl.ds(pl.multiple_of(kb * tk, tk), tk)
        if masked:
            qpos = q_first + lax.broadcasted_iota(jnp.int32, (tq, 1), 0)
            kpos = kb * tk + lax.broadcasted_iota(jnp.int32, (tq, tk), 1)
            before1 = kpos < qpos
            before = jnp.concatenate([before1, before1], axis=1)

        def scores(p):
            kp = k_ref[0, p, rows, :]
            zero = jnp.zeros_like(kp)
            k2 = jnp.concatenate([jnp.where(first_k, kp, zero), jnp.where(first_k, zero, kp)], axis=0)
            return _dot_nt(q_ref[0, p], k2) * _LOG2E

        npair = SB_HEADS // 2
        zs = [scores(p) for p in range(npair)]
        keeps, betas, betweens = [], [], []
        for p in range(npair):
            z = zs[p]
            nz = -z
            sp = jnp.log(1.0 + jnp.exp2(jnp.minimum(z, nz))) * _LOG2E
            log_keep = jnp.minimum(nz, 0.0) - sp
            betas.append(log_keep + z)
            if masked:
                log_keep = jnp.where(before, log_keep, 0.0)
            keeps.append(log_keep)
            hi = log_keep.astype(BF16)
            lo = (log_keep - hi.astype(F32)).astype(BF16)
            betweens.append(jnp.concatenate(
                [_dot(hi[:, :tk], tri) + _dot(lo[:, :tk], tri), _dot(hi[:, tk:], tri) + _dot(lo[:, tk:], tri)], axis=1))
        for p in range(npair):
            vp = v_ref[0, p, rows, :]
            zero = jnp.zeros_like(vp)
            v2 = jnp.concatenate([jnp.where(first_k, vp, zero), jnp.where(first_k, zero, vp)], axis=0)
            run_a = run_ref[2 * p]
            run_b = run_ref[2 * p + 1]
            run = jnp.concatenate([_lane_tile(run_a, reps), _lane_tile(run_b, reps)], axis=1)
            att = jnp.exp2(betas[p] + betweens[p] + run)
            if masked:
                att = jnp.where(before, att, 0.0)
            run_ref[2 * p] = run_a + jnp.sum(keeps[p][:, :tk], axis=1, keepdims=True)
            run_ref[2 * p + 1] = run_b + jnp.sum(keeps[p][:, tk:], axis=1, keepdims=True)
            acc_ref[p] = acc_ref[p] + _dot(att.astype(BF16), v2)

    def masked_step(t, carry):
        step(last - t, True)
        return carry

    lax.fori_loop(0, last - first_unmasked, masked_step, 0)

    def more(carry):
        kb, live = carry
        return jnp.logical_and(kb >= 0, live > 0)

    def unmasked_step(carry):
        kb, _ = carry
        step(kb, False)
        live = (jnp.max(run_ref[...]) >= _SB_UNDERFLOW_LOG2).astype(jnp.int32)
        return kb - 1, live

    lax.while_loop(more, unmasked_step, (first_unmasked, jnp.int32(1)))
    o_ref[0] = acc_ref[...].astype(BF16)


def _sb_attention(q, k, v, *, q0, tq, tk):
    B, P, Tq, _ = q.shape
    Tk = k.shape[2]
    nq, nk = Tq // tq, Tk // tk
    kw = dict(q0=q0, tq=tq, tk=tk, nk=nk)
    r = lax.broadcasted_iota(jnp.int32, (tk, tk), 0)
    c = lax.broadcasted_iota(jnp.int32, (tk, tk), 1)
    tri = (r > c).astype(BF16)

    kv_spec = pl.BlockSpec((1, P, Tk, LANES), lambda b, i: (b, 0, 0, 0))
    return pl.pallas_call(
        functools.partial(_sb_body, **kw),
        grid=(B, nq),
        in_specs=[pl.BlockSpec((1, P, tq, LANES), lambda b, i: (b, 0, i, 0)), kv_spec, kv_spec, _full((tk, tk))],
        out_specs=pl.BlockSpec((1, P, tq, LANES), lambda b, i: (b, 0, i, 0)),
        out_shape=jax.ShapeDtypeStruct((B, P, Tq, LANES), BF16),
        scratch_shapes=[pltpu.VMEM((2 * P, tq, LANES), F32), pltpu.VMEM((P, tq, LANES), F32)],
        compiler_params=_cparams(2), name="sb_attn",
    )(q, k, v, tri)


def _ret_body(q_ref, k_ref, v_ref, g_ref, s0_ref, dmask_ref, dq_ref, dk_ref, cd_ref, bd_ref, gn_ref,
              y_ref, snew_ref, state_ref, *, tc, lc, nc):
    c = pl.program_id(1)

    @pl.when(c == 0)
    def _():
        state_ref[...] = s0_ref[0]

    lane = lax.broadcasted_iota(jnp.int32, (lc, LANES), 1)
    first = lane < HALF
    for p in range(RET_HEADS // 2):
        for ch in range(tc // lc):
            rows = slice(ch * lc, (ch + 1) * lc)
            q = q_ref[0, p, rows, :]
            k = k_ref[0, p, rows, :]
            v = v_ref[0, p, rows, :]
            state = state_ref[p]
            cross = _dot((q.astype(F32) * dq_ref[p]).astype(BF16), state.astype(BF16))
            zero = jnp.zeros_like(q)
            inner = []
            for a in range(2):
                qa = jnp.where(first, q, zero) if a == 0 else jnp.where(first, zero, q)
                scores = _dot_nt(qa, k) * dmask_ref[2 * p + a]
                inner.append(_dot(scores.astype(BF16), v))
            o = jnp.where(first, inner[0], inner[1]) + cross
            kd = (k.astype(F32) * dk_ref[p]).astype(BF16)
            kv = lax.dot_general(kd, v, (((0,), (0,)), ((), ())), preferred_element_type=F32)
            state_ref[p] = cd_ref[p] * state + bd_ref[...] * kv
            zf = jnp.zeros_like(o)
            mu = jnp.where(first, jnp.sum(jnp.where(first, o, zf), axis=1, keepdims=True),
                           jnp.sum(jnp.where(first, zf, o), axis=1, keepdims=True)) * (1.0 / RET_DV)
            d = o - mu
            d2 = d * d
            var = jnp.where(first, jnp.sum(jnp.where(first, d2, zf), axis=1, keepdims=True),
                            jnp.sum(jnp.where(first, zf, d2), axis=1, keepdims=True)) * (1.0 / RET_DV)
            yn = d * lax.rsqrt(var + EPS) * gn_ref[p]
            gate = g_ref[0, p, rows, :]
            y_ref[0, p, rows, :] = (yn * (gate * jax.nn.sigmoid(gate))).astype(BF16)

    @pl.when(c == nc - 1)
    def _():
        snew_ref[0] = state_ref[...]


def _ret_tables(lc):
    log_gamma = jnp.log(1.0 - 2.0 ** (-5.0 - jnp.arange(RET_HEADS, dtype=F32)))
    idx = jnp.arange(lc, dtype=F32)
    diff = idx[:, None] - idx[None, :]
    dmask = jnp.where(diff[None] >= 0, jnp.exp(jnp.maximum(diff, 0.0)[None] * log_gamma[:, None, None]), 0.0)
    dk = jnp.exp((lc - 1.0 - idx)[:, None] * log_gamma[None, :])
    dq = jnp.exp((idx + 1.0)[:, None] * log_gamma[None, :])
    cd = jnp.exp(lc * log_gamma)

    def lanes(t):
        t = jnp.repeat(t[:, :, None], HALF, axis=2).reshape(t.shape[0], RET_HEADS // 2, LANES)
        return jnp.transpose(t, (1, 0, 2))

    blk = jnp.arange(LANES) // HALF
    bd = (blk[:, None] == blk[None, :]).astype(F32)
    cd_rows = jnp.repeat(cd, HALF).reshape(RET_HEADS // 2, LANES, 1)
    return dmask, lanes(dq), lanes(dk), cd_rows * bd[None], bd


def _state_to_pairs(s):
    B = s.shape[0]
    s = s.astype(F32).reshape(B, RET_HEADS // 2, 2, RET_DK, RET_DV)
    z = jnp.zeros_like(s[:, :, 0])
    top = jnp.concatenate([s[:, :, 0], z], axis=-1)
    bot = jnp.concatenate([z, s[:, :, 1]], axis=-1)
    return jnp.concatenate([top, bot], axis=-2)


def _pairs_to_state(sp):
    a = sp[:, :, :HALF, :HALF]
    b = sp[:, :, HALF:, HALF:]
    B = sp.shape[0]
    return jnp.stack([a, b], axis=2).reshape(B, RET_HEADS, RET_DK, RET_DV)


def _retention(rq, rk, rv, rg, s0_pairs, g_norm_pairs, *, lc):
    B, P, T, _ = rq.shape
    tc = _tile(T, TC_RET)
    nc = T // tc
    dmask, dq, dk, cd, bd = _ret_tables(lc)
    pair = pl.BlockSpec((1, P, tc, LANES), lambda b, c: (b, 0, c, 0))
    st = pl.BlockSpec((1, P, LANES, LANES), lambda b, c: (b, 0, 0, 0))
    return pl.pallas_call(
        functools.partial(_ret_body, tc=tc, lc=lc, nc=nc),
        grid=(B, nc),
        in_specs=[pair, pair, pair, pair, st, _full(dmask.shape), _full(dq.shape), _full(dk.shape),
                  _full(cd.shape), _full(bd.shape), _full(g_norm_pairs.shape)],
        out_specs=[pair, st],
        out_shape=[jax.ShapeDtypeStruct((B, P, T, LANES), BF16),
                   jax.ShapeDtypeStruct((B, P, LANES, LANES), F32)],
        scratch_shapes=[pltpu.VMEM((P, LANES, LANES), F32)],
        compiler_params=_cparams(2), name="retention",
    )(rq, rk, rv, rg, s0_pairs, dmask, dq, dk, cd, bd, g_norm_pairs)


def _out_mem_body(x_ref, omla_ref, ret_ref, osb_ref, wout_ref, gq_ref, wq_ref, mk_ref, mv_ref, wo_ref, y_ref):
    mixed = jnp.concatenate([ref[0, p] for ref, n in ((omla_ref, MLA_HEADS // 2), (ret_ref, RET_HEADS // 2),
                                                      (osb_ref, SB_HEADS // 2)) for p in range(n)], axis=1)
    x1 = x_ref[0] + _dot(mixed, wout_ref[...])
    hq = _rms(x1, gq_ref[...]).astype(BF16)
    q = _dot(hq, wq_ref[...]).astype(BF16)
    cols = [slice(hh * MEM_HD, (hh + 1) * MEM_HD) for hh in range(MEM_HEADS)]
    scores = [_dot_nt(q[:, sl], mk_ref[0, :, sl]) * MEM_SCALE for sl in cols]
    probs = []
    for s in scores:
        e = jnp.exp(s - jnp.max(s, axis=1, keepdims=True))
        probs.append((e / jnp.sum(e, axis=1, keepdims=True)).astype(BF16))
    heads = [_dot(pr, mv_ref[0, :, sl]).astype(BF16) for pr, sl in zip(probs, cols)]
    o = jnp.concatenate(heads, axis=1)
    y_ref[0] = x1 + _dot(o, wo_ref[...])


def _out_mem(x, o_mla, ret_y, o_sb, mem_k, mem_v, lw):
    B, T, D = x.shape
    tm = _tile(T, TM_OUT)
    M = mem_k.shape[1]
    mw = MEM_HEADS * MEM_HD
    nat = pl.BlockSpec((1, tm, D), lambda b, t: (b, t, 0))
    pair = lambda n: pl.BlockSpec((1, n, tm, LANES), lambda b, t: (b, 0, t, 0))
    mem = pl.BlockSpec((1, M, mw), lambda b, t: (b, 0, 0))
    return pl.pallas_call(
        _out_mem_body, grid=(B, T // tm),
        in_specs=[nat, pair(MLA_HEADS // 2), pair(RET_HEADS // 2), pair(SB_HEADS // 2), _full((D, D)),
                  _full((1, D)), _full((D, mw)), mem, mem, _full((mw, D))],
        out_specs=nat, out_shape=jax.ShapeDtypeStruct((B, T, D), F32),
        compiler_params=_cparams(2), name="out_mem",
    )(x, o_mla, ret_y, o_sb, lw["w_out"], lw["g_mem_q"], lw["w_mem_q"], mem_k, mem_v, lw["w_mem_o"])


def _mem_kv_body(mem_ref, g_ref, wk_ref, wv_ref, k_ref, v_ref, kb_ref, vb_ref):
    m = _rms(mem_ref[0], g_ref[...]).astype(BF16)
    k = _dot(m, wk_ref[...])
    v = _dot(m, wv_ref[...])
    k_ref[0] = k
    v_ref[0] = v
    kb_ref[0] = k.astype(BF16)
    vb_ref[0] = v.astype(BF16)


def _mem_kv(mem, lw):
    B, M, D = mem.shape
    mw = MEM_HEADS * MEM_HD
    o = pl.BlockSpec((1, M, mw), lambda b: (b, 0, 0))
    return pl.pallas_call(
        _mem_kv_body, grid=(B,),
        in_specs=[pl.BlockSpec((1, M, D), lambda b: (b, 0, 0)), _full((1, D)), _full((D, mw)), _full((D, mw))],
        out_specs=[o, o, o, o],
        out_shape=[jax.ShapeDtypeStruct((B, M, mw), F32), jax.ShapeDtypeStruct((B, M, mw), F32),
                   jax.ShapeDtypeStruct((B, M, mw), BF16), jax.ShapeDtypeStruct((B, M, mw), BF16)],
        compiler_params=_cparams(1), name="mem_kv",
    )(mem, lw["g_mem_kv"], lw["w_mem_k"], lw["w_mem_v"])


_CARRY_ROW = 8 - (CONV_W - 1)


def _ffn_body(x_ref, g_ref, wup_ref, wconv_ref, bconv_ref, wdown_ref, conv0_ref, gfin_ref,
              y_ref, convnew_ref, carry_ref, exta_ref, extb_ref, act_ref, *, tm, nt, final):
    t = pl.program_id(1)

    @pl.when(t == 0)
    def _():
        carry_ref[_CARRY_ROW:8, :] = conv0_ref[0]

    x = x_ref[0]
    h = _rms(x, g_ref[...]).astype(BF16)

    def conv_part(c0, ext_ref):
        cols = slice(c0, c0 + TF_FFN)
        u = _dot(h, wup_ref[:, cols])
        ext_ref[_CARRY_ROW:8, :] = carry_ref[_CARRY_ROW:8, cols]
        ext_ref[8:8 + tm, :] = u
        carry_ref[_CARRY_ROW:8, cols] = u[tm - (CONV_W - 1):, :]
        w = wconv_ref[:, cols]
        c = bconv_ref[:, cols]
        c = c + ext_ref[_CARRY_ROW:_CARRY_ROW + tm, :] * w[0:1]
        c = c + ext_ref[_CARRY_ROW + 1:_CARRY_ROW + 1 + tm, :] * w[1:2]
        return c + u * w[2:3]

    for ci in range(D_FF // TF_FFN):
        c0 = ci * TF_FFN
        a = conv_part(c0, exta_ref)
        b = conv_part(D_FF + c0, extb_ref)
        act_ref[:, c0:c0 + TF_FFN] = (a * jax.nn.sigmoid(a) * b).astype(BF16)

    x3 = x + _dot(act_ref[...], wdown_ref[...])
    y_ref[0] = _rms(x3, gfin_ref[...]) if final else x3

    @pl.when(t == nt - 1)
    def _():
        convnew_ref[0] = carry_ref[_CARRY_ROW:8, :]


def _conv_ffn(x, conv0, lw, g_final, *, final):
    B, T, D = x.shape
    tm = _tile(T, TM_FFN)
    nt = T // tm
    f2 = 2 * D_FF
    nat = pl.BlockSpec((1, tm, D), lambda b, t: (b, t, 0))
    cv = pl.BlockSpec((1, CONV_W - 1, f2), lambda b, t: (b, 0, 0))
    return pl.pallas_call(
        functools.partial(_ffn_body, tm=tm, nt=nt, final=final),
        grid=(B, nt),
        in_specs=[nat, _full((1, D)), _full((D, f2)), _full((CONV_W, f2)), _full((1, f2)), _full((D_FF, D)),
                  cv, _full((1, D))],
        out_specs=[nat, cv],
        out_shape=[jax.ShapeDtypeStruct((B, T, D), F32), jax.ShapeDtypeStruct((B, CONV_W - 1, f2), F32)],
        scratch_shapes=[pltpu.VMEM((8, f2), F32), pltpu.VMEM((8 + tm, TF_FFN), F32),
                        pltpu.VMEM((8 + tm, TF_FFN), F32), pltpu.VMEM((tm, D_FF), BF16)],
        compiler_params=_cparams(2), name="conv_ffn",
    )(x, lw["g_ffn"], lw["w_ffn_up"], lw["w_ffn_conv"], lw["b_ffn_conv"], lw["w_ffn_down"], conv0, g_final)


def _pad_cols(w, width):
    return jnp.pad(w, ((0, 0), (0, width - w.shape[1])))


def _prep_layer(l, g_mix, w_in, g_q_lora, w_q_up, g_kv_lora, w_kv_up, g_ret_norm, w_out,
                g_mem_q, g_mem_kv, w_mem_q, w_mem_k, w_mem_v, w_mem_o,
                g_ffn, w_ffn_up, w_ffn_conv, b_ffn_conv, w_ffn_down):
    w = w_in[l]
    parts, c0 = [], 0
    for n in IN_SIZES:
        parts.append(w[:, c0:c0 + n])
        c0 += n
    cq, ckv, kpe, rq, rk, rv, rg, sq, sk, sv = parts
    w1 = jnp.concatenate([cq, ckv, _pad_cols(kpe, LANES), rq, rk, rv, rg, sq, sk, sv], axis=1).astype(BF16)
    kq = w_q_up.shape[1]
    wq3 = w_q_up[l].reshape(kq, MLA_HEADS, MLA_NOPE + MLA_ROPE)
    tail = LANES - MLA_NOPE - MLA_ROPE
    q_pad = jnp.pad(wq3, ((0, 0), (0, 0), (0, tail))).reshape(kq, MLA_HEADS * LANES)
    wkv3 = w_kv_up[l].reshape(MLA_KV_LORA, MLA_HEADS, MLA_NOPE + MLA_V)
    wk = jnp.pad(wkv3[:, :, :MLA_NOPE], ((0, 0), (0, 0), (0, LANES - MLA_NOPE))).reshape(MLA_KV_LORA, MLA_HEADS * LANES)
    wv = wkv3[:, :, MLA_NOPE:].reshape(MLA_KV_LORA, MLA_HEADS * MLA_V)
    r = jnp.arange(LANES)[:, None]
    c = jnp.arange(MLA_HEADS * LANES)[None, :]
    we = ((r < MLA_ROPE) & (c % LANES == r + MLA_NOPE)).astype(BF16)
    row = lambda g: g[l].reshape(1, -1).astype(F32)
    return dict(
        g_mix=row(g_mix), w1=w1, g_q=row(g_q_lora), wq=q_pad.astype(BF16),
        g_kv=row(g_kv_lora), wke=jnp.concatenate([wk.astype(BF16), we], axis=0), wv=wv.astype(BF16),
        g_ret=g_ret_norm[l].astype(F32).reshape(RET_HEADS // 2, 1, LANES),
        w_out=w_out[l].astype(BF16), g_mem_q=row(g_mem_q), g_mem_kv=row(g_mem_kv),
        w_mem_q=w_mem_q[l].astype(BF16), w_mem_k=w_mem_k[l].astype(BF16), w_mem_v=w_mem_v[l].astype(BF16),
        w_mem_o=w_mem_o[l].astype(BF16), g_ffn=row(g_ffn), w_ffn_up=w_ffn_up[l].astype(BF16),
        w_ffn_conv=w_ffn_conv[l].astype(F32), b_ffn_conv=row(b_ffn_conv), w_ffn_down=w_ffn_down[l].astype(BF16))


def _rope_tables(pos):
    t = pos.shape[0]
    posf = pos.astype(F32)[:, None]

    def cos_sin(d):
        inv = ROPE_BASE ** (-jnp.arange(0, d, 2, dtype=F32) / d)
        ang = posf * inv[None, :]
        return jnp.cos(ang), jnp.sin(ang)

    c16, s16 = cos_sin(MLA_ROPE)
    c32, s32 = cos_sin(RET_DK)
    z = lambda n: jnp.zeros((t, n), F32)
    tail = LANES - MLA_NOPE - MLA_ROPE
    return dict(
        cslot=jnp.concatenate([jnp.ones((t, MLA_NOPE), F32), c16, c16, z(tail)], axis=1),
        slo_slot=jnp.concatenate([z(MLA_NOPE), -s16, z(MLA_ROPE // 2 + tail)], axis=1),
        shi_slot=jnp.concatenate([z(MLA_NOPE + MLA_ROPE // 2), s16, z(tail)], axis=1),
        ck=jnp.concatenate([c16, c16, z(LANES - MLA_ROPE)], axis=1),
        slo_k=jnp.concatenate([-s16, z(LANES - MLA_ROPE // 2)], axis=1),
        shi_k=jnp.concatenate([z(MLA_ROPE // 2), s16, z(LANES - MLA_ROPE)], axis=1),
        cr=jnp.tile(jnp.concatenate([c32, c32], axis=1), (1, LANES // RET_DK)),
        slo_r=jnp.tile(jnp.concatenate([-s32, z(RET_DK // 2)], axis=1), (1, LANES // RET_DK)),
        shi_r=jnp.tile(jnp.concatenate([z(RET_DK // 2), s32], axis=1), (1, LANES // RET_DK)))


def _pad_rows(a, axis, n):
    pad = [(0, 0)] * a.ndim
    pad[axis] = (0, n - a.shape[axis])
    return jnp.pad(a, pad)


def _to_pairs(a):
    B, T, _ = a.shape
    return jnp.transpose(a.reshape(B, T, 2, LANES), (0, 2, 1, 3))


def _layer(x, tabs, lw, mem_k, mem_v, past, g_final, *, q0, layer, depth, prev):
    B, T, _ = x.shape
    final = layer == depth - 1
    outs = _in_proj(x, tabs, lw, fuse_kv=past is None, layer=layer, depth=depth, prev=prev)
    (ckv_st, kpe_st, kpe_b, qcat, rq, rk, rv, rg, sq, sk_st, sv_st, sk_b, sv_b) = outs[:13]
    if past is None:
        kcat, vmla = outs[13:]
        sk_all, sv_all = sk_b, sv_b
        s0 = jnp.zeros((B, RET_HEADS // 2, LANES, LANES), F32)
        conv0 = jnp.zeros((B, CONV_W - 1, 2 * D_FF), F32)
        kv_len = T
        lc = _tile(T, L_RET)
        tq_mla, tk_mla = _tile(T, TQ_MLA), _tile(T, TK_MLA)
        tq_sb, tk_sb = _tile(T, TQ_SB), _tile(T, TK_SB)
    else:
        ckv_c, kpe_c, s0, sk_c, sv_c, conv0 = past
        P = ckv_c.shape[1]
        kv_len = P + T
        tk_sb = TK_SB
        tk_pad = -(-kv_len // tk_sb) * tk_sb
        ckv_all = _pad_rows(jnp.concatenate([ckv_c.astype(F32), ckv_st[layer]], axis=1), 1, tk_pad)
        kpe_cb = _pad_cols(kpe_c.reshape(B * P, MLA_ROPE), LANES).reshape(B, P, LANES).astype(BF16)
        kpe_all = _pad_rows(jnp.concatenate([kpe_cb, kpe_b], axis=1), 1, tk_pad)
        sk_all = _pad_rows(jnp.concatenate([_to_pairs(sk_c.reshape(B, P, -1).astype(BF16)), sk_b], axis=2), 2, tk_pad)
        sv_all = _pad_rows(jnp.concatenate([_to_pairs(sv_c.reshape(B, P, -1).astype(BF16)), sv_b], axis=2), 2, tk_pad)
        s0 = _state_to_pairs(s0)
        conv0 = conv0.astype(F32)
        lc = T
        tq_mla, tk_mla = T, tk_pad
        tq_sb = T
        kcat, vmla = _kv_up(ckv_all, kpe_all, lw)
    o_mla = _mla_attention(qcat, kcat, vmla, q0=q0, kv_len=kv_len, tq=tq_mla, tk=tk_mla)
    o_sb = _sb_attention(sq, sk_all, sv_all, q0=q0, tq=tq_sb, tk=tk_sb)
    ret_y, s_new = _retention(rq, rk, rv, rg, s0, lw["g_ret"], lc=lc)
    x = _out_mem(x, o_mla, ret_y, o_sb, mem_k, mem_v, lw)
    x, conv_new = _conv_ffn(x, conv0, lw, g_final, final=final)
    return x, (ckv_st, kpe_st, sk_st, sv_st), (_pairs_to_state(s_new), conv_new)


def kernel(x_prompt, x_sample, cache_mla_ckv, cache_mla_kpe, state_ret, cache_sb_k, cache_sb_v, cache_mem_k, cache_mem_v, state_ffn_conv, mem_prompt, g_mix, w_in, g_q_lora, w_q_up, g_kv_lora, w_kv_up, g_ret_norm, w_out, g_mem_q, g_mem_kv, w_mem_q, w_mem_k, w_mem_v, w_mem_o, g_ffn, w_ffn_up, w_ffn_conv, b_ffn_conv, w_ffn_down, g_final):
    depth = w_in.shape[0]
    layers = [_prep_layer(l, g_mix, w_in, g_q_lora, w_q_up, g_kv_lora, w_kv_up, g_ret_norm, w_out,
                          g_mem_q, g_mem_kv, w_mem_q, w_mem_k, w_mem_v, w_mem_o,
                          g_ffn, w_ffn_up, w_ffn_conv, b_ffn_conv, w_ffn_down) for l in range(depth)]
    gfin = g_final.reshape(1, -1).astype(F32)
    mw = MEM_HEADS * MEM_HD

    xp = x_prompt
    Bp, Tp, _ = xp.shape
    tabs_p = _rope_tables(jnp.arange(Tp))
    p_states, p_mem_k, p_mem_v = [], [], []
    p_caches = None
    for l in range(depth):
        mk, mv, mk_b, mv_b = _mem_kv(mem_prompt, layers[l])
        xp, p_caches, st = _layer(xp, tabs_p, layers[l], mk_b, mv_b, None, gfin, q0=0, layer=l, depth=depth,
                                  prev=p_caches)
        p_states.append(st)
        p_mem_k.append(mk.reshape(Bp, -1, MEM_HEADS, MEM_HD))
        p_mem_v.append(mv.reshape(Bp, -1, MEM_HEADS, MEM_HD))

    xs = x_sample
    Bs, Ts, _ = xs.shape
    past_len = cache_mla_ckv.shape[2]
    tabs_s = _rope_tables(past_len + jnp.arange(Ts))
    s_states = []
    s_caches = None
    for l in range(depth):
        past = (cache_mla_ckv[l], cache_mla_kpe[l], state_ret[l], cache_sb_k[l], cache_sb_v[l], state_ffn_conv[l])
        mk_b = cache_mem_k[l].reshape(Bs, -1, mw).astype(BF16)
        mv_b = cache_mem_v[l].reshape(Bs, -1, mw).astype(BF16)
        xs, s_caches, st = _layer(xs, tabs_s, layers[l], mk_b, mv_b, past, gfin, q0=past_len, layer=l, depth=depth,
                                  prev=s_caches)
        s_states.append(st)

    def leaves(caches, states, b, t):
        ckv, kpe, sk, sv = caches
        ret, conv = [jnp.stack(f) for f in zip(*states)]
        heads = (depth, b, t, SB_HEADS, SB_HD)
        return ckv, kpe, ret, sk.reshape(heads), sv.reshape(heads), conv

    p_ckv, p_kpe, p_ret, p_sbk, p_sbv, p_conv = leaves(p_caches, p_states, Bp, Tp)
    s_ckv, s_kpe, s_ret, s_sbk, s_sbv, s_conv = leaves(s_caches, s_states, Bs, Ts)
    return (xp, xs, p_ckv, p_kpe, p_ret, p_sbk, p_sbv, jnp.stack(p_mem_k), jnp.stack(p_mem_v), p_conv,
            s_ckv, s_kpe, s_ret, s_sbk, s_sbv, s_conv)
```

```python
import functools

import jax
import jax.numpy as jnp
from jax import lax
from jax.experimental import pallas as pl
from jax.experimental.pallas import tpu as pltpu

F32 = jnp.float32
BF16 = jnp.bfloat16

CHUNK = 64
EPS = 1e-6
ROPE_BASE = 10000.0

MLA_HEADS = 8
MLA_NOPE = 64
MLA_ROPE = 32
MLA_V = 64
MLA_Q_LORA = 256
MLA_KV_LORA = 128
MLA_SCALE = (MLA_NOPE + MLA_ROPE) ** -0.5
_LOG2E = 1.4426950408889634
_MLA_EXP2_SCALE = MLA_SCALE * _LOG2E
_SB_UNDERFLOW_LOG2 = -160.0
RET_HEADS = 4
RET_DK = 64
RET_DV = 64
RET_SCALE = RET_DK ** -0.5
SB_HEADS = 4
SB_HD = 64
SB_SCALE = SB_HD ** -0.5
MEM_HEADS = 4
MEM_HD = 128
MEM_SCALE = MEM_HD ** -0.5
D_FF = 2816
CONV_W = 3

IN_SIZES = (MLA_Q_LORA, MLA_KV_LORA, MLA_ROPE, RET_HEADS * RET_DK, RET_HEADS * RET_DK,
            RET_HEADS * RET_DV, RET_HEADS * RET_DV, SB_HEADS * SB_HD, SB_HEADS * SB_HD, SB_HEADS * SB_HD)

LANES = 128
HALF = LANES // 2
VMEM_LIMIT = 56 * 1024 * 1024

TM_PROJ = 1024
TM_KVUP = 2304
TQ_MLA = 512
TK_MLA = 512
TQ_SB = 256
TK_SB = 256
TC_RET = 512
L_RET = 256
TM_OUT = 1024
TM_FFN = 512
TF_FFN = 256


def _tile(n, target):
    t = min(n, target)
    while n % t:
        t -= 1
    return t


def _cparams(n_axes):
    return pltpu.CompilerParams(dimension_semantics=("arbitrary",) * n_axes, vmem_limit_bytes=VMEM_LIMIT)


def _rms(x, g):
    return x * lax.rsqrt(jnp.mean(x * x, axis=-1, keepdims=True) + EPS) * g


def _lane_tile(x, reps):
    return jnp.concatenate([x] * reps, axis=1) if reps > 1 else x


def _full(shape):
    n = len(shape)
    return pl.BlockSpec(shape, lambda *_: (0,) * n)


def _dot(a, b):
    return jnp.dot(a, b, preferred_element_type=F32)


def _dot_nt(a, b):
    return lax.dot_general(a, b, (((1,), (1,)), ((), ())), preferred_element_type=F32)


_C_CQ = 0
_C_CKV = 256
_C_KPE = 384
_C_RQ = 512
_C_RK = 768
_C_RV = 1024
_C_RG = 1280
_C_SQ = 1536
_C_SK = 1792
_C_SV = 2048
_W1_COLS = 2304


def _rope(x, cos, sin_lo, sin_hi, half):
    return x * cos + pltpu.roll(x, half, 1) * sin_hi + pltpu.roll(x, LANES - half, 1) * sin_lo


def _kv_slots(c, kp, wke_ref, wv_ref, kcat_ref, v_ref):
    ck = jnp.concatenate([c, kp], axis=1)
    for p in range(MLA_HEADS // 2):
        k2 = _dot(ck, wke_ref[:, 2 * LANES * p:2 * LANES * (p + 1)])
        kcat_ref[0, 2 * p] = k2[:, :LANES].astype(BF16)
        kcat_ref[0, 2 * p + 1] = k2[:, LANES:].astype(BF16)
    for p in range(MLA_HEADS // 4):
        v2 = _dot(c, wv_ref[:, 2 * LANES * p:2 * LANES * (p + 1)])
        v_ref[0, 2 * p] = v2[:, :LANES].astype(BF16)
        v_ref[0, 2 * p + 1] = v2[:, LANES:].astype(BF16)


_IN_PROJ_INPUTS = 17


def _in_proj_body(*refs, n_prev):
    (x_ref, gmix_ref, w1_ref, gq_ref, wq_ref, gkv_ref, wke_ref, wv_ref,
     cslot_ref, slo_slot_ref, shi_slot_ref, ck_ref, slo_k_ref, shi_k_ref, cr_ref, slo_r_ref,
     shi_r_ref) = refs[:_IN_PROJ_INPUTS]
    outs = refs[_IN_PROJ_INPUTS + n_prev:]
    (ckv_ref, kpe_ref, kpeb_ref, qcat_ref, rq_ref, rk_ref, rv_ref, rg_ref,
     sq_ref, sk_ref, sv_ref, skb_ref, svb_ref) = outs[:13]
    kv_refs = outs[13:]
    h = _rms(x_ref[0], gmix_ref[...]).astype(BF16)

    def proj(c0, width):
        return _dot(h, w1_ref[:, c0:c0 + width])

    cq = proj(_C_CQ, 256)
    ckv_raw = proj(_C_CKV, 128)
    kpe_raw = proj(_C_KPE, 128)
    rq = proj(_C_RQ, 256)
    rk = proj(_C_RK, 256)
    rv = proj(_C_RV, 256)
    rg = proj(_C_RG, 256)
    sq = proj(_C_SQ, 256)
    sk = proj(_C_SK, 256)
    sv = proj(_C_SV, 256)

    ckv = _rms(ckv_raw, gkv_ref[...])
    ckv_ref[0, 0] = ckv
    kpe = _rope(kpe_raw, ck_ref[...], slo_k_ref[...], shi_k_ref[...], MLA_ROPE // 2)
    kpe_ref[0, 0] = kpe[:, :MLA_ROPE]
    kpe_b = kpe.astype(BF16)
    kpeb_ref[0] = kpe_b
    cqn = _rms(cq, gq_ref[...]).astype(BF16)
    cslot, slo_slot, shi_slot = cslot_ref[...], slo_slot_ref[...], shi_slot_ref[...]
    for p in range(MLA_HEADS // 2):
        q2 = _dot(cqn, wq_ref[:, 2 * LANES * p:2 * LANES * (p + 1)])
        for a in range(2):
            qs = q2[:, a * LANES:(a + 1) * LANES]
            qcat_ref[0, 2 * p + a] = _rope(qs, cslot, slo_slot, shi_slot, MLA_ROPE // 2).astype(BF16)
    if kv_refs:
        _kv_slots(ckv.astype(BF16), kpe_b, wke_ref, wv_ref, *kv_refs)

    cr, slo_r, shi_r = cr_ref[...], slo_r_ref[...], shi_r_ref[...]
    sk_ref[0, 0] = sk
    sv_ref[0, 0] = sv
    for p in range(2):
        sl = slice(p * LANES, (p + 1) * LANES)
        rq_ref[0, p] = _rope(rq[:, sl], cr, slo_r, shi_r, RET_DK // 2).astype(BF16)
        rk_ref[0, p] = (_rope(rk[:, sl], cr, slo_r, shi_r, RET_DK // 2) * RET_SCALE).astype(BF16)
        rv_ref[0, p] = rv[:, sl].astype(BF16)
        rg_ref[0, p] = rg[:, sl]
        sq_ref[0, p] = (sq[:, sl] * SB_SCALE).astype(BF16)
        skb_ref[0, p] = sk[:, sl].astype(BF16)
        svb_ref[0, p] = sv[:, sl].astype(BF16)


_STACKED_OUTS = (0, 1, 9, 10)


def _in_proj(x, tabs, lw, *, fuse_kv, layer, depth, prev):
    B, T, D = x.shape
    tm = _tile(T, TM_PROJ)
    grid = (B, T // tm)
    row = pl.BlockSpec((tm, LANES), lambda b, t: (t, 0))
    pair = lambda n: pl.BlockSpec((1, n, tm, LANES), lambda b, t: (b, 0, t, 0))
    nat = lambda w: pl.BlockSpec((1, tm, w), lambda b, t: (b, t, 0))
    stacked = lambda w: pl.BlockSpec((1, 1, tm, w), lambda b, t: (layer, b, t, 0))
    in_specs = [nat(D), _full((1, D)), _full((D, _W1_COLS)), _full((1, MLA_Q_LORA)),
                _full((MLA_Q_LORA, MLA_HEADS * LANES)), _full((1, MLA_KV_LORA)),
                _full((MLA_KV_LORA + LANES, MLA_HEADS * LANES)), _full((MLA_KV_LORA, MLA_HEADS * MLA_V))] + [row] * 9
    assert len(in_specs) == _IN_PROJ_INPUTS
    out_shape = [
        jax.ShapeDtypeStruct((depth, B, T, MLA_KV_LORA), F32),
        jax.ShapeDtypeStruct((depth, B, T, MLA_ROPE), F32),
        jax.ShapeDtypeStruct((B, T, LANES), BF16),
        jax.ShapeDtypeStruct((B, MLA_HEADS, T, LANES), BF16),
        jax.ShapeDtypeStruct((B, 2, T, LANES), BF16),
        jax.ShapeDtypeStruct((B, 2, T, LANES), BF16),
        jax.ShapeDtypeStruct((B, 2, T, LANES), BF16),
        jax.ShapeDtypeStruct((B, 2, T, LANES), F32),
        jax.ShapeDtypeStruct((B, 2, T, LANES), BF16),
        jax.ShapeDtypeStruct((depth, B, T, 256), F32),
        jax.ShapeDtypeStruct((depth, B, T, 256), F32),
        jax.ShapeDtypeStruct((B, 2, T, LANES), BF16),
        jax.ShapeDtypeStruct((B, 2, T, LANES), BF16),
    ]
    out_specs = [stacked(MLA_KV_LORA), stacked(MLA_ROPE), nat(LANES), pair(MLA_HEADS), pair(2), pair(2), pair(2),
                 pair(2), pair(2), stacked(256), stacked(256), pair(2), pair(2)]
    if fuse_kv:
        out_shape += [jax.ShapeDtypeStruct((B, MLA_HEADS, T, LANES), BF16),
                      jax.ShapeDtypeStruct((B, MLA_HEADS // 2, T, LANES), BF16)]
        out_specs += [pair(MLA_HEADS), pair(MLA_HEADS // 2)]
    prev = () if prev is None else tuple(prev)
    aliases = {_IN_PROJ_INPUTS + n: o for n, o in enumerate(_STACKED_OUTS[:len(prev)])}
    return pl.pallas_call(
        functools.partial(_in_proj_body, n_prev=len(prev)), grid=grid,
        in_specs=in_specs + [pl.BlockSpec(memory_space=pl.ANY)] * len(prev),
        out_specs=out_specs, out_shape=out_shape, input_output_aliases=aliases,
        compiler_params=_cparams(2), name="in_proj",
    )(x, lw["g_mix"], lw["w1"], lw["g_q"], lw["wq"], lw["g_kv"], lw["wke"], lw["wv"],
      tabs["cslot"], tabs["slo_slot"], tabs["shi_slot"], tabs["ck"], tabs["slo_k"], tabs["shi_k"],
      tabs["cr"], tabs["slo_r"], tabs["shi_r"], *prev)


def _kv_up_body(ckv_ref, kpe_ref, wke_ref, wv_ref, kcat_ref, v_ref):
    _kv_slots(ckv_ref[0].astype(BF16), kpe_ref[0], wke_ref, wv_ref, kcat_ref, v_ref)


def _kv_up(ckv, kpe_pad, lw):
    B, T, _ = ckv.shape
    tm = _tile(T, TM_KVUP)
    nat = lambda w: pl.BlockSpec((1, tm, w), lambda b, t: (b, t, 0))
    pair = lambda n: pl.BlockSpec((1, n, tm, LANES), lambda b, t: (b, 0, t, 0))
    return pl.pallas_call(
        _kv_up_body, grid=(B, T // tm),
        in_specs=[nat(MLA_KV_LORA), nat(LANES), _full((MLA_KV_LORA + LANES, MLA_HEADS * LANES)),
                  _full((MLA_KV_LORA, MLA_HEADS * MLA_V))],
        out_specs=[pair(MLA_HEADS), pair(MLA_HEADS // 2)],
        out_shape=[jax.ShapeDtypeStruct((B, MLA_HEADS, T, LANES), BF16),
                   jax.ShapeDtypeStruct((B, MLA_HEADS // 2, T, LANES), BF16)],
        compiler_params=_cparams(2), name="kv_up",
    )(ckv, kpe_pad, lw["wke"], lw["wv"])


def _mla_last_block(i, *, q0, tq, tk, nk):
    last_q = q0 + i * tq + (tq - 1)
    last_key = (last_q // CHUNK) * CHUNK + (CHUNK - 1)
    return jnp.minimum(last_key // tk, nk - 1)


def _mla_body(q_ref, k_ref, v_ref, o_ref, m_ref, l_ref, acc_ref, *, q0, tq, tk, nk, kv_len):
    i = pl.program_id(1)

    def step(j, masked, first_block):
        lane = lax.broadcasted_iota(jnp.int32, (tq, LANES), 1)
        first = lane < HALF
        rows = pl.ds(pl.multiple_of(j * tk, tk), tk)
        if masked:
            qpos = q0 + i * tq + lax.broadcasted_iota(jnp.int32, (tq, 1), 0)
            limit = jnp.minimum((jnp.right_shift(qpos, 6) + 1) * CHUNK, kv_len)
            kpos = j * tk + lax.broadcasted_iota(jnp.int32, (tq, tk), 1)
            bias = jnp.where(kpos < limit, 0.0, -jnp.inf).astype(F32)

        def scores(hh):
            s = _dot_nt(q_ref[0, hh], k_ref[0, hh, rows, :])
            return s + bias if masked else s

        s_next = scores(0)
        alphas, pvs = [], []
        for hh in range(MLA_HEADS):
            p, a = divmod(hh, 2)
            s = s_next
            if hh + 1 < MLA_HEADS:
                s_next = scores(hh + 1)
            m_cur = jnp.max(s, axis=1, keepdims=True)
            if first_block:
                m_new = jnp.broadcast_to(m_cur, (tq, LANES))
                e = jnp.exp2((s - m_cur) * _MLA_EXP2_SCALE)
                l_ref[hh] = jnp.broadcast_to(jnp.sum(e, axis=1, keepdims=True), (tq, LANES))
            else:
                m_prev = m_ref[hh]
                m_new = jnp.maximum(m_prev, m_cur)
                e = jnp.exp2((s - _lane_tile(m_new, tk // LANES)) * _MLA_EXP2_SCALE)
                alpha = jnp.exp2((m_prev - m_new) * _MLA_EXP2_SCALE)
                l_ref[hh] = alpha * l_ref[hh] + jnp.sum(e, axis=1, keepdims=True)
                alphas.append(alpha)
            m_ref[hh] = m_new
            pvs.append(_dot(e.astype(BF16), v_ref[0, p, rows, :]))
            if a == 1:
                new = jnp.where(first, pvs[0], pvs[1])
                if first_block:
                    acc_ref[p] = new
                else:
                    acc_ref[p] = acc_ref[p] * jnp.where(first, alphas[0], alphas[1]) + new
                alphas, pvs = [], []

    all_visible = jnp.minimum(((q0 + i * tq) // CHUNK + 1) * CHUNK, kv_len)
    last = _mla_last_block(i, q0=q0, tq=tq, tk=tk, nk=nk)
    n_full = jnp.minimum(all_visible // tk, last + 1)

    @pl.when(n_full > 0)
    def _():
        step(0, False, True)

    @pl.when(n_full == 0)
    def _():
        step(0, True, True)

    def full_step(j, carry):
        step(j, False, False)
        return carry

    def masked_step(j, carry):
        step(j, True, False)
        return carry

    lax.fori_loop(1, n_full, full_step, 0)
    lax.fori_loop(jnp.maximum(n_full, 1), last + 1, masked_step, 0)

    lane = lax.broadcasted_iota(jnp.int32, (tq, LANES), 1)
    for p in range(MLA_HEADS // 2):
        inv = jnp.where(lane < HALF, 1.0 / l_ref[2 * p], 1.0 / l_ref[2 * p + 1])
        o_ref[0, p] = (acc_ref[p] * inv).astype(BF16)


def _mla_attention(qcat, kcat, v, *, q0, kv_len, tq, tk):
    B, H, Tq, _ = qcat.shape
    Tk = kcat.shape[2]
    nq, nk = Tq // tq, Tk // tk
    return pl.pallas_call(
        functools.partial(_mla_body, q0=q0, tq=tq, tk=tk, nk=nk, kv_len=kv_len),
        grid=(B, nq),
        in_specs=[pl.BlockSpec((1, H, tq, LANES), lambda b, i: (b, 0, i, 0)),
                  pl.BlockSpec((1, H, Tk, LANES), lambda b, i: (b, 0, 0, 0)),
                  pl.BlockSpec((1, H // 2, Tk, LANES), lambda b, i: (b, 0, 0, 0))],
        out_specs=pl.BlockSpec((1, H // 2, tq, LANES), lambda b, i: (b, 0, i, 0)),
        out_shape=jax.ShapeDtypeStruct((B, H // 2, Tq, LANES), BF16),
        scratch_shapes=[pltpu.VMEM((H, tq, LANES), F32), pltpu.VMEM((H, tq, LANES), F32),
                        pltpu.VMEM((H // 2, tq, LANES), F32)],
        compiler_params=_cparams(2), name="mla_attn",
    )(qcat, kcat, v)


def _sb_last_block(i, *, q0, tq, tk, nk):
    last_key = q0 + i * tq + (tq - 1) - 1
    return jnp.clip(last_key // tk, 0, nk - 1)


def _sb_body(q_ref, k_ref, v_ref, u_ref, o_ref, run_ref, acc_ref, *, q0, tq, tk, nk):
    i = pl.program_id(1)
    q_first = q0 + i * tq
    last = _sb_last_block(i, q0=q0, tq=tq, tk=tk, nk=nk)
    first_unmasked = jnp.minimum(q_first // tk - 1, last)
    run_ref[...] = jnp.zeros(run_ref.shape, F32)
    acc_ref[...] = jnp.zeros(acc_ref.shape, F32)

    def step(kb, masked):
        first_k = lax.broadcasted_iota(jnp.int32, (tk, LANES), 1) < HALF
        tri = u_ref[...]
        reps = tk // LANES
        rows = pl.ds(pl.multiple_of(kb * tk, tk), tk)
        if masked:
            qpos = q_first + lax.broadcasted_iota(jnp.int32, (tq, 1), 0)
            kpos = kb * tk + lax.broadcasted_iota(jnp.int32, (tq, tk), 1)
            before1 = kpos < qpos
            before = jnp.concatenate([before1, before1], axis=1)

        def scores(p):
            kp = k_ref[0, p, rows, :]
            zero = jnp.zeros_like(kp)
            k2 = jnp.concatenate([jnp.where(first_k, kp, zero), jnp.where(first_k, zero, kp)], axis=0)
            return _dot_nt(q_ref[0, p], k2) * _LOG2E

        npair = SB_HEADS // 2
        zs = [scores(p) for p in range(npair)]
        keeps, betas, betweens = [], [], []
        for p in range(npair):
            z = zs[p]
            nz = -z
            sp = jnp.log(1.0 + jnp.exp2(jnp.minimum(z, nz))) * _LOG2E
            log_keep = jnp.minimum(nz, 0.0) - sp
            betas.append(log_keep + z)
            if masked:
                log_keep = jnp.where(before, log_keep, 0.0)
            keeps.append(log_keep)
            hi = log_keep.astype(BF16)
            lo = (log_keep - hi.astype(F32)).astype(BF16)
            betweens.append(jnp.concatenate(
                [_dot(hi[:, :tk], tri) + _dot(lo[:, :tk], tri), _dot(hi[:, tk:], tri) + _dot(lo[:, tk:], tri)], axis=1))
        for p in range(npair):
            vp = v_ref[0, p, rows, :]
            zero = jnp.zeros_like(vp)
            v2 = jnp.concatenate([jnp.where(first_k, vp, zero), jnp.where(first_k, zero, vp)], axis=0)
            run_a = run_ref[2 * p]
            run_b = run_ref[2 * p + 1]
            run = jnp.concatenate([_lane_tile(run_a, reps), _lane_tile(run_b, reps)], axis=1)
            att = jnp.exp2(betas[p] + betweens[p] + run)
            if masked:
                att = jnp.where(before, att, 0.0)
            run_ref[2 * p] = run_a + jnp.sum(keeps[p][:, :tk], axis=1, keepdims=True)
            run_ref[2 * p + 1] = run_b + jnp.sum(keeps[p][:, tk:], axis=1, keepdims=True)
            acc_ref[p] = acc_ref[p] + _dot(att.astype(BF16), v2)

    def masked_step(t, carry):
        step(last - t, True)
        return carry

    lax.fori_loop(0, last - first_unmasked, masked_step, 0)

    def more(carry):
        kb, live = carry
        return jnp.logical_and(kb >= 0, live > 0)

    def unmasked_step(carry):
        kb, _ = carry
        step(kb, False)
        live = (jnp.max(run_ref[...]) >= _SB_UNDERFLOW_LOG2).astype(jnp.int32)
        return kb - 1, live

    lax.while_loop(more, unmasked_step, (first_unmasked, jnp.int32(1)))
    o_ref[0] = acc_ref[...].astype(BF16)


def _sb_attention(q, k, v, *, q0, tq, tk):
    B, P, Tq, _ = q.shape
    Tk = k.shape[2]
    nq, nk = Tq // tq, Tk // tk
    kw = dict(q0=q0, tq=tq, tk=tk, nk=nk)
    r = lax.broadcasted_iota(jnp.int32, (tk, tk), 0)
    c = lax.broadcasted_iota(jnp.int32, (tk, tk), 1)
    tri = (r > c).astype(BF16)

    kv_spec = pl.BlockSpec((1, P, Tk, LANES), lambda b, i: (b, 0, 0, 0))
    return pl.pallas_call(
        functools.partial(_sb_body, **kw),
        grid=(B, nq),
        in_specs=[pl.BlockSpec((1, P, tq, LANES), lambda b, i: (b, 0, i, 0)), kv_spec, kv_spec, _full((tk, tk))],
        out_specs=pl.BlockSpec((1, P, tq, LANES), lambda b, i: (b, 0, i, 0)),
        out_shape=jax.ShapeDtypeStruct((B, P, Tq, LANES), BF16),
        scratch_shapes=[pltpu.VMEM((2 * P, tq, LANES), F32), pltpu.VMEM((P, tq, LANES), F32)],
        compiler_params=_cparams(2), name="sb_attn",
    )(q, k, v, tri)


def _ret_body(q_ref, k_ref, v_ref, g_ref, s0_ref, dmask_ref, dq_ref, dk_ref, cd_ref, bd_ref, gn_ref,
              y_ref, snew_ref, state_ref, *, tc, lc, nc):
    c = pl.program_id(1)

    @pl.when(c == 0)
    def _():
        state_ref[...] = s0_ref[0]

    lane = lax.broadcasted_iota(jnp.int32, (lc, LANES), 1)
    first = lane < HALF
    for p in range(RET_HEADS // 2):
        for ch in range(tc // lc):
            rows = slice(ch * lc, (ch + 1) * lc)
            q = q_ref[0, p, rows, :]
            k = k_ref[0, p, rows, :]
            v = v_ref[0, p, rows, :]
            state = state_ref[p]
            cross = _dot((q.astype(F32) * dq_ref[p]).astype(BF16), state.astype(BF16))
            zero = jnp.zeros_like(q)
            inner = []
            for a in range(2):
                qa = jnp.where(first, q, zero) if a == 0 else jnp.where(first, zero, q)
                scores = _dot_nt(qa, k) * dmask_ref[2 * p + a]
                inner.append(_dot(scores.astype(BF16), v))
            o = jnp.where(first, inner[0], inner[1]) + cross
            kd = (k.astype(F32) * dk_ref[p]).astype(BF16)
            kv = lax.dot_general(kd, v, (((0,), (0,)), ((), ())), preferred_element_type=F32)
            state_ref[p] = cd_ref[p] * state + bd_ref[...] * kv
            zf = jnp.zeros_like(o)
            mu = jnp.where(first, jnp.sum(jnp.where(first, o, zf), axis=1, keepdims=True),
                           jnp.sum(jnp.where(first, zf, o), axis=1, keepdims=True)) * (1.0 / RET_DV)
            d = o - mu
            d2 = d * d
            var = jnp.where(first, jnp.sum(jnp.where(first, d2, zf), axis=1, keepdims=True),
                            jnp.sum(jnp.where(first, zf, d2), axis=1, keepdims=True)) * (1.0 / RET_DV)
            yn = d * lax.rsqrt(var + EPS) * gn_ref[p]
            gate = g_ref[0, p, rows, :]
            y_ref[0, p, rows, :] = (yn * (gate * jax.nn.sigmoid(gate))).astype(BF16)

    @pl.when(c == nc - 1)
    def _():
        snew_ref[0] = state_ref[...]


def _ret_tables(lc):
    log_gamma = jnp.log(1.0 - 2.0 ** (-5.0 - jnp.arange(RET_HEADS, dtype=F32)))
    idx = jnp.arange(lc, dtype=F32)
    diff = idx[:, None] - idx[None, :]
    dmask = jnp.where(diff[None] >= 0, jnp.exp(jnp.maximum(diff, 0.0)[None] * log_gamma[:, None, None]), 0.0)
    dk = jnp.exp((lc - 1.0 - idx)[:, None] * log_gamma[None, :])
    dq = jnp.exp((idx + 1.0)[:, None] * log_gamma[None, :])
    cd = jnp.exp(lc * log_gamma)

    def lanes(t):
        t = jnp.repeat(t[:, :, None], HALF, axis=2).reshape(t.shape[0], RET_HEADS // 2, LANES)
        return jnp.transpose(t, (1, 0, 2))

    blk = jnp.arange(LANES) // HALF
    bd = (blk[:, None] == blk[None, :]).astype(F32)
    cd_rows = jnp.repeat(cd, HALF).reshape(RET_HEADS // 2, LANES, 1)
    return dmask, lanes(dq), lanes(dk), cd_rows * bd[None], bd


def _state_to_pairs(s):
    B = s.shape[0]
    s = s.astype(F32).reshape(B, RET_HEADS // 2, 2, RET_DK, RET_DV)
    z = jnp.zeros_like(s[:, :, 0])
    top = jnp.concatenate([s[:, :, 0], z], axis=-1)
    bot = jnp.concatenate([z, s[:, :, 1]], axis=-1)
    return jnp.concatenate([top, bot], axis=-2)


def _pairs_to_state(sp):
    a = sp[:, :, :HALF, :HALF]
    b = sp[:, :, HALF:, HALF:]
    B = sp.shape[0]
    return jnp.stack([a, b], axis=2).reshape(B, RET_HEADS, RET_DK, RET_DV)


def _retention(rq, rk, rv, rg, s0_pairs, g_norm_pairs, *, lc):
    B, P, T, _ = rq.shape
    tc = _tile(T, TC_RET)
    nc = T // tc
    dmask, dq, dk, cd, bd = _ret_tables(lc)
    pair = pl.BlockSpec((1, P, tc, LANES), lambda b, c: (b, 0, c, 0))
    st = pl.BlockSpec((1, P, LANES, LANES), lambda b, c: (b, 0, 0, 0))
    return pl.pallas_call(
        functools.partial(_ret_body, tc=tc, lc=lc, nc=nc),
        grid=(B, nc),
        in_specs=[pair, pair, pair, pair, st, _full(dmask.shape), _full(dq.shape), _full(dk.shape),
                  _full(cd.shape), _full(bd.shape), _full(g_norm_pairs.shape)],
        out_specs=[pair, st],
        out_shape=[jax.ShapeDtypeStruct((B, P, T, LANES), BF16),
                   jax.ShapeDtypeStruct((B, P, LANES, LANES), F32)],
        scratch_shapes=[pltpu.VMEM((P, LANES, LANES), F32)],
        compiler_params=_cparams(2), name="retention",
    )(rq, rk, rv, rg, s0_pairs, dmask, dq, dk, cd, bd, g_norm_pairs)


def _out_mem_body(x_ref, omla_ref, ret_ref, osb_ref, wout_ref, gq_ref, wq_ref, mk_ref, mv_ref, wo_ref, y_ref):
    mixed = jnp.concatenate([ref[0, p] for ref, n in ((omla_ref, MLA_HEADS // 2), (ret_ref, RET_HEADS // 2),
                                                      (osb_ref, SB_HEADS // 2)) for p in range(n)], axis=1)
    x1 = x_ref[0] + _dot(mixed, wout_ref[...])
    hq = _rms(x1, gq_ref[...]).astype(BF16)
    q = _dot(hq, wq_ref[...]).astype(BF16)
    cols = [slice(hh * MEM_HD, (hh + 1) * MEM_HD) for hh in range(MEM_HEADS)]
    scores = [_dot_nt(q[:, sl], mk_ref[0, :, sl]) * MEM_SCALE for sl in cols]
    probs = []
    for s in scores:
        e = jnp.exp(s - jnp.max(s, axis=1, keepdims=True))
        probs.append((e / jnp.sum(e, axis=1, keepdims=True)).astype(BF16))
    heads = [_dot(pr, mv_ref[0, :, sl]).astype(BF16) for pr, sl in zip(probs, cols)]
    o = jnp.concatenate(heads, axis=1)
    y_ref[0] = x1 + _dot(o, wo_ref[...])


def _out_mem(x, o_mla, ret_y, o_sb, mem_k, mem_v, lw):
    B, T, D = x.shape
    tm = _tile(T, TM_OUT)
    M = mem_k.shape[1]
    mw = MEM_HEADS * MEM_HD
    nat = pl.BlockSpec((1, tm, D), lambda b, t: (b, t, 0))
    pair = lambda n: pl.BlockSpec((1, n, tm, LANES), lambda b, t: (b, 0, t, 0))
    mem = pl.BlockSpec((1, M, mw), lambda b, t: (b, 0, 0))
    return pl.pallas_call(
        _out_mem_body, grid=(B, T // tm),
        in_specs=[nat, pair(MLA_HEADS // 2), pair(RET_HEADS // 2), pair(SB_HEADS // 2), _full((D, D)),
                  _full((1, D)), _full((D, mw)), mem, mem, _full((mw, D))],
        out_specs=nat, out_shape=jax.ShapeDtypeStruct((B, T, D), F32),
        compiler_params=_cparams(2), name="out_mem",
    )(x, o_mla, ret_y, o_sb, lw["w_out"], lw["g_mem_q"], lw["w_mem_q"], mem_k, mem_v, lw["w_mem_o"])


def _mem_kv_body(mem_ref, g_ref, wk_ref, wv_ref, k_ref, v_ref, kb_ref, vb_ref):
    m = _rms(mem_ref[0], g_ref[...]).astype(BF16)
    k = _dot(m, wk_ref[...])
    v = _dot(m, wv_ref[...])
    k_ref[0] = k
    v_ref[0] = v
    kb_ref[0] = k.astype(BF16)
    vb_ref[0] = v.astype(BF16)


def _mem_kv(mem, lw):
    B, M, D = mem.shape
    mw = MEM_HEADS * MEM_HD
    o = pl.BlockSpec((1, M, mw), lambda b: (b, 0, 0))
    return pl.pallas_call(
        _mem_kv_body, grid=(B,),
        in_specs=[pl.BlockSpec((1, M, D), lambda b: (b, 0, 0)), _full((1, D)), _full((D, mw)), _full((D, mw))],
        out_specs=[o, o, o, o],
        out_shape=[jax.ShapeDtypeStruct((B, M, mw), F32), jax.ShapeDtypeStruct((B, M, mw), F32),
                   jax.ShapeDtypeStruct((B, M, mw), BF16), jax.ShapeDtypeStruct((B, M, mw), BF16)],
        compiler_params=_cparams(1), name="mem_kv",
    )(mem, lw["g_mem_kv"], lw["w_mem_k"], lw["w_mem_v"])


_CARRY_ROW = 8 - (CONV_W - 1)


def _ffn_body(x_ref, g_ref, wup_ref, wconv_ref, bconv_ref, wdown_ref, conv0_ref, gfin_ref,
              y_ref, convnew_ref, carry_ref, act_ref, *, tm, nt, final):
    t = pl.program_id(1)

    @pl.when(t == 0)
    def _():
        carry_ref[_CARRY_ROW:8, :] = conv0_ref[0]

    x = x_ref[0]
    h = _rms(x, g_ref[...]).astype(BF16)

    sub = lax.broadcasted_iota(jnp.int32, (8, TF_FFN), 0)

    def delay(u, before_first):
        r = pltpu.roll(u, 1, 0)
        head = jnp.where(sub == 0, before_first, r[0:8, :])
        return jnp.concatenate([head, r[8:, :]], axis=0) if tm > 8 else head

    def conv_part(c0):
        cols = slice(c0, c0 + TF_FFN)
        u = _dot(h, wup_ref[:, cols])
        u1 = delay(u, carry_ref[_CARRY_ROW + 1:_CARRY_ROW + 2, cols])
        u2 = delay(u1, carry_ref[_CARRY_ROW:_CARRY_ROW + 1, cols])
        carry_ref[_CARRY_ROW:8, cols] = u[tm - (CONV_W - 1):, :]
        w = wconv_ref[:, cols]
        c = bconv_ref[:, cols]
        c = c + u2 * w[0:1]
        c = c + u1 * w[1:2]
        return c + u * w[2:3]

    for ci in range(D_FF // TF_FFN):
        c0 = ci * TF_FFN
        a = conv_part(c0)
        b = conv_part(D_FF + c0)
        act_ref[:, c0:c0 + TF_FFN] = (a * jax.nn.sigmoid(a) * b).astype(BF16)

    x3 = x + _dot(act_ref[...], wdown_ref[...])
    y_ref[0] = _rms(x3, gfin_ref[...]) if final else x3

    @pl.when(t == nt - 1)
    def _():
        convnew_ref[0] = carry_ref[_CARRY_ROW:8, :]


def _conv_ffn(x, conv0, lw, g_final, *, final):
    B, T, D = x.shape
    tm = _tile(T, TM_FFN)
    nt = T // tm
    f2 = 2 * D_FF
    nat = pl.BlockSpec((1, tm, D), lambda b, t: (b, t, 0))
    cv = pl.BlockSpec((1, CONV_W - 1, f2), lambda b, t: (b, 0, 0))
    return pl.pallas_call(
        functools.partial(_ffn_body, tm=tm, nt=nt, final=final),
        grid=(B, nt),
        in_specs=[nat, _full((1, D)), _full((D, f2)), _full((CONV_W, f2)), _full((1, f2)), _full((D_FF, D)),
                  cv, _full((1, D))],
        out_specs=[nat, cv],
        out_shape=[jax.ShapeDtypeStruct((B, T, D), F32), jax.ShapeDtypeStruct((B, CONV_W - 1, f2), F32)],
        scratch_shapes=[pltpu.VMEM((8, f2), F32), pltpu.VMEM((tm, D_FF), BF16)],
        compiler_params=_cparams(2), name="conv_ffn",
    )(x, lw["g_ffn"], lw["w_ffn_up"], lw["w_ffn_conv"], lw["b_ffn_conv"], lw["w_ffn_down"], conv0, g_final)


def _pad_cols(w, width):
    return jnp.pad(w, ((0, 0), (0, width - w.shape[1])))


def _prep_layer(l, g_mix, w_in, g_q_lora, w_q_up, g_kv_lora, w_kv_up, g_ret_norm, w_out,
                g_mem_q, g_mem_kv, w_mem_q, w_mem_k, w_mem_v, w_mem_o,
                g_ffn, w_ffn_up, w_ffn_conv, b_ffn_conv, w_ffn_down):
    w = w_in[l]
    parts, c0 = [], 0
    for n in IN_SIZES:
        parts.append(w[:, c0:c0 + n])
        c0 += n
    cq, ckv, kpe, rq, rk, rv, rg, sq, sk, sv = parts
    w1 = jnp.concatenate([cq, ckv, _pad_cols(kpe, LANES), rq, rk, rv, rg, sq, sk, sv], axis=1).astype(BF16)
    kq = w_q_up.shape[1]
    wq3 = w_q_up[l].reshape(kq, MLA_HEADS, MLA_NOPE + MLA_ROPE)
    tail = LANES - MLA_NOPE - MLA_ROPE
    q_pad = jnp.pad(wq3, ((0, 0), (0, 0), (0, tail))).reshape(kq, MLA_HEADS * LANES)
    wkv3 = w_kv_up[l].reshape(MLA_KV_LORA, MLA_HEADS, MLA_NOPE + MLA_V)
    wk = jnp.pad(wkv3[:, :, :MLA_NOPE], ((0, 0), (0, 0), (0, LANES - MLA_NOPE))).reshape(MLA_KV_LORA, MLA_HEADS * LANES)
    wv = wkv3[:, :, MLA_NOPE:].reshape(MLA_KV_LORA, MLA_HEADS * MLA_V)
    r = jnp.arange(LANES)[:, None]
    c = jnp.arange(MLA_HEADS * LANES)[None, :]
    we = ((r < MLA_ROPE) & (c % LANES == r + MLA_NOPE)).astype(BF16)
    row = lambda g: g[l].reshape(1, -1).astype(F32)
    return dict(
        g_mix=row(g_mix), w1=w1, g_q=row(g_q_lora), wq=q_pad.astype(BF16),
        g_kv=row(g_kv_lora), wke=jnp.concatenate([wk.astype(BF16), we], axis=0), wv=wv.astype(BF16),
        g_ret=g_ret_norm[l].astype(F32).reshape(RET_HEADS // 2, 1, LANES),
        w_out=w_out[l].astype(BF16), g_mem_q=row(g_mem_q), g_mem_kv=row(g_mem_kv),
        w_mem_q=w_mem_q[l].astype(BF16), w_mem_k=w_mem_k[l].astype(BF16), w_mem_v=w_mem_v[l].astype(BF16),
        w_mem_o=w_mem_o[l].astype(BF16), g_ffn=row(g_ffn), w_ffn_up=w_ffn_up[l].astype(BF16),
        w_ffn_conv=w_ffn_conv[l].astype(F32), b_ffn_conv=row(b_ffn_conv), w_ffn_down=w_ffn_down[l].astype(BF16))


def _rope_tables(pos):
    t = pos.shape[0]
    posf = pos.astype(F32)[:, None]

    def cos_sin(d):
        inv = ROPE_BASE ** (-jnp.arange(0, d, 2, dtype=F32) / d)
        ang = posf * inv[None, :]
        return jnp.cos(ang), jnp.sin(ang)

    c16, s16 = cos_sin(MLA_ROPE)
    c32, s32 = cos_sin(RET_DK)
    z = lambda n: jnp.zeros((t, n), F32)
    tail = LANES - MLA_NOPE - MLA_ROPE
    return dict(
        cslot=jnp.concatenate([jnp.ones((t, MLA_NOPE), F32), c16, c16, z(tail)], axis=1),
        slo_slot=jnp.concatenate([z(MLA_NOPE), -s16, z(MLA_ROPE // 2 + tail)], axis=1),
        shi_slot=jnp.concatenate([z(MLA_NOPE + MLA_ROPE // 2), s16, z(tail)], axis=1),
        ck=jnp.concatenate([c16, c16, z(LANES - MLA_ROPE)], axis=1),
        slo_k=jnp.concatenate([-s16, z(LANES - MLA_ROPE // 2)], axis=1),
        shi_k=jnp.concatenate([z(MLA_ROPE // 2), s16, z(LANES - MLA_ROPE)], axis=1),
        cr=jnp.tile(jnp.concatenate([c32, c32], axis=1), (1, LANES // RET_DK)),
        slo_r=jnp.tile(jnp.concatenate([-s32, z(RET_DK // 2)], axis=1), (1, LANES // RET_DK)),
        shi_r=jnp.tile(jnp.concatenate([z(RET_DK // 2), s32], axis=1), (1, LANES // RET_DK)))


def _pad_rows(a, axis, n):
    pad = [(0, 0)] * a.ndim
    pad[axis] = (0, n - a.shape[axis])
    return jnp.pad(a, pad)


def _to_pairs(a):
    B, T, _ = a.shape
    return jnp.transpose(a.reshape(B, T, 2, LANES), (0, 2, 1, 3))


def _layer(x, tabs, lw, mem_k, mem_v, past, g_final, *, q0, layer, depth, prev):
    B, T, _ = x.shape
    final = layer == depth - 1
    outs = _in_proj(x, tabs, lw, fuse_kv=past is None, layer=layer, depth=depth, prev=prev)
    (ckv_st, kpe_st, kpe_b, qcat, rq, rk, rv, rg, sq, sk_st, sv_st, sk_b, sv_b) = outs[:13]
    if past is None:
        kcat, vmla = outs[13:]
        sk_all, sv_all = sk_b, sv_b
        s0 = jnp.zeros((B, RET_HEADS // 2, LANES, LANES), F32)
        conv0 = jnp.zeros((B, CONV_W - 1, 2 * D_FF), F32)
        kv_len = T
        lc = _tile(T, L_RET)
        tq_mla, tk_mla = _tile(T, TQ_MLA), _tile(T, TK_MLA)
        tq_sb, tk_sb = _tile(T, TQ_SB), _tile(T, TK_SB)
    else:
        ckv_c, kpe_c, s0, sk_c, sv_c, conv0 = past
        P = ckv_c.shape[1]
        kv_len = P + T
        tk_sb = TK_SB
        tk_pad = -(-kv_len // tk_sb) * tk_sb
        ckv_all = _pad_rows(jnp.concatenate([ckv_c.astype(F32), ckv_st[layer]], axis=1), 1, tk_pad)
        kpe_cb = _pad_cols(kpe_c.reshape(B * P, MLA_ROPE), LANES).reshape(B, P, LANES).astype(BF16)
        kpe_all = _pad_rows(jnp.concatenate([kpe_cb, kpe_b], axis=1), 1, tk_pad)
        sk_all = _pad_rows(jnp.concatenate([_to_pairs(sk_c.reshape(B, P, -1).astype(BF16)), sk_b], axis=2), 2, tk_pad)
        sv_all = _pad_rows(jnp.concatenate([_to_pairs(sv_c.reshape(B, P, -1).astype(BF16)), sv_b], axis=2), 2, tk_pad)
        s0 = _state_to_pairs(s0)
        conv0 = conv0.astype(F32)
        lc = T
        tq_mla, tk_mla = T, tk_pad
        tq_sb = T
        kcat, vmla = _kv_up(ckv_all, kpe_all, lw)
    o_mla = _mla_attention(qcat, kcat, vmla, q0=q0, kv_len=kv_len, tq=tq_mla, tk=tk_mla)
    o_sb = _sb_attention(sq, sk_all, sv_all, q0=q0, tq=tq_sb, tk=tk_sb)
    ret_y, s_new = _retention(rq, rk, rv, rg, s0, lw["g_ret"], lc=lc)
    x = _out_mem(x, o_mla, ret_y, o_sb, mem_k, mem_v, lw)
    x, conv_new = _conv_ffn(x, conv0, lw, g_final, final=final)
    return x, (ckv_st, kpe_st, sk_st, sv_st), (_pairs_to_state(s_new), conv_new)


def kernel(x_prompt, x_sample, cache_mla_ckv, cache_mla_kpe, state_ret, cache_sb_k, cache_sb_v, cache_mem_k, cache_mem_v, state_ffn_conv, mem_prompt, g_mix, w_in, g_q_lora, w_q_up, g_kv_lora, w_kv_up, g_ret_norm, w_out, g_mem_q, g_mem_kv, w_mem_q, w_mem_k, w_mem_v, w_mem_o, g_ffn, w_ffn_up, w_ffn_conv, b_ffn_conv, w_ffn_down, g_final):
    depth = w_in.shape[0]
    layers = [_prep_layer(l, g_mix, w_in, g_q_lora, w_q_up, g_kv_lora, w_kv_up, g_ret_norm, w_out,
                          g_mem_q, g_mem_kv, w_mem_q, w_mem_k, w_mem_v, w_mem_o,
                          g_ffn, w_ffn_up, w_ffn_conv, b_ffn_conv, w_ffn_down) for l in range(depth)]
    gfin = g_final.reshape(1, -1).astype(F32)
    mw = MEM_HEADS * MEM_HD

    xp = x_prompt
    Bp, Tp, _ = xp.shape
    tabs_p = _rope_tables(jnp.arange(Tp))
    p_states, p_mem_k, p_mem_v = [], [], []
    p_caches = None
    for l in range(depth):
        mk, mv, mk_b, mv_b = _mem_kv(mem_prompt, layers[l])
        xp, p_caches, st = _layer(xp, tabs_p, layers[l], mk_b, mv_b, None, gfin, q0=0, layer=l, depth=depth,
                                  prev=p_caches)
        p_states.append(st)
        p_mem_k.append(mk.reshape(Bp, -1, MEM_HEADS, MEM_HD))
        p_mem_v.append(mv.reshape(Bp, -1, MEM_HEADS, MEM_HD))

    xs = x_sample
    Bs, Ts, _ = xs.shape
    past_len = cache_mla_ckv.shape[2]
    tabs_s = _rope_tables(past_len + jnp.arange(Ts))
    s_states = []
    s_caches = None
    for l in range(depth):
        past = (cache_mla_ckv[l], cache_mla_kpe[l], state_ret[l], cache_sb_k[l], cache_sb_v[l], state_ffn_conv[l])
        mk_b = cache_mem_k[l].reshape(Bs, -1, mw).astype(BF16)
        mv_b = cache_mem_v[l].reshape(Bs, -1, mw).astype(BF16)
        xs, s_caches, st = _layer(xs, tabs_s, layers[l], mk_b, mv_b, past, gfin, q0=past_len, layer=l, depth=depth,
                                  prev=s_caches)
        s_states.append(st)

    def leaves(caches, states, b, t):
        ckv, kpe, sk, sv = caches
        ret, conv = [jnp.stack(f) for f in zip(*states)]
        heads = (depth, b, t, SB_HEADS, SB_HD)
        return ckv, kpe, ret, sk.reshape(heads), sv.reshape(heads), conv

    p_ckv, p_kpe, p_ret, p_sbk, p_sbv, p_conv = leaves(p_caches, p_states, Bp, Tp)
    s_ckv, s_kpe, s_ret, s_sbk, s_sbv, s_conv = leaves(s_caches, s_states, Bs, Ts)
    return (xp, xs, p_ckv, p_kpe, p_ret, p_sbk, p_sbv, jnp.stack(p_mem_k), jnp.stack(p_mem_v), p_conv,
            s_ckv, s_kpe, s_ret, s_sbk, s_sbv, s_conv)
```

```python
import functools

import jax
import jax.numpy as jnp
from jax import lax
from jax.experimental import pallas as pl
from jax.experimental.pallas import tpu as pltpu

F32 = jnp.float32
BF16 = jnp.bfloat16

CHUNK = 64
EPS = 1e-6
ROPE_BASE = 10000.0

MLA_HEADS = 8
MLA_NOPE = 64
MLA_ROPE = 32
MLA_V = 64
MLA_Q_LORA = 256
MLA_KV_LORA = 128
MLA_SCALE = (MLA_NOPE + MLA_ROPE) ** -0.5
_LOG2E = 1.4426950408889634
_MLA_EXP2_SCALE = MLA_SCALE * _LOG2E
_SB_UNDERFLOW_LOG2 = -160.0
RET_HEADS = 4
RET_DK = 64
RET_DV = 64
RET_SCALE = RET_DK ** -0.5
SB_HEADS = 4
SB_HD = 64
SB_SCALE = SB_HD ** -0.5
MEM_HEADS = 4
MEM_HD = 128
MEM_SCALE = MEM_HD ** -0.5
D_FF = 2816
CONV_W = 3

IN_SIZES = (MLA_Q_LORA, MLA_KV_LORA, MLA_ROPE, RET_HEADS * RET_DK, RET_HEADS * RET_DK,
            RET_HEADS * RET_DV, RET_HEADS * RET_DV, SB_HEADS * SB_HD, SB_HEADS * SB_HD, SB_HEADS * SB_HD)

LANES = 128
HALF = LANES // 2
VMEM_LIMIT = 56 * 1024 * 1024

TM_PROJ = 1024
TM_KVUP = 2304
TQ_MLA = 512
TK_MLA = 512
TQ_SB = 256
TK_SB = 256
TC_RET = 512
L_RET = 256
TM_OUT = 1024
TM_FFN = 512
TF_FFN = 256


def _tile(n, target):
    t = min(n, target)
    while n % t:
        t -= 1
    return t


def _cparams(n_axes):
    return pltpu.CompilerParams(dimension_semantics=("arbitrary",) * n_axes, vmem_limit_bytes=VMEM_LIMIT)


def _rms(x, g):
    return x * lax.rsqrt(jnp.mean(x * x, axis=-1, keepdims=True) + EPS) * g


def _lane_tile(x, reps):
    return jnp.concatenate([x] * reps, axis=1) if reps > 1 else x


def _full(shape):
    n = len(shape)
    return pl.BlockSpec(shape, lambda *_: (0,) * n)


def _dot(a, b):
    return jnp.dot(a, b, preferred_element_type=F32)


def _dot_nt(a, b):
    return lax.dot_general(a, b, (((1,), (1,)), ((), ())), preferred_element_type=F32)


_C_CQ = 0
_C_CKV = 256
_C_KPE = 384
_C_RQ = 512
_C_RK = 768
_C_RV = 1024
_C_RG = 1280
_C_SQ = 1536
_C_SK = 1792
_C_SV = 2048
_W1_COLS = 2304


def _rope(x, cos, sin_lo, sin_hi, half):
    return x * cos + pltpu.roll(x, half, 1) * sin_hi + pltpu.roll(x, LANES - half, 1) * sin_lo


def _kv_slots(c, kp, wke_ref, wv_ref, kcat_ref, v_ref):
    ck = jnp.concatenate([c, kp], axis=1)
    for p in range(MLA_HEADS // 2):
        k2 = _dot(ck, wke_ref[:, 2 * LANES * p:2 * LANES * (p + 1)])
        kcat_ref[0, 2 * p] = k2[:, :LANES].astype(BF16)
        kcat_ref[0, 2 * p + 1] = k2[:, LANES:].astype(BF16)
    for p in range(MLA_HEADS // 4):
        v2 = _dot(c, wv_ref[:, 2 * LANES * p:2 * LANES * (p + 1)])
        v_ref[0, 2 * p] = v2[:, :LANES].astype(BF16)
        v_ref[0, 2 * p + 1] = v2[:, LANES:].astype(BF16)


_IN_PROJ_INPUTS = 17


def _in_proj_body(*refs, n_prev):
    (x_ref, gmix_ref, w1_ref, gq_ref, wq_ref, gkv_ref, wke_ref, wv_ref,
     cslot_ref, slo_slot_ref, shi_slot_ref, ck_ref, slo_k_ref, shi_k_ref, cr_ref, slo_r_ref,
     shi_r_ref) = refs[:_IN_PROJ_INPUTS]
    outs = refs[_IN_PROJ_INPUTS + n_prev:]
    (ckv_ref, kpe_ref, kpeb_ref, qcat_ref, rq_ref, rk_ref, rv_ref, rg_ref,
     sq_ref, sk_ref, sv_ref, skb_ref, svb_ref) = outs[:13]
    kv_refs = outs[13:]
    h = _rms(x_ref[0], gmix_ref[...]).astype(BF16)

    def proj(c0, width):
        return _dot(h, w1_ref[:, c0:c0 + width])

    cq = proj(_C_CQ, 256)
    ckv_raw = proj(_C_CKV, 128)
    kpe_raw = proj(_C_KPE, 128)
    rq = proj(_C_RQ, 256)
    rk = proj(_C_RK, 256)
    rv = proj(_C_RV, 256)
    rg = proj(_C_RG, 256)
    sq = proj(_C_SQ, 256)
    sk = proj(_C_SK, 256)
    sv = proj(_C_SV, 256)

    ckv = _rms(ckv_raw, gkv_ref[...])
    ckv_ref[0, 0] = ckv
    kpe = _rope(kpe_raw, ck_ref[...], slo_k_ref[...], shi_k_ref[...], MLA_ROPE // 2)
    kpe_ref[0, 0] = kpe[:, :MLA_ROPE]
    kpe_b = kpe.astype(BF16)
    kpeb_ref[0] = kpe_b
    cqn = _rms(cq, gq_ref[...]).astype(BF16)
    cslot, slo_slot, shi_slot = cslot_ref[...], slo_slot_ref[...], shi_slot_ref[...]
    for p in range(MLA_HEADS // 2):
        q2 = _dot(cqn, wq_ref[:, 2 * LANES * p:2 * LANES * (p + 1)])
        for a in range(2):
            qs = q2[:, a * LANES:(a + 1) * LANES]
            qcat_ref[0, 2 * p + a] = _rope(qs, cslot, slo_slot, shi_slot, MLA_ROPE // 2).astype(BF16)
    if kv_refs:
        _kv_slots(ckv.astype(BF16), kpe_b, wke_ref, wv_ref, *kv_refs)

    cr, slo_r, shi_r = cr_ref[...], slo_r_ref[...], shi_r_ref[...]
    sk_ref[0, 0] = sk
    sv_ref[0, 0] = sv
    for p in range(2):
        sl = slice(p * LANES, (p + 1) * LANES)
        rq_ref[0, p] = _rope(rq[:, sl], cr, slo_r, shi_r, RET_DK // 2).astype(BF16)
        rk_ref[0, p] = (_rope(rk[:, sl], cr, slo_r, shi_r, RET_DK // 2) * RET_SCALE).astype(BF16)
        rv_ref[0, p] = rv[:, sl].astype(BF16)
        rg_ref[0, p] = rg[:, sl]
        sq_ref[0, p] = (sq[:, sl] * SB_SCALE).astype(BF16)
        skb_ref[0, p] = sk[:, sl].astype(BF16)
        svb_ref[0, p] = sv[:, sl].astype(BF16)


_STACKED_OUTS = (0, 1, 9, 10)


def _in_proj(x, tabs, lw, *, fuse_kv, layer, depth, prev):
    B, T, D = x.shape
    tm = _tile(T, TM_PROJ)
    grid = (B, T // tm)
    row = pl.BlockSpec((tm, LANES), lambda b, t: (t, 0))
    pair = lambda n: pl.BlockSpec((1, n, tm, LANES), lambda b, t: (b, 0, t, 0))
    nat = lambda w: pl.BlockSpec((1, tm, w), lambda b, t: (b, t, 0))
    stacked = lambda w: pl.BlockSpec((1, 1, tm, w), lambda b, t: (layer, b, t, 0))
    in_specs = [nat(D), _full((1, D)), _full((D, _W1_COLS)), _full((1, MLA_Q_LORA)),
                _full((MLA_Q_LORA, MLA_HEADS * LANES)), _full((1, MLA_KV_LORA)),
                _full((MLA_KV_LORA + LANES, MLA_HEADS * LANES)), _full((MLA_KV_LORA, MLA_HEADS * MLA_V))] + [row] * 9
    assert len(in_specs) == _IN_PROJ_INPUTS
    out_shape = [
        jax.ShapeDtypeStruct((depth, B, T, MLA_KV_LORA), F32),
        jax.ShapeDtypeStruct((depth, B, T, MLA_ROPE), F32),
        jax.ShapeDtypeStruct((B, T, LANES), BF16),
        jax.ShapeDtypeStruct((B, MLA_HEADS, T, LANES), BF16),
        jax.ShapeDtypeStruct((B, 2, T, LANES), BF16),
        jax.ShapeDtypeStruct((B, 2, T, LANES), BF16),
        jax.ShapeDtypeStruct((B, 2, T, LANES), BF16),
        jax.ShapeDtypeStruct((B, 2, T, LANES), F32),
        jax.ShapeDtypeStruct((B, 2, T, LANES), BF16),
        jax.ShapeDtypeStruct((depth, B, T, 256), F32),
        jax.ShapeDtypeStruct((depth, B, T, 256), F32),
        jax.ShapeDtypeStruct((B, 2, T, LANES), BF16),
        jax.ShapeDtypeStruct((B, 2, T, LANES), BF16),
    ]
    out_specs = [stacked(MLA_KV_LORA), stacked(MLA_ROPE), nat(LANES), pair(MLA_HEADS), pair(2), pair(2), pair(2),
                 pair(2), pair(2), stacked(256), stacked(256), pair(2), pair(2)]
    if fuse_kv:
        out_shape += [jax.ShapeDtypeStruct((B, MLA_HEADS, T, LANES), BF16),
                      jax.ShapeDtypeStruct((B, MLA_HEADS // 2, T, LANES), BF16)]
        out_specs += [pair(MLA_HEADS), pair(MLA_HEADS // 2)]
    prev = () if prev is None else tuple(prev)
    aliases = {_IN_PROJ_INPUTS + n: o for n, o in enumerate(_STACKED_OUTS[:len(prev)])}
    return pl.pallas_call(
        functools.partial(_in_proj_body, n_prev=len(prev)), grid=grid,
        in_specs=in_specs + [pl.BlockSpec(memory_space=pl.ANY)] * len(prev),
        out_specs=out_specs, out_shape=out_shape, input_output_aliases=aliases,
        compiler_params=_cparams(2), name="in_proj",
    )(x, lw["g_mix"], lw["w1"], lw["g_q"], lw["wq"], lw["g_kv"], lw["wke"], lw["wv"],
      tabs["cslot"], tabs["slo_slot"], tabs["shi_slot"], tabs["ck"], tabs["slo_k"], tabs["shi_k"],
      tabs["cr"], tabs["slo_r"], tabs["shi_r"], *prev)


def _kv_up_body(ckv_ref, kpe_ref, wke_ref, wv_ref, kcat_ref, v_ref):
    _kv_slots(ckv_ref[0].astype(BF16), kpe_ref[0], wke_ref, wv_ref, kcat_ref, v_ref)


def _kv_up(ckv, kpe_pad, lw):
    B, T, _ = ckv.shape
    tm = _tile(T, TM_KVUP)
    nat = lambda w: pl.BlockSpec((1, tm, w), lambda b, t: (b, t, 0))
    pair = lambda n: pl.BlockSpec((1, n, tm, LANES), lambda b, t: (b, 0, t, 0))
    return pl.pallas_call(
        _kv_up_body, grid=(B, T // tm),
        in_specs=[nat(MLA_KV_LORA), nat(LANES), _full((MLA_KV_LORA + LANES, MLA_HEADS * LANES)),
                  _full((MLA_KV_LORA, MLA_HEADS * MLA_V))],
        out_specs=[pair(MLA_HEADS), pair(MLA_HEADS // 2)],
        out_shape=[jax.ShapeDtypeStruct((B, MLA_HEADS, T, LANES), BF16),
                   jax.ShapeDtypeStruct((B, MLA_HEADS // 2, T, LANES), BF16)],
        compiler_params=_cparams(2), name="kv_up",
    )(ckv, kpe_pad, lw["wke"], lw["wv"])


def _mla_last_block(i, *, q0, tq, tk, nk):
    last_q = q0 + i * tq + (tq - 1)
    last_key = (last_q // CHUNK) * CHUNK + (CHUNK - 1)
    return jnp.minimum(last_key // tk, nk - 1)


def _mla_body(q_ref, k_ref, v_ref, o_ref, m_ref, l_ref, acc_ref, *, q0, tq, tk, nk, kv_len):
    i = pl.program_id(1)

    def step(j, masked, first_block):
        lane = lax.broadcasted_iota(jnp.int32, (tq, LANES), 1)
        first = lane < HALF
        rows = pl.ds(pl.multiple_of(j * tk, tk), tk)
        if masked:
            qpos = q0 + i * tq + lax.broadcasted_iota(jnp.int32, (tq, 1), 0)
            limit = jnp.minimum((jnp.right_shift(qpos, 6) + 1) * CHUNK, kv_len)
            kpos = j * tk + lax.broadcasted_iota(jnp.int32, (tq, tk), 1)
            bias = jnp.where(kpos < limit, 0.0, -jnp.inf).astype(F32)

        def scores(hh):
            s = _dot_nt(q_ref[0, hh], k_ref[0, hh, rows, :])
            return s + bias if masked else s

        s_next = scores(0)
        alphas, pvs = [], []
        for hh in range(MLA_HEADS):
            p, a = divmod(hh, 2)
            s = s_next
            if hh + 1 < MLA_HEADS:
                s_next = scores(hh + 1)
            m_cur = jnp.max(s, axis=1, keepdims=True)
            if first_block:
                m_new = jnp.broadcast_to(m_cur, (tq, LANES))
                e = jnp.exp2((s - m_cur) * _MLA_EXP2_SCALE)
                l_ref[hh] = jnp.broadcast_to(jnp.sum(e, axis=1, keepdims=True), (tq, LANES))
            else:
                m_prev = m_ref[hh]
                m_new = jnp.maximum(m_prev, m_cur)
                e = jnp.exp2((s - _lane_tile(m_new, tk // LANES)) * _MLA_EXP2_SCALE)
                alpha = jnp.exp2((m_prev - m_new) * _MLA_EXP2_SCALE)
                l_ref[hh] = alpha * l_ref[hh] + jnp.sum(e, axis=1, keepdims=True)
                alphas.append(alpha)
            m_ref[hh] = m_new
            pvs.append(_dot(e.astype(BF16), v_ref[0, p, rows, :]))
            if a == 1:
                new = jnp.where(first, pvs[0], pvs[1])
                if first_block:
                    acc_ref[p] = new
                else:
                    acc_ref[p] = acc_ref[p] * jnp.where(first, alphas[0], alphas[1]) + new
                alphas, pvs = [], []

    all_visible = jnp.minimum(((q0 + i * tq) // CHUNK + 1) * CHUNK, kv_len)
    last = _mla_last_block(i, q0=q0, tq=tq, tk=tk, nk=nk)
    n_full = jnp.minimum(all_visible // tk, last + 1)

    @pl.when(n_full > 0)
    def _():
        step(0, False, True)

    @pl.when(n_full == 0)
    def _():
        step(0, True, True)

    def full_step(j, carry):
        step(j, False, False)
        return carry

    def masked_step(j, carry):
        step(j, True, False)
        return carry

    lax.fori_loop(1, n_full, full_step, 0)
    lax.fori_loop(jnp.maximum(n_full, 1), last + 1, masked_step, 0)

    lane = lax.broadcasted_iota(jnp.int32, (tq, LANES), 1)
    for p in range(MLA_HEADS // 2):
        inv = jnp.where(lane < HALF, 1.0 / l_ref[2 * p], 1.0 / l_ref[2 * p + 1])
        o_ref[0, p] = (acc_ref[p] * inv).astype(BF16)


def _mla_attention(qcat, kcat, v, *, q0, kv_len, tq, tk):
    B, H, Tq, _ = qcat.shape
    Tk = kcat.shape[2]
    nq, nk = Tq // tq, Tk // tk
    return pl.pallas_call(
        functools.partial(_mla_body, q0=q0, tq=tq, tk=tk, nk=nk, kv_len=kv_len),
        grid=(B, nq),
        in_specs=[pl.BlockSpec((1, H, tq, LANES), lambda b, i: (b, 0, i, 0)),
                  pl.BlockSpec((1, H, Tk, LANES), lambda b, i: (b, 0, 0, 0)),
                  pl.BlockSpec((1, H // 2, Tk, LANES), lambda b, i: (b, 0, 0, 0))],
        out_specs=pl.BlockSpec((1, H // 2, tq, LANES), lambda b, i: (b, 0, i, 0)),
        out_shape=jax.ShapeDtypeStruct((B, H // 2, Tq, LANES), BF16),
        scratch_shapes=[pltpu.VMEM((H, tq, LANES), F32), pltpu.VMEM((H, tq, LANES), F32),
                        pltpu.VMEM((H // 2, tq, LANES), F32)],
        compiler_params=_cparams(2), name="mla_attn",
    )(qcat, kcat, v)


def _sb_last_block(i, *, q0, tq, tk, nk):
    last_key = q0 + i * tq + (tq - 1) - 1
    return jnp.clip(last_key // tk, 0, nk - 1)


def _sb_body(q_ref, k_ref, v_ref, u_ref, o_ref, run_ref, acc_ref, *, q0, tq, tk, nk):
    i = pl.program_id(1)
    q_first = q0 + i * tq
    last = _sb_last_block(i, q0=q0, tq=tq, tk=tk, nk=nk)
    first_unmasked = jnp.minimum(q_first // tk - 1, last)

    def step(kb, masked, first_block=False):
        first_k = lax.broadcasted_iota(jnp.int32, (tk, LANES), 1) < HALF
        tri = u_ref[...]
        reps = tk // LANES
        rows = pl.ds(pl.multiple_of(kb * tk, tk), tk)
        if masked:
            qpos = q_first + lax.broadcasted_iota(jnp.int32, (tq, 1), 0)
            kpos = kb * tk + lax.broadcasted_iota(jnp.int32, (tq, tk), 1)
            before1 = kpos < qpos
            before = jnp.concatenate([before1, before1], axis=1)

        def scores(p):
            kp = k_ref[0, p, rows, :]
            zero = jnp.zeros_like(kp)
            k2 = jnp.concatenate([jnp.where(first_k, kp, zero), jnp.where(first_k, zero, kp)], axis=0)
            return _dot_nt(q_ref[0, p], k2) * _LOG2E

        npair = SB_HEADS // 2
        zs = [scores(p) for p in range(npair)]
        keeps, betas, betweens = [], [], []
        for p in range(npair):
            z = zs[p]
            nz = -z
            sp = jnp.log(1.0 + jnp.exp2(jnp.minimum(z, nz))) * _LOG2E
            log_keep = jnp.minimum(nz, 0.0) - sp
            betas.append(log_keep + z)
            if masked:
                log_keep = jnp.where(before, log_keep, 0.0)
            keeps.append(log_keep)
            hi = log_keep.astype(BF16)
            lo = (log_keep - hi.astype(F32)).astype(BF16)
            betweens.append(jnp.concatenate(
                [_dot(hi[:, :tk], tri) + _dot(lo[:, :tk], tri), _dot(hi[:, tk:], tri) + _dot(lo[:, tk:], tri)], axis=1))
        for p in range(npair):
            vp = v_ref[0, p, rows, :]
            zero = jnp.zeros_like(vp)
            v2 = jnp.concatenate([jnp.where(first_k, vp, zero), jnp.where(first_k, zero, vp)], axis=0)
            sum_a = jnp.sum(keeps[p][:, :tk], axis=1, keepdims=True)
            sum_b = jnp.sum(keeps[p][:, tk:], axis=1, keepdims=True)
            if first_block:
                att = jnp.exp2(betas[p] + betweens[p])
                run_ref[2 * p] = jnp.broadcast_to(sum_a, (tq, LANES))
                run_ref[2 * p + 1] = jnp.broadcast_to(sum_b, (tq, LANES))
            else:
                run_a = run_ref[2 * p]
                run_b = run_ref[2 * p + 1]
                run = jnp.concatenate([_lane_tile(run_a, reps), _lane_tile(run_b, reps)], axis=1)
                att = jnp.exp2(betas[p] + betweens[p] + run)
                run_ref[2 * p] = run_a + sum_a
                run_ref[2 * p + 1] = run_b + sum_b
            if masked:
                att = jnp.where(before, att, 0.0)
            pv = _dot(att.astype(BF16), v2)
            acc_ref[p] = pv if first_block else acc_ref[p] + pv

    step(last, True, first_block=True)

    def masked_step(t, carry):
        step(last - t, True)
        return carry

    lax.fori_loop(1, last - first_unmasked, masked_step, 0)

    def more(carry):
        kb, live = carry
        return jnp.logical_and(kb >= 0, live > 0)

    def unmasked_step(carry):
        kb, _ = carry
        step(kb, False)
        live = (jnp.max(run_ref[...]) >= _SB_UNDERFLOW_LOG2).astype(jnp.int32)
        return kb - 1, live

    lax.while_loop(more, unmasked_step, (jnp.minimum(first_unmasked, last - 1), jnp.int32(1)))
    o_ref[0] = acc_ref[...].astype(BF16)


def _sb_attention(q, k, v, *, q0, tq, tk):
    B, P, Tq, _ = q.shape
    Tk = k.shape[2]
    nq, nk = Tq // tq, Tk // tk
    kw = dict(q0=q0, tq=tq, tk=tk, nk=nk)
    r = lax.broadcasted_iota(jnp.int32, (tk, tk), 0)
    c = lax.broadcasted_iota(jnp.int32, (tk, tk), 1)
    tri = (r > c).astype(BF16)

    kv_spec = pl.BlockSpec((1, P, Tk, LANES), lambda b, i: (b, 0, 0, 0))
    return pl.pallas_call(
        functools.partial(_sb_body, **kw),
        grid=(B, nq),
        in_specs=[pl.BlockSpec((1, P, tq, LANES), lambda b, i: (b, 0, i, 0)), kv_spec, kv_spec, _full((tk, tk))],
        out_specs=pl.BlockSpec((1, P, tq, LANES), lambda b, i: (b, 0, i, 0)),
        out_shape=jax.ShapeDtypeStruct((B, P, Tq, LANES), BF16),
        scratch_shapes=[pltpu.VMEM((2 * P, tq, LANES), F32), pltpu.VMEM((P, tq, LANES), F32)],
        compiler_params=_cparams(2), name="sb_attn",
    )(q, k, v, tri)


def _ret_body(q_ref, k_ref, v_ref, g_ref, s0_ref, dmask_ref, dq_ref, dk_ref, cd_ref, bd_ref, gn_ref,
              y_ref, snew_ref, state_ref, *, tc, lc, nc):
    c = pl.program_id(1)

    @pl.when(c == 0)
    def _():
        state_ref[...] = s0_ref[0]

    lane = lax.broadcasted_iota(jnp.int32, (lc, LANES), 1)
    first = lane < HALF
    for p in range(RET_HEADS // 2):
        for ch in range(tc // lc):
            rows = slice(ch * lc, (ch + 1) * lc)
            q = q_ref[0, p, rows, :]
            k = k_ref[0, p, rows, :]
            v = v_ref[0, p, rows, :]
            state = state_ref[p]
            cross = _dot((q.astype(F32) * dq_ref[p]).astype(BF16), state.astype(BF16))
            zero = jnp.zeros_like(q)
            inner = []
            for a in range(2):
                qa = jnp.where(first, q, zero) if a == 0 else jnp.where(first, zero, q)
                scores = _dot_nt(qa, k) * dmask_ref[2 * p + a]
                inner.append(_dot(scores.astype(BF16), v))
            o = jnp.where(first, inner[0], inner[1]) + cross
            kd = (k.astype(F32) * dk_ref[p]).astype(BF16)
            kv = lax.dot_general(kd, v, (((0,), (0,)), ((), ())), preferred_element_type=F32)
            state_ref[p] = cd_ref[p] * state + bd_ref[...] * kv
            zf = jnp.zeros_like(o)
            mu = jnp.where(first, jnp.sum(jnp.where(first, o, zf), axis=1, keepdims=True),
                           jnp.sum(jnp.where(first, zf, o), axis=1, keepdims=True)) * (1.0 / RET_DV)
            d = o - mu
            d2 = d * d
            var = jnp.where(first, jnp.sum(jnp.where(first, d2, zf), axis=1, keepdims=True),
                            jnp.sum(jnp.where(first, zf, d2), axis=1, keepdims=True)) * (1.0 / RET_DV)
            yn = d * lax.rsqrt(var + EPS) * gn_ref[p]
            gate = g_ref[0, p, rows, :]
            y_ref[0, p, rows, :] = (yn * (gate * jax.nn.sigmoid(gate))).astype(BF16)

    @pl.when(c == nc - 1)
    def _():
        snew_ref[0] = state_ref[...]


def _ret_tables(lc):
    log_gamma = jnp.log(1.0 - 2.0 ** (-5.0 - jnp.arange(RET_HEADS, dtype=F32)))
    idx = jnp.arange(lc, dtype=F32)
    diff = idx[:, None] - idx[None, :]
    dmask = jnp.where(diff[None] >= 0, jnp.exp(jnp.maximum(diff, 0.0)[None] * log_gamma[:, None, None]), 0.0)
    dk = jnp.exp((lc - 1.0 - idx)[:, None] * log_gamma[None, :])
    dq = jnp.exp((idx + 1.0)[:, None] * log_gamma[None, :])
    cd = jnp.exp(lc * log_gamma)

    def lanes(t):
        t = jnp.repeat(t[:, :, None], HALF, axis=2).reshape(t.shape[0], RET_HEADS // 2, LANES)
        return jnp.transpose(t, (1, 0, 2))

    blk = jnp.arange(LANES) // HALF
    bd = (blk[:, None] == blk[None, :]).astype(F32)
    cd_rows = jnp.repeat(cd, HALF).reshape(RET_HEADS // 2, LANES, 1)
    return dmask, lanes(dq), lanes(dk), cd_rows * bd[None], bd


def _state_to_pairs(s):
    B = s.shape[0]
    s = s.astype(F32).reshape(B, RET_HEADS // 2, 2, RET_DK, RET_DV)
    z = jnp.zeros_like(s[:, :, 0])
    top = jnp.concatenate([s[:, :, 0], z], axis=-1)
    bot = jnp.concatenate([z, s[:, :, 1]], axis=-1)
    return jnp.concatenate([top, bot], axis=-2)


def _pairs_to_state(sp):
    a = sp[:, :, :HALF, :HALF]
    b = sp[:, :, HALF:, HALF:]
    B = sp.shape[0]
    return jnp.stack([a, b], axis=2).reshape(B, RET_HEADS, RET_DK, RET_DV)


def _retention(rq, rk, rv, rg, s0_pairs, g_norm_pairs, *, lc):
    B, P, T, _ = rq.shape
    tc = _tile(T, TC_RET)
    nc = T // tc
    dmask, dq, dk, cd, bd = _ret_tables(lc)
    pair = pl.BlockSpec((1, P, tc, LANES), lambda b, c: (b, 0, c, 0))
    st = pl.BlockSpec((1, P, LANES, LANES), lambda b, c: (b, 0, 0, 0))
    return pl.pallas_call(
        functools.partial(_ret_body, tc=tc, lc=lc, nc=nc),
        grid=(B, nc),
        in_specs=[pair, pair, pair, pair, st, _full(dmask.shape), _full(dq.shape), _full(dk.shape),
                  _full(cd.shape), _full(bd.shape), _full(g_norm_pairs.shape)],
        out_specs=[pair, st],
        out_shape=[jax.ShapeDtypeStruct((B, P, T, LANES), BF16),
                   jax.ShapeDtypeStruct((B, P, LANES, LANES), F32)],
        scratch_shapes=[pltpu.VMEM((P, LANES, LANES), F32)],
        compiler_params=_cparams(2), name="retention",
    )(rq, rk, rv, rg, s0_pairs, dmask, dq, dk, cd, bd, g_norm_pairs)


def _out_mem_body(x_ref, omla_ref, ret_ref, osb_ref, wout_ref, gq_ref, wq_ref, mk_ref, mv_ref, wo_ref, y_ref):
    mixed = jnp.concatenate([ref[0, p] for ref, n in ((omla_ref, MLA_HEADS // 2), (ret_ref, RET_HEADS // 2),
                                                      (osb_ref, SB_HEADS // 2)) for p in range(n)], axis=1)
    x1 = x_ref[0] + _dot(mixed, wout_ref[...])
    hq = _rms(x1, gq_ref[...]).astype(BF16)
    q = _dot(hq, wq_ref[...]).astype(BF16)
    cols = [slice(hh * MEM_HD, (hh + 1) * MEM_HD) for hh in range(MEM_HEADS)]
    scores = [_dot_nt(q[:, sl], mk_ref[0, :, sl]) * MEM_SCALE for sl in cols]
    probs = []
    for s in scores:
        e = jnp.exp(s - jnp.max(s, axis=1, keepdims=True))
        probs.append((e / jnp.sum(e, axis=1, keepdims=True)).astype(BF16))
    heads = [_dot(pr, mv_ref[0, :, sl]).astype(BF16) for pr, sl in zip(probs, cols)]
    o = jnp.concatenate(heads, axis=1)
    y_ref[0] = x1 + _dot(o, wo_ref[...])


def _out_mem(x, o_mla, ret_y, o_sb, mem_k, mem_v, lw):
    B, T, D = x.shape
    tm = _tile(T, TM_OUT)
    M = mem_k.shape[1]
    mw = MEM_HEADS * MEM_HD
    nat = pl.BlockSpec((1, tm, D), lambda b, t: (b, t, 0))
    pair = lambda n: pl.BlockSpec((1, n, tm, LANES), lambda b, t: (b, 0, t, 0))
    mem = pl.BlockSpec((1, M, mw), lambda b, t: (b, 0, 0))
    return pl.pallas_call(
        _out_mem_body, grid=(B, T // tm),
        in_specs=[nat, pair(MLA_HEADS // 2), pair(RET_HEADS // 2), pair(SB_HEADS // 2), _full((D, D)),
                  _full((1, D)), _full((D, mw)), mem, mem, _full((mw, D))],
        out_specs=nat, out_shape=jax.ShapeDtypeStruct((B, T, D), F32),
        compiler_params=_cparams(2), name="out_mem",
    )(x, o_mla, ret_y, o_sb, lw["w_out"], lw["g_mem_q"], lw["w_mem_q"], mem_k, mem_v, lw["w_mem_o"])


def _mem_kv_body(mem_ref, g_ref, wk_ref, wv_ref, k_ref, v_ref, kb_ref, vb_ref):
    m = _rms(mem_ref[0], g_ref[...]).astype(BF16)
    k = _dot(m, wk_ref[...])
    v = _dot(m, wv_ref[...])
    k_ref[0] = k
    v_ref[0] = v
    kb_ref[0] = k.astype(BF16)
    vb_ref[0] = v.astype(BF16)


def _mem_kv(mem, lw):
    B, M, D = mem.shape
    mw = MEM_HEADS * MEM_HD
    o = pl.BlockSpec((1, M, mw), lambda b: (b, 0, 0))
    return pl.pallas_call(
        _mem_kv_body, grid=(B,),
        in_specs=[pl.BlockSpec((1, M, D), lambda b: (b, 0, 0)), _full((1, D)), _full((D, mw)), _full((D, mw))],
        out_specs=[o, o, o, o],
        out_shape=[jax.ShapeDtypeStruct((B, M, mw), F32), jax.ShapeDtypeStruct((B, M, mw), F32),
                   jax.ShapeDtypeStruct((B, M, mw), BF16), jax.ShapeDtypeStruct((B, M, mw), BF16)],
        compiler_params=_cparams(1), name="mem_kv",
    )(mem, lw["g_mem_kv"], lw["w_mem_k"], lw["w_mem_v"])


_CARRY_ROW = 8 - (CONV_W - 1)


def _ffn_body(x_ref, g_ref, wup_ref, wconv_ref, bconv_ref, wdown_ref, conv0_ref, gfin_ref,
              y_ref, convnew_ref, carry_ref, act_ref, *, tm, nt, final):
    t = pl.program_id(1)

    @pl.when(t == 0)
    def _():
        carry_ref[_CARRY_ROW:8, :] = conv0_ref[0]

    x = x_ref[0]
    h = _rms(x, g_ref[...]).astype(BF16)

    sub = lax.broadcasted_iota(jnp.int32, (8, TF_FFN), 0)

    def delay(u, before_first):
        r = pltpu.roll(u, 1, 0)
        head = jnp.where(sub == 0, before_first, r[0:8, :])
        return jnp.concatenate([head, r[8:, :]], axis=0) if tm > 8 else head

    def conv_part(c0):
        cols = slice(c0, c0 + TF_FFN)
        u = _dot(h, wup_ref[:, cols])
        u1 = delay(u, carry_ref[_CARRY_ROW + 1:_CARRY_ROW + 2, cols])
        u2 = delay(u1, carry_ref[_CARRY_ROW:_CARRY_ROW + 1, cols])
        carry_ref[_CARRY_ROW:8, cols] = u[tm - (CONV_W - 1):, :]
        w = wconv_ref[:, cols]
        c = bconv_ref[:, cols]
        c = c + u2 * w[0:1]
        c = c + u1 * w[1:2]
        return c + u * w[2:3]

    for ci in range(D_FF // TF_FFN):
        c0 = ci * TF_FFN
        a = conv_part(c0)
        b = conv_part(D_FF + c0)
        act_ref[:, c0:c0 + TF_FFN] = (a * jax.nn.sigmoid(a) * b).astype(BF16)

    x3 = x + _dot(act_ref[...], wdown_ref[...])
    y_ref[0] = _rms(x3, gfin_ref[...]) if final else x3

    @pl.when(t == nt - 1)
    def _():
        convnew_ref[0] = carry_ref[_CARRY_ROW:8, :]


def _conv_ffn(x, conv0, lw, g_final, *, final):
    B, T, D = x.shape
    tm = _tile(T, TM_FFN)
    nt = T // tm
    f2 = 2 * D_FF
    nat = pl.BlockSpec((1, tm, D), lambda b, t: (b, t, 0))
    cv = pl.BlockSpec((1, CONV_W - 1, f2), lambda b, t: (b, 0, 0))
    return pl.pallas_call(
        functools.partial(_ffn_body, tm=tm, nt=nt, final=final),
        grid=(B, nt),
        in_specs=[nat, _full((1, D)), _full((D, f2)), _full((CONV_W, f2)), _full((1, f2)), _full((D_FF, D)),
                  cv, _full((1, D))],
        out_specs=[nat, cv],
        out_shape=[jax.ShapeDtypeStruct((B, T, D), F32), jax.ShapeDtypeStruct((B, CONV_W - 1, f2), F32)],
        scratch_shapes=[pltpu.VMEM((8, f2), F32), pltpu.VMEM((tm, D_FF), BF16)],
        compiler_params=_cparams(2), name="conv_ffn",
    )(x, lw["g_ffn"], lw["w_ffn_up"], lw["w_ffn_conv"], lw["b_ffn_conv"], lw["w_ffn_down"], conv0, g_final)


def _pad_cols(w, width):
    return jnp.pad(w, ((0, 0), (0, width - w.shape[1])))


def _prep_layer(l, g_mix, w_in, g_q_lora, w_q_up, g_kv_lora, w_kv_up, g_ret_norm, w_out,
                g_mem_q, g_mem_kv, w_mem_q, w_mem_k, w_mem_v, w_mem_o,
                g_ffn, w_ffn_up, w_ffn_conv, b_ffn_conv, w_ffn_down):
    w = w_in[l]
    parts, c0 = [], 0
    for n in IN_SIZES:
        parts.append(w[:, c0:c0 + n])
        c0 += n
    cq, ckv, kpe, rq, rk, rv, rg, sq, sk, sv = parts
    w1 = jnp.concatenate([cq, ckv, _pad_cols(kpe, LANES), rq, rk, rv, rg, sq, sk, sv], axis=1).astype(BF16)
    kq = w_q_up.shape[1]
    wq3 = w_q_up[l].reshape(kq, MLA_HEADS, MLA_NOPE + MLA_ROPE)
    tail = LANES - MLA_NOPE - MLA_ROPE
    q_pad = jnp.pad(wq3, ((0, 0), (0, 0), (0, tail))).reshape(kq, MLA_HEADS * LANES)
    wkv3 = w_kv_up[l].reshape(MLA_KV_LORA, MLA_HEADS, MLA_NOPE + MLA_V)
    wk = jnp.pad(wkv3[:, :, :MLA_NOPE], ((0, 0), (0, 0), (0, LANES - MLA_NOPE))).reshape(MLA_KV_LORA, MLA_HEADS * LANES)
    wv = wkv3[:, :, MLA_NOPE:].reshape(MLA_KV_LORA, MLA_HEADS * MLA_V)
    r = jnp.arange(LANES)[:, None]
    c = jnp.arange(MLA_HEADS * LANES)[None, :]
    we = ((r < MLA_ROPE) & (c % LANES == r + MLA_NOPE)).astype(BF16)
    row = lambda g: g[l].reshape(1, -1).astype(F32)
    return dict(
        g_mix=row(g_mix), w1=w1, g_q=row(g_q_lora), wq=q_pad.astype(BF16),
        g_kv=row(g_kv_lora), wke=jnp.concatenate([wk.astype(BF16), we], axis=0), wv=wv.astype(BF16),
        g_ret=g_ret_norm[l].astype(F32).reshape(RET_HEADS // 2, 1, LANES),
        w_out=w_out[l].astype(BF16), g_mem_q=row(g_mem_q), g_mem_kv=row(g_mem_kv),
        w_mem_q=w_mem_q[l].astype(BF16), w_mem_k=w_mem_k[l].astype(BF16), w_mem_v=w_mem_v[l].astype(BF16),
        w_mem_o=w_mem_o[l].astype(BF16), g_ffn=row(g_ffn), w_ffn_up=w_ffn_up[l].astype(BF16),
        w_ffn_conv=w_ffn_conv[l].astype(F32), b_ffn_conv=row(b_ffn_conv), w_ffn_down=w_ffn_down[l].astype(BF16))


def _rope_tables(pos):
    t = pos.shape[0]
    posf = pos.astype(F32)[:, None]

    def cos_sin(d):
        inv = ROPE_BASE ** (-jnp.arange(0, d, 2, dtype=F32) / d)
        ang = posf * inv[None, :]
        return jnp.cos(ang), jnp.sin(ang)

    c16, s16 = cos_sin(MLA_ROPE)
    c32, s32 = cos_sin(RET_DK)
    z = lambda n: jnp.zeros((t, n), F32)
    tail = LANES - MLA_NOPE - MLA_ROPE
    return dict(
        cslot=jnp.concatenate([jnp.ones((t, MLA_NOPE), F32), c16, c16, z(tail)], axis=1),
        slo_slot=jnp.concatenate([z(MLA_NOPE), -s16, z(MLA_ROPE // 2 + tail)], axis=1),
        shi_slot=jnp.concatenate([z(MLA_NOPE + MLA_ROPE // 2), s16, z(tail)], axis=1),
        ck=jnp.concatenate([c16, c16, z(LANES - MLA_ROPE)], axis=1),
        slo_k=jnp.concatenate([-s16, z(LANES - MLA_ROPE // 2)], axis=1),
        shi_k=jnp.concatenate([z(MLA_ROPE // 2), s16, z(LANES - MLA_ROPE)], axis=1),
        cr=jnp.tile(jnp.concatenate([c32, c32], axis=1), (1, LANES // RET_DK)),
        slo_r=jnp.tile(jnp.concatenate([-s32, z(RET_DK // 2)], axis=1), (1, LANES // RET_DK)),
        shi_r=jnp.tile(jnp.concatenate([z(RET_DK // 2), s32], axis=1), (1, LANES // RET_DK)))


def _pad_rows(a, axis, n):
    pad = [(0, 0)] * a.ndim
    pad[axis] = (0, n - a.shape[axis])
    return jnp.pad(a, pad)


def _to_pairs(a):
    B, T, _ = a.shape
    return jnp.transpose(a.reshape(B, T, 2, LANES), (0, 2, 1, 3))


def _layer(x, tabs, lw, mem_k, mem_v, past, g_final, *, q0, layer, depth, prev):
    B, T, _ = x.shape
    final = layer == depth - 1
    outs = _in_proj(x, tabs, lw, fuse_kv=past is None, layer=layer, depth=depth, prev=prev)
    (ckv_st, kpe_st, kpe_b, qcat, rq, rk, rv, rg, sq, sk_st, sv_st, sk_b, sv_b) = outs[:13]
    if past is None:
        kcat, vmla = outs[13:]
        sk_all, sv_all = sk_b, sv_b
        s0 = jnp.zeros((B, RET_HEADS // 2, LANES, LANES), F32)
        conv0 = jnp.zeros((B, CONV_W - 1, 2 * D_FF), F32)
        kv_len = T
        lc = _tile(T, L_RET)
        tq_mla, tk_mla = _tile(T, TQ_MLA), _tile(T, TK_MLA)
        tq_sb, tk_sb = _tile(T, TQ_SB), _tile(T, TK_SB)
    else:
        ckv_c, kpe_c, s0, sk_c, sv_c, conv0 = past
        P = ckv_c.shape[1]
        kv_len = P + T
        tk_sb = TK_SB
        tk_pad = -(-kv_len // tk_sb) * tk_sb
        ckv_all = _pad_rows(jnp.concatenate([ckv_c.astype(F32), ckv_st[layer]], axis=1), 1, tk_pad)
        kpe_cb = _pad_cols(kpe_c.reshape(B * P, MLA_ROPE), LANES).reshape(B, P, LANES).astype(BF16)
        kpe_all = _pad_rows(jnp.concatenate([kpe_cb, kpe_b], axis=1), 1, tk_pad)
        sk_all = _pad_rows(jnp.concatenate([_to_pairs(sk_c.reshape(B, P, -1).astype(BF16)), sk_b], axis=2), 2, tk_pad)
        sv_all = _pad_rows(jnp.concatenate([_to_pairs(sv_c.reshape(B, P, -1).astype(BF16)), sv_b], axis=2), 2, tk_pad)
        s0 = _state_to_pairs(s0)
        conv0 = conv0.astype(F32)
        lc = T
        tq_mla, tk_mla = T, tk_pad
        tq_sb = T
        kcat, vmla = _kv_up(ckv_all, kpe_all, lw)
    o_mla = _mla_attention(qcat, kcat, vmla, q0=q0, kv_len=kv_len, tq=tq_mla, tk=tk_mla)
    o_sb = _sb_attention(sq, sk_all, sv_all, q0=q0, tq=tq_sb, tk=tk_sb)
    ret_y, s_new = _retention(rq, rk, rv, rg, s0, lw["g_ret"], lc=lc)
    x = _out_mem(x, o_mla, ret_y, o_sb, mem_k, mem_v, lw)
    x, conv_new = _conv_ffn(x, conv0, lw, g_final, final=final)
    return x, (ckv_st, kpe_st, sk_st, sv_st), (_pairs_to_state(s_new), conv_new)


def kernel(x_prompt, x_sample, cache_mla_ckv, cache_mla_kpe, state_ret, cache_sb_k, cache_sb_v, cache_mem_k, cache_mem_v, state_ffn_conv, mem_prompt, g_mix, w_in, g_q_lora, w_q_up, g_kv_lora, w_kv_up, g_ret_norm, w_out, g_mem_q, g_mem_kv, w_mem_q, w_mem_k, w_mem_v, w_mem_o, g_ffn, w_ffn_up, w_ffn_conv, b_ffn_conv, w_ffn_down, g_final):
    depth = w_in.shape[0]
    layers = [_prep_layer(l, g_mix, w_in, g_q_lora, w_q_up, g_kv_lora, w_kv_up, g_ret_norm, w_out,
                          g_mem_q, g_mem_kv, w_mem_q, w_mem_k, w_mem_v, w_mem_o,
                          g_ffn, w_ffn_up, w_ffn_conv, b_ffn_conv, w_ffn_down) for l in range(depth)]
    gfin = g_final.reshape(1, -1).astype(F32)
    mw = MEM_HEADS * MEM_HD

    xp = x_prompt
    Bp, Tp, _ = xp.shape
    tabs_p = _rope_tables(jnp.arange(Tp))
    p_states, p_mem_k, p_mem_v = [], [], []
    p_caches = None
    for l in range(depth):
        mk, mv, mk_b, mv_b = _mem_kv(mem_prompt, layers[l])
        xp, p_caches, st = _layer(xp, tabs_p, layers[l], mk_b, mv_b, None, gfin, q0=0, layer=l, depth=depth,
                                  prev=p_caches)
        p_states.append(st)
        p_mem_k.append(mk.reshape(Bp, -1, MEM_HEADS, MEM_HD))
        p_mem_v.append(mv.reshape(Bp, -1, MEM_HEADS, MEM_HD))

    xs = x_sample
    Bs, Ts, _ = xs.shape
    past_len = cache_mla_ckv.shape[2]
    tabs_s = _rope_tables(past_len + jnp.arange(Ts))
    s_states = []
    s_caches = None
    for l in range(depth):
        past = (cache_mla_ckv[l], cache_mla_kpe[l], state_ret[l], cache_sb_k[l], cache_sb_v[l], state_ffn_conv[l])
        mk_b = cache_mem_k[l].reshape(Bs, -1, mw).astype(BF16)
        mv_b = cache_mem_v[l].reshape(Bs, -1, mw).astype(BF16)
        xs, s_caches, st = _layer(xs, tabs_s, layers[l], mk_b, mv_b, past, gfin, q0=past_len, layer=l, depth=depth,
                                  prev=s_caches)
        s_states.append(st)

    def leaves(caches, states, b, t):
        ckv, kpe, sk, sv = caches
        ret, conv = [jnp.stack(f) for f in zip(*states)]
        heads = (depth, b, t, SB_HEADS, SB_HD)
        return ckv, kpe, ret, sk.reshape(heads), sv.reshape(heads), conv

    p_ckv, p_kpe, p_ret, p_sbk, p_sbv, p_conv = leaves(p_caches, p_states, Bp, Tp)
    s_ckv, s_kpe, s_ret, s_sbk, s_sbv, s_conv = leaves(s_caches, s_states, Bs, Ts)
    return (xp, xs, p_ckv, p_kpe, p_ret, p_sbk, p_sbv, jnp.stack(p_mem_k), jnp.stack(p_mem_v), p_conv,
            s_ckv, s_kpe, s_ret, s_sbk, s_sbv, s_conv)
```

```python
import functools

import jax
import jax.numpy as jnp
from jax import lax
from jax.experimental import pallas as pl
from jax.experimental.pallas import tpu as pltpu

F32 = jnp.float32
BF16 = jnp.bfloat16

CHUNK = 64
EPS = 1e-6
ROPE_BASE = 10000.0

MLA_HEADS = 8
MLA_NOPE = 64
MLA_ROPE = 32
MLA_V = 64
MLA_Q_LORA = 256
MLA_KV_LORA = 128
MLA_SCALE = (MLA_NOPE + MLA_ROPE) ** -0.5
_LOG2E = 1.4426950408889634
_MLA_EXP2_SCALE = MLA_SCALE * _LOG2E
_SB_UNDERFLOW_LOG2 = -160.0
RET_HEADS = 4
RET_DK = 64
RET_DV = 64
RET_SCALE = RET_DK ** -0.5
SB_HEADS = 4
SB_HD = 64
SB_SCALE = SB_HD ** -0.5
MEM_HEADS = 4
MEM_HD = 128
MEM_SCALE = MEM_HD ** -0.5
D_FF = 2816
CONV_W = 3

IN_SIZES = (MLA_Q_LORA, MLA_KV_LORA, MLA_ROPE, RET_HEADS * RET_DK, RET_HEADS * RET_DK,
            RET_HEADS * RET_DV, RET_HEADS * RET_DV, SB_HEADS * SB_HD, SB_HEADS * SB_HD, SB_HEADS * SB_HD)

LANES = 128
HALF = LANES // 2
VMEM_LIMIT = 56 * 1024 * 1024

TM_PROJ = 1024
TM_KVUP = 2304
TQ_MLA = 512
TK_MLA = 512
TQ_SB = 256
TK_SB = 256
TC_RET = 1024
L_RET = 256
TM_OUT = 1024
TM_FFN = 512
TF_FFN = 256


def _tile(n, target):
    t = min(n, target)
    while n % t:
        t -= 1
    return t


def _cparams(n_axes):
    return pltpu.CompilerParams(dimension_semantics=("arbitrary",) * n_axes, vmem_limit_bytes=VMEM_LIMIT)


def _rms(x, g):
    return x * lax.rsqrt(jnp.mean(x * x, axis=-1, keepdims=True) + EPS) * g


def _lane_tile(x, reps):
    return jnp.concatenate([x] * reps, axis=1) if reps > 1 else x


def _full(shape):
    n = len(shape)
    return pl.BlockSpec(shape, lambda *_: (0,) * n)


def _dot(a, b):
    return jnp.dot(a, b, preferred_element_type=F32)


def _dot_nt(a, b):
    return lax.dot_general(a, b, (((1,), (1,)), ((), ())), preferred_element_type=F32)


_C_CQ = 0
_C_CKV = 256
_C_KPE = 384
_C_RQ = 512
_C_RK = 768
_C_RV = 1024
_C_RG = 1280
_C_SQ = 1536
_C_SK = 1792
_C_SV = 2048
_W1_COLS = 2304


def _rope(x, cos, sin_lo, sin_hi, half):
    return x * cos + pltpu.roll(x, half, 1) * sin_hi + pltpu.roll(x, LANES - half, 1) * sin_lo


def _kv_slots(c, kp, wke_ref, wv_ref, kcat_ref, v_ref):
    ck = jnp.concatenate([c, kp], axis=1)
    for p in range(MLA_HEADS // 2):
        k2 = _dot(ck, wke_ref[:, 2 * LANES * p:2 * LANES * (p + 1)])
        kcat_ref[0, 2 * p] = k2[:, :LANES].astype(BF16)
        kcat_ref[0, 2 * p + 1] = k2[:, LANES:].astype(BF16)
    for p in range(MLA_HEADS // 4):
        v2 = _dot(c, wv_ref[:, 2 * LANES * p:2 * LANES * (p + 1)])
        v_ref[0, 2 * p] = v2[:, :LANES].astype(BF16)
        v_ref[0, 2 * p + 1] = v2[:, LANES:].astype(BF16)


_IN_PROJ_INPUTS = 17


def _in_proj_body(*refs, n_prev):
    (x_ref, gmix_ref, w1_ref, gq_ref, wq_ref, gkv_ref, wke_ref, wv_ref,
     cslot_ref, slo_slot_ref, shi_slot_ref, ck_ref, slo_k_ref, shi_k_ref, cr_ref, slo_r_ref,
     shi_r_ref) = refs[:_IN_PROJ_INPUTS]
    outs = refs[_IN_PROJ_INPUTS + n_prev:]
    (ckv_ref, kpe_ref, kpeb_ref, qcat_ref, rq_ref, rk_ref, rv_ref, rg_ref,
     sq_ref, sk_ref, sv_ref, skb_ref, svb_ref) = outs[:13]
    kv_refs = outs[13:]
    h = _rms(x_ref[0], gmix_ref[...]).astype(BF16)

    def proj(c0, width):
        return _dot(h, w1_ref[:, c0:c0 + width])

    cq = proj(_C_CQ, 256)
    ckv_raw = proj(_C_CKV, 128)
    kpe_raw = proj(_C_KPE, 128)
    rq = proj(_C_RQ, 256)
    rk = proj(_C_RK, 256)
    rv = proj(_C_RV, 256)
    rg = proj(_C_RG, 256)
    sq = proj(_C_SQ, 256)
    sk = proj(_C_SK, 256)
    sv = proj(_C_SV, 256)

    ckv = _rms(ckv_raw, gkv_ref[...])
    ckv_ref[0, 0] = ckv
    kpe = _rope(kpe_raw, ck_ref[...], slo_k_ref[...], shi_k_ref[...], MLA_ROPE // 2)
    kpe_ref[0, 0] = kpe[:, :MLA_ROPE]
    kpe_b = kpe.astype(BF16)
    kpeb_ref[0] = kpe_b
    cqn = _rms(cq, gq_ref[...]).astype(BF16)
    cslot, slo_slot, shi_slot = cslot_ref[...], slo_slot_ref[...], shi_slot_ref[...]
    for p in range(MLA_HEADS // 2):
        q2 = _dot(cqn, wq_ref[:, 2 * LANES * p:2 * LANES * (p + 1)])
        for a in range(2):
            qs = q2[:, a * LANES:(a + 1) * LANES]
            qcat_ref[0, 2 * p + a] = _rope(qs, cslot, slo_slot, shi_slot, MLA_ROPE // 2).astype(BF16)
    if kv_refs:
        _kv_slots(ckv.astype(BF16), kpe_b, wke_ref, wv_ref, *kv_refs)

    cr, slo_r, shi_r = cr_ref[...], slo_r_ref[...], shi_r_ref[...]
    sk_ref[0, 0] = sk
    sv_ref[0, 0] = sv
    for p in range(2):
        sl = slice(p * LANES, (p + 1) * LANES)
        rq_ref[0, p] = _rope(rq[:, sl], cr, slo_r, shi_r, RET_DK // 2).astype(BF16)
        rk_ref[0, p] = (_rope(rk[:, sl], cr, slo_r, shi_r, RET_DK // 2) * RET_SCALE).astype(BF16)
        rv_ref[0, p] = rv[:, sl].astype(BF16)
        rg_ref[0, p] = rg[:, sl]
        sq_ref[0, p] = (sq[:, sl] * SB_SCALE).astype(BF16)
        skb_ref[0, p] = sk[:, sl].astype(BF16)
        svb_ref[0, p] = sv[:, sl].astype(BF16)


_STACKED_OUTS = (0, 1, 9, 10)


def _in_proj(x, tabs, lw, *, fuse_kv, layer, depth, prev):
    B, T, D = x.shape
    tm = _tile(T, TM_PROJ)
    grid = (B, T // tm)
    row = pl.BlockSpec((tm, LANES), lambda b, t: (t, 0))
    pair = lambda n: pl.BlockSpec((1, n, tm, LANES), lambda b, t: (b, 0, t, 0))
    nat = lambda w: pl.BlockSpec((1, tm, w), lambda b, t: (b, t, 0))
    stacked = lambda w: pl.BlockSpec((1, 1, tm, w), lambda b, t: (layer, b, t, 0))
    in_specs = [nat(D), _full((1, D)), _full((D, _W1_COLS)), _full((1, MLA_Q_LORA)),
                _full((MLA_Q_LORA, MLA_HEADS * LANES)), _full((1, MLA_KV_LORA)),
                _full((MLA_KV_LORA + LANES, MLA_HEADS * LANES)), _full((MLA_KV_LORA, MLA_HEADS * MLA_V))] + [row] * 9
    assert len(in_specs) == _IN_PROJ_INPUTS
    out_shape = [
        jax.ShapeDtypeStruct((depth, B, T, MLA_KV_LORA), F32),
        jax.ShapeDtypeStruct((depth, B, T, MLA_ROPE), F32),
        jax.ShapeDtypeStruct((B, T, LANES), BF16),
        jax.ShapeDtypeStruct((B, MLA_HEADS, T, LANES), BF16),
        jax.ShapeDtypeStruct((B, 2, T, LANES), BF16),
        jax.ShapeDtypeStruct((B, 2, T, LANES), BF16),
        jax.ShapeDtypeStruct((B, 2, T, LANES), BF16),
        jax.ShapeDtypeStruct((B, 2, T, LANES), F32),
        jax.ShapeDtypeStruct((B, 2, T, LANES), BF16),
        jax.ShapeDtypeStruct((depth, B, T, 256), F32),
        jax.ShapeDtypeStruct((depth, B, T, 256), F32),
        jax.ShapeDtypeStruct((B, 2, T, LANES), BF16),
        jax.ShapeDtypeStruct((B, 2, T, LANES), BF16),
    ]
    out_specs = [stacked(MLA_KV_LORA), stacked(MLA_ROPE), nat(LANES), pair(MLA_HEADS), pair(2), pair(2), pair(2),
                 pair(2), pair(2), stacked(256), stacked(256), pair(2), pair(2)]
    if fuse_kv:
        out_shape += [jax.ShapeDtypeStruct((B, MLA_HEADS, T, LANES), BF16),
                      jax.ShapeDtypeStruct((B, MLA_HEADS // 2, T, LANES), BF16)]
        out_specs += [pair(MLA_HEADS), pair(MLA_HEADS // 2)]
    prev = () if prev is None else tuple(prev)
    aliases = {_IN_PROJ_INPUTS + n: o for n, o in enumerate(_STACKED_OUTS[:len(prev)])}
    return pl.pallas_call(
        functools.partial(_in_proj_body, n_prev=len(prev)), grid=grid,
        in_specs=in_specs + [pl.BlockSpec(memory_space=pl.ANY)] * len(prev),
        out_specs=out_specs, out_shape=out_shape, input_output_aliases=aliases,
        compiler_params=_cparams(2), name="in_proj",
    )(x, lw["g_mix"], lw["w1"], lw["g_q"], lw["wq"], lw["g_kv"], lw["wke"], lw["wv"],
      tabs["cslot"], tabs["slo_slot"], tabs["shi_slot"], tabs["ck"], tabs["slo_k"], tabs["shi_k"],
      tabs["cr"], tabs["slo_r"], tabs["shi_r"], *prev)


def _kv_up_body(ckv_ref, kpe_ref, wke_ref, wv_ref, kcat_ref, v_ref):
    _kv_slots(ckv_ref[0].astype(BF16), kpe_ref[0], wke_ref, wv_ref, kcat_ref, v_ref)


def _kv_up(ckv, kpe_pad, lw):
    B, T, _ = ckv.shape
    tm = _tile(T, TM_KVUP)
    nat = lambda w: pl.BlockSpec((1, tm, w), lambda b, t: (b, t, 0))
    pair = lambda n: pl.BlockSpec((1, n, tm, LANES), lambda b, t: (b, 0, t, 0))
    return pl.pallas_call(
        _kv_up_body, grid=(B, T // tm),
        in_specs=[nat(MLA_KV_LORA), nat(LANES), _full((MLA_KV_LORA + LANES, MLA_HEADS * LANES)),
                  _full((MLA_KV_LORA, MLA_HEADS * MLA_V))],
        out_specs=[pair(MLA_HEADS), pair(MLA_HEADS // 2)],
        out_shape=[jax.ShapeDtypeStruct((B, MLA_HEADS, T, LANES), BF16),
                   jax.ShapeDtypeStruct((B, MLA_HEADS // 2, T, LANES), BF16)],
        compiler_params=_cparams(2), name="kv_up",
    )(ckv, kpe_pad, lw["wke"], lw["wv"])


def _mla_last_block(i, *, q0, tq, tk, nk):
    last_q = q0 + i * tq + (tq - 1)
    last_key = (last_q // CHUNK) * CHUNK + (CHUNK - 1)
    return jnp.minimum(last_key // tk, nk - 1)


def _mla_body(q_ref, k_ref, v_ref, o_ref, m_ref, l_ref, acc_ref, *, q0, tq, tk, nk, kv_len):
    i = pl.program_id(1)

    def step(j, masked, first_block):
        lane = lax.broadcasted_iota(jnp.int32, (tq, LANES), 1)
        first = lane < HALF
        rows = pl.ds(pl.multiple_of(j * tk, tk), tk)
        if masked:
            qpos = q0 + i * tq + lax.broadcasted_iota(jnp.int32, (tq, 1), 0)
            limit = jnp.minimum((jnp.right_shift(qpos, 6) + 1) * CHUNK, kv_len)
            kpos = j * tk + lax.broadcasted_iota(jnp.int32, (tq, tk), 1)
            bias = jnp.where(kpos < limit, 0.0, -jnp.inf).astype(F32)

        def scores(hh):
            s = _dot_nt(q_ref[0, hh], k_ref[0, hh, rows, :])
            return s + bias if masked else s

        s_next = scores(0)
        alphas, pvs = [], []
        for hh in range(MLA_HEADS):
            p, a = divmod(hh, 2)
            s = s_next
            if hh + 1 < MLA_HEADS:
                s_next = scores(hh + 1)
            m_cur = jnp.max(s, axis=1, keepdims=True)
            if first_block:
                m_new = jnp.broadcast_to(m_cur, (tq, LANES))
                e = jnp.exp2((s - m_cur) * _MLA_EXP2_SCALE)
                l_ref[hh] = jnp.broadcast_to(jnp.sum(e, axis=1, keepdims=True), (tq, LANES))
            else:
                m_prev = m_ref[hh]
                m_new = jnp.maximum(m_prev, m_cur)
                e = jnp.exp2((s - _lane_tile(m_new, tk // LANES)) * _MLA_EXP2_SCALE)
                alpha = jnp.exp2((m_prev - m_new) * _MLA_EXP2_SCALE)
                l_ref[hh] = alpha * l_ref[hh] + jnp.sum(e, axis=1, keepdims=True)
                alphas.append(alpha)
            m_ref[hh] = m_new
            pvs.append(_dot(e.astype(BF16), v_ref[0, p, rows, :]))
            if a == 1:
                new = jnp.where(first, pvs[0], pvs[1])
                if first_block:
                    acc_ref[p] = new
                else:
                    acc_ref[p] = acc_ref[p] * jnp.where(first, alphas[0], alphas[1]) + new
                alphas, pvs = [], []

    all_visible = jnp.minimum(((q0 + i * tq) // CHUNK + 1) * CHUNK, kv_len)
    last = _mla_last_block(i, q0=q0, tq=tq, tk=tk, nk=nk)
    n_full = jnp.minimum(all_visible // tk, last + 1)

    @pl.when(n_full > 0)
    def _():
        step(0, False, True)

    @pl.when(n_full == 0)
    def _():
        step(0, True, True)

    def full_step(j, carry):
        step(j, False, False)
        return carry

    def masked_step(j, carry):
        step(j, True, False)
        return carry

    lax.fori_loop(1, n_full, full_step, 0)
    lax.fori_loop(jnp.maximum(n_full, 1), last + 1, masked_step, 0)

    lane = lax.broadcasted_iota(jnp.int32, (tq, LANES), 1)
    for p in range(MLA_HEADS // 2):
        inv = jnp.where(lane < HALF, 1.0 / l_ref[2 * p], 1.0 / l_ref[2 * p + 1])
        o_ref[0, p] = (acc_ref[p] * inv).astype(BF16)


def _mla_attention(qcat, kcat, v, *, q0, kv_len, tq, tk):
    B, H, Tq, _ = qcat.shape
    Tk = kcat.shape[2]
    nq, nk = Tq // tq, Tk // tk
    return pl.pallas_call(
        functools.partial(_mla_body, q0=q0, tq=tq, tk=tk, nk=nk, kv_len=kv_len),
        grid=(B, nq),
        in_specs=[pl.BlockSpec((1, H, tq, LANES), lambda b, i: (b, 0, i, 0)),
                  pl.BlockSpec((1, H, Tk, LANES), lambda b, i: (b, 0, 0, 0)),
                  pl.BlockSpec((1, H // 2, Tk, LANES), lambda b, i: (b, 0, 0, 0))],
        out_specs=pl.BlockSpec((1, H // 2, tq, LANES), lambda b, i: (b, 0, i, 0)),
        out_shape=jax.ShapeDtypeStruct((B, H // 2, Tq, LANES), BF16),
        scratch_shapes=[pltpu.VMEM((H, tq, LANES), F32), pltpu.VMEM((H, tq, LANES), F32),
                        pltpu.VMEM((H // 2, tq, LANES), F32)],
        compiler_params=_cparams(2), name="mla_attn",
    )(qcat, kcat, v)


def _sb_last_block(i, *, q0, tq, tk, nk):
    last_key = q0 + i * tq + (tq - 1) - 1
    return jnp.clip(last_key // tk, 0, nk - 1)


def _sb_body(q_ref, k_ref, v_ref, u_ref, o_ref, run_ref, acc_ref, *, q0, tq, tk, nk):
    i = pl.program_id(1)
    q_first = q0 + i * tq
    last = _sb_last_block(i, q0=q0, tq=tq, tk=tk, nk=nk)
    first_unmasked = jnp.minimum(q_first // tk - 1, last)

    def step(kb, masked, first_block=False):
        first_k = lax.broadcasted_iota(jnp.int32, (tk, LANES), 1) < HALF
        tri = u_ref[...]
        reps = tk // LANES
        rows = pl.ds(pl.multiple_of(kb * tk, tk), tk)
        if masked:
            qpos = q_first + lax.broadcasted_iota(jnp.int32, (tq, 1), 0)
            kpos = kb * tk + lax.broadcasted_iota(jnp.int32, (tq, tk), 1)
            before1 = kpos < qpos
            before = jnp.concatenate([before1, before1], axis=1)

        def scores(p):
            kp = k_ref[0, p, rows, :]
            zero = jnp.zeros_like(kp)
            k2 = jnp.concatenate([jnp.where(first_k, kp, zero), jnp.where(first_k, zero, kp)], axis=0)
            return _dot_nt(q_ref[0, p], k2) * _LOG2E

        npair = SB_HEADS // 2
        zs = [scores(p) for p in range(npair)]
        keeps, betas, betweens = [], [], []
        for p in range(npair):
            z = zs[p]
            nz = -z
            sp = jnp.log(1.0 + jnp.exp2(jnp.minimum(z, nz))) * _LOG2E
            log_keep = jnp.minimum(nz, 0.0) - sp
            betas.append(log_keep + z)
            if masked:
                log_keep = jnp.where(before, log_keep, 0.0)
            keeps.append(log_keep)
            hi = log_keep.astype(BF16)
            lo = (log_keep - hi.astype(F32)).astype(BF16)
            betweens.append(jnp.concatenate(
                [_dot(hi[:, :tk], tri) + _dot(lo[:, :tk], tri), _dot(hi[:, tk:], tri) + _dot(lo[:, tk:], tri)], axis=1))
        for p in range(npair):
            vp = v_ref[0, p, rows, :]
            zero = jnp.zeros_like(vp)
            v2 = jnp.concatenate([jnp.where(first_k, vp, zero), jnp.where(first_k, zero, vp)], axis=0)
            sum_a = jnp.sum(keeps[p][:, :tk], axis=1, keepdims=True)
            sum_b = jnp.sum(keeps[p][:, tk:], axis=1, keepdims=True)
            if first_block:
                att = jnp.exp2(betas[p] + betweens[p])
                run_ref[2 * p] = jnp.broadcast_to(sum_a, (tq, LANES))
                run_ref[2 * p + 1] = jnp.broadcast_to(sum_b, (tq, LANES))
            else:
                run_a = run_ref[2 * p]
                run_b = run_ref[2 * p + 1]
                run = jnp.concatenate([_lane_tile(run_a, reps), _lane_tile(run_b, reps)], axis=1)
                att = jnp.exp2(betas[p] + betweens[p] + run)
                run_ref[2 * p] = run_a + sum_a
                run_ref[2 * p + 1] = run_b + sum_b
            if masked:
                att = jnp.where(before, att, 0.0)
            pv = _dot(att.astype(BF16), v2)
            acc_ref[p] = pv if first_block else acc_ref[p] + pv

    step(last, True, first_block=True)

    def masked_step(t, carry):
        step(last - t, True)
        return carry

    lax.fori_loop(1, last - first_unmasked, masked_step, 0)

    def more(carry):
        kb, live = carry
        return jnp.logical_and(kb >= 0, live > 0)

    def unmasked_step(carry):
        kb, _ = carry
        step(kb, False)
        live = (jnp.max(run_ref[...]) >= _SB_UNDERFLOW_LOG2).astype(jnp.int32)
        return kb - 1, live

    lax.while_loop(more, unmasked_step, (jnp.minimum(first_unmasked, last - 1), jnp.int32(1)))
    o_ref[0] = acc_ref[...].astype(BF16)


def _sb_attention(q, k, v, *, q0, tq, tk):
    B, P, Tq, _ = q.shape
    Tk = k.shape[2]
    nq, nk = Tq // tq, Tk // tk
    kw = dict(q0=q0, tq=tq, tk=tk, nk=nk)
    r = lax.broadcasted_iota(jnp.int32, (tk, tk), 0)
    c = lax.broadcasted_iota(jnp.int32, (tk, tk), 1)
    tri = (r > c).astype(BF16)

    kv_spec = pl.BlockSpec((1, P, Tk, LANES), lambda b, i: (b, 0, 0, 0))
    return pl.pallas_call(
        functools.partial(_sb_body, **kw),
        grid=(B, nq),
        in_specs=[pl.BlockSpec((1, P, tq, LANES), lambda b, i: (b, 0, i, 0)), kv_spec, kv_spec, _full((tk, tk))],
        out_specs=pl.BlockSpec((1, P, tq, LANES), lambda b, i: (b, 0, i, 0)),
        out_shape=jax.ShapeDtypeStruct((B, P, Tq, LANES), BF16),
        scratch_shapes=[pltpu.VMEM((2 * P, tq, LANES), F32), pltpu.VMEM((P, tq, LANES), F32)],
        compiler_params=_cparams(2), name="sb_attn",
    )(q, k, v, tri)


def _ret_body(q_ref, k_ref, v_ref, g_ref, s0_ref, dmask_ref, dq_ref, dk_ref, cd_ref, bd_ref, gn_ref,
              y_ref, snew_ref, state_ref, *, tc, lc, nc):
    c = pl.program_id(1)

    @pl.when(c == 0)
    def _():
        state_ref[...] = s0_ref[0]

    lane = lax.broadcasted_iota(jnp.int32, (lc, LANES), 1)
    first = lane < HALF
    pairs = range(RET_HEADS // 2)
    for ch in range(tc // lc):
        rows = slice(ch * lc, (ch + 1) * lc)
        q = [q_ref[0, p, rows, :] for p in pairs]
        k = [k_ref[0, p, rows, :] for p in pairs]
        v = [v_ref[0, p, rows, :] for p in pairs]
        state = [state_ref[p] for p in pairs]
        cross = [_dot((q[p].astype(F32) * dq_ref[p]).astype(BF16), state[p].astype(BF16)) for p in pairs]
        scores = []
        for p in pairs:
            zero = jnp.zeros_like(q[p])
            scores.append([_dot_nt(jnp.where(first, q[p], zero), k[p]) * dmask_ref[2 * p],
                           _dot_nt(jnp.where(first, zero, q[p]), k[p]) * dmask_ref[2 * p + 1]])
        kv = [lax.dot_general((k[p].astype(F32) * dk_ref[p]).astype(BF16), v[p], (((0,), (0,)), ((), ())),
                              preferred_element_type=F32) for p in pairs]
        inner = [[_dot(sc.astype(BF16), v[p]) for sc in scores[p]] for p in pairs]
        for p in pairs:
            state_ref[p] = cd_ref[p] * state[p] + bd_ref[...] * kv[p]
            o = jnp.where(first, inner[p][0], inner[p][1]) + cross[p]
            zf = jnp.zeros_like(o)
            mu = jnp.where(first, jnp.sum(jnp.where(first, o, zf), axis=1, keepdims=True),
                           jnp.sum(jnp.where(first, zf, o), axis=1, keepdims=True)) * (1.0 / RET_DV)
            d = o - mu
            d2 = d * d
            var = jnp.where(first, jnp.sum(jnp.where(first, d2, zf), axis=1, keepdims=True),
                            jnp.sum(jnp.where(first, zf, d2), axis=1, keepdims=True)) * (1.0 / RET_DV)
            yn = d * lax.rsqrt(var + EPS) * gn_ref[p]
            gate = g_ref[0, p, rows, :]
            y_ref[0, p, rows, :] = (yn * (gate * jax.nn.sigmoid(gate))).astype(BF16)

    @pl.when(c == nc - 1)
    def _():
        snew_ref[0] = state_ref[...]


def _ret_tables(lc):
    log_gamma = jnp.log(1.0 - 2.0 ** (-5.0 - jnp.arange(RET_HEADS, dtype=F32)))
    idx = jnp.arange(lc, dtype=F32)
    diff = idx[:, None] - idx[None, :]
    dmask = jnp.where(diff[None] >= 0, jnp.exp(jnp.maximum(diff, 0.0)[None] * log_gamma[:, None, None]), 0.0)
    dk = jnp.exp((lc - 1.0 - idx)[:, None] * log_gamma[None, :])
    dq = jnp.exp((idx + 1.0)[:, None] * log_gamma[None, :])
    cd = jnp.exp(lc * log_gamma)

    def lanes(t):
        t = jnp.repeat(t[:, :, None], HALF, axis=2).reshape(t.shape[0], RET_HEADS // 2, LANES)
        return jnp.transpose(t, (1, 0, 2))

    blk = jnp.arange(LANES) // HALF
    bd = (blk[:, None] == blk[None, :]).astype(F32)
    cd_rows = jnp.repeat(cd, HALF).reshape(RET_HEADS // 2, LANES, 1)
    return dmask, lanes(dq), lanes(dk), cd_rows * bd[None], bd


def _state_to_pairs(s):
    B = s.shape[0]
    s = s.astype(F32).reshape(B, RET_HEADS // 2, 2, RET_DK, RET_DV)
    z = jnp.zeros_like(s[:, :, 0])
    top = jnp.concatenate([s[:, :, 0], z], axis=-1)
    bot = jnp.concatenate([z, s[:, :, 1]], axis=-1)
    return jnp.concatenate([top, bot], axis=-2)


def _pairs_to_state(sp):
    a = sp[:, :, :HALF, :HALF]
    b = sp[:, :, HALF:, HALF:]
    B = sp.shape[0]
    return jnp.stack([a, b], axis=2).reshape(B, RET_HEADS, RET_DK, RET_DV)


def _retention(rq, rk, rv, rg, s0_pairs, g_norm_pairs, *, lc):
    B, P, T, _ = rq.shape
    tc = _tile(T, TC_RET)
    nc = T // tc
    dmask, dq, dk, cd, bd = _ret_tables(lc)
    pair = pl.BlockSpec((1, P, tc, LANES), lambda b, c: (b, 0, c, 0))
    st = pl.BlockSpec((1, P, LANES, LANES), lambda b, c: (b, 0, 0, 0))
    return pl.pallas_call(
        functools.partial(_ret_body, tc=tc, lc=lc, nc=nc),
        grid=(B, nc),
        in_specs=[pair, pair, pair, pair, st, _full(dmask.shape), _full(dq.shape), _full(dk.shape),
                  _full(cd.shape), _full(bd.shape), _full(g_norm_pairs.shape)],
        out_specs=[pair, st],
        out_shape=[jax.ShapeDtypeStruct((B, P, T, LANES), BF16),
                   jax.ShapeDtypeStruct((B, P, LANES, LANES), F32)],
        scratch_shapes=[pltpu.VMEM((P, LANES, LANES), F32)],
        compiler_params=_cparams(2), name="retention",
    )(rq, rk, rv, rg, s0_pairs, dmask, dq, dk, cd, bd, g_norm_pairs)


def _out_mem_body(x_ref, omla_ref, ret_ref, osb_ref, wout_ref, gq_ref, wq_ref, mk_ref, mv_ref, wo_ref, y_ref):
    mixed = jnp.concatenate([ref[0, p] for ref, n in ((omla_ref, MLA_HEADS // 2), (ret_ref, RET_HEADS // 2),
                                                      (osb_ref, SB_HEADS // 2)) for p in range(n)], axis=1)
    x1 = x_ref[0] + _dot(mixed, wout_ref[...])
    hq = _rms(x1, gq_ref[...]).astype(BF16)
    q = _dot(hq, wq_ref[...]).astype(BF16)
    cols = [slice(hh * MEM_HD, (hh + 1) * MEM_HD) for hh in range(MEM_HEADS)]
    scores = [_dot_nt(q[:, sl], mk_ref[0, :, sl]) * MEM_SCALE for sl in cols]
    probs = []
    for s in scores:
        e = jnp.exp(s - jnp.max(s, axis=1, keepdims=True))
        probs.append((e / jnp.sum(e, axis=1, keepdims=True)).astype(BF16))
    heads = [_dot(pr, mv_ref[0, :, sl]).astype(BF16) for pr, sl in zip(probs, cols)]
    o = jnp.concatenate(heads, axis=1)
    y_ref[0] = x1 + _dot(o, wo_ref[...])


def _out_mem(x, o_mla, ret_y, o_sb, mem_k, mem_v, lw):
    B, T, D = x.shape
    tm = _tile(T, TM_OUT)
    M = mem_k.shape[1]
    mw = MEM_HEADS * MEM_HD
    nat = pl.BlockSpec((1, tm, D), lambda b, t: (b, t, 0))
    pair = lambda n: pl.BlockSpec((1, n, tm, LANES), lambda b, t: (b, 0, t, 0))
    mem = pl.BlockSpec((1, M, mw), lambda b, t: (b, 0, 0))
    return pl.pallas_call(
        _out_mem_body, grid=(B, T // tm),
        in_specs=[nat, pair(MLA_HEADS // 2), pair(RET_HEADS // 2), pair(SB_HEADS // 2), _full((D, D)),
                  _full((1, D)), _full((D, mw)), mem, mem, _full((mw, D))],
        out_specs=nat, out_shape=jax.ShapeDtypeStruct((B, T, D), F32),
        compiler_params=_cparams(2), name="out_mem",
    )(x, o_mla, ret_y, o_sb, lw["w_out"], lw["g_mem_q"], lw["w_mem_q"], mem_k, mem_v, lw["w_mem_o"])


def _mem_kv_body(mem_ref, g_ref, wk_ref, wv_ref, k_ref, v_ref, kb_ref, vb_ref):
    m = _rms(mem_ref[0], g_ref[...]).astype(BF16)
    k = _dot(m, wk_ref[...])
    v = _dot(m, wv_ref[...])
    k_ref[0] = k
    v_ref[0] = v
    kb_ref[0] = k.astype(BF16)
    vb_ref[0] = v.astype(BF16)


def _mem_kv(mem, lw):
    B, M, D = mem.shape
    mw = MEM_HEADS * MEM_HD
    o = pl.BlockSpec((1, M, mw), lambda b: (b, 0, 0))
    return pl.pallas_call(
        _mem_kv_body, grid=(B,),
        in_specs=[pl.BlockSpec((1, M, D), lambda b: (b, 0, 0)), _full((1, D)), _full((D, mw)), _full((D, mw))],
        out_specs=[o, o, o, o],
        out_shape=[jax.ShapeDtypeStruct((B, M, mw), F32), jax.ShapeDtypeStruct((B, M, mw), F32),
                   jax.ShapeDtypeStruct((B, M, mw), BF16), jax.ShapeDtypeStruct((B, M, mw), BF16)],
        compiler_params=_cparams(1), name="mem_kv",
    )(mem, lw["g_mem_kv"], lw["w_mem_k"], lw["w_mem_v"])


_CARRY_ROW = 8 - (CONV_W - 1)


def _ffn_body(x_ref, g_ref, wup_ref, wconv_ref, bconv_ref, wdown_ref, conv0_ref, gfin_ref,
              y_ref, convnew_ref, carry_ref, act_ref, *, tm, nt, final):
    t = pl.program_id(1)

    @pl.when(t == 0)
    def _():
        carry_ref[_CARRY_ROW:8, :] = conv0_ref[0]

    x = x_ref[0]
    h = _rms(x, g_ref[...]).astype(BF16)

    sub = lax.broadcasted_iota(jnp.int32, (8, TF_FFN), 0)

    def delay(u, before_first):
        r = pltpu.roll(u, 1, 0)
        head = jnp.where(sub == 0, before_first, r[0:8, :])
        return jnp.concatenate([head, r[8:, :]], axis=0) if tm > 8 else head

    def conv_part(c0):
        cols = slice(c0, c0 + TF_FFN)
        u = _dot(h, wup_ref[:, cols])
        u1 = delay(u, carry_ref[_CARRY_ROW + 1:_CARRY_ROW + 2, cols])
        u2 = delay(u1, carry_ref[_CARRY_ROW:_CARRY_ROW + 1, cols])
        carry_ref[_CARRY_ROW:8, cols] = u[tm - (CONV_W - 1):, :]
        w = wconv_ref[:, cols]
        c = bconv_ref[:, cols]
        c = c + u2 * w[0:1]
        c = c + u1 * w[1:2]
        return c + u * w[2:3]

    for ci in range(D_FF // TF_FFN):
        c0 = ci * TF_FFN
        a = conv_part(c0)
        b = conv_part(D_FF + c0)
        act_ref[:, c0:c0 + TF_FFN] = (a * jax.nn.sigmoid(a) * b).astype(BF16)

    x3 = x + _dot(act_ref[...], wdown_ref[...])
    y_ref[0] = _rms(x3, gfin_ref[...]) if final else x3

    @pl.when(t == nt - 1)
    def _():
        convnew_ref[0] = carry_ref[_CARRY_ROW:8, :]


def _conv_ffn(x, conv0, lw, g_final, *, final):
    B, T, D = x.shape
    tm = _tile(T, TM_FFN)
    nt = T // tm
    f2 = 2 * D_FF
    nat = pl.BlockSpec((1, tm, D), lambda b, t: (b, t, 0))
    cv = pl.BlockSpec((1, CONV_W - 1, f2), lambda b, t: (b, 0, 0))
    return pl.pallas_call(
        functools.partial(_ffn_body, tm=tm, nt=nt, final=final),
        grid=(B, nt),
        in_specs=[nat, _full((1, D)), _full((D, f2)), _full((CONV_W, f2)), _full((1, f2)), _full((D_FF, D)),
                  cv, _full((1, D))],
        out_specs=[nat, cv],
        out_shape=[jax.ShapeDtypeStruct((B, T, D), F32), jax.ShapeDtypeStruct((B, CONV_W - 1, f2), F32)],
        scratch_shapes=[pltpu.VMEM((8, f2), F32), pltpu.VMEM((tm, D_FF), BF16)],
        compiler_params=_cparams(2), name="conv_ffn",
    )(x, lw["g_ffn"], lw["w_ffn_up"], lw["w_ffn_conv"], lw["b_ffn_conv"], lw["w_ffn_down"], conv0, g_final)


def _pad_cols(w, width):
    return jnp.pad(w, ((0, 0), (0, width - w.shape[1])))


def _prep_layer(l, g_mix, w_in, g_q_lora, w_q_up, g_kv_lora, w_kv_up, g_ret_norm, w_out,
                g_mem_q, g_mem_kv, w_mem_q, w_mem_k, w_mem_v, w_mem_o,
                g_ffn, w_ffn_up, w_ffn_conv, b_ffn_conv, w_ffn_down):
    w = w_in[l]
    parts, c0 = [], 0
    for n in IN_SIZES:
        parts.append(w[:, c0:c0 + n])
        c0 += n
    cq, ckv, kpe, rq, rk, rv, rg, sq, sk, sv = parts
    w1 = jnp.concatenate([cq, ckv, _pad_cols(kpe, LANES), rq, rk, rv, rg, sq, sk, sv], axis=1).astype(BF16)
    kq = w_q_up.shape[1]
    wq3 = w_q_up[l].reshape(kq, MLA_HEADS, MLA_NOPE + MLA_ROPE)
    tail = LANES - MLA_NOPE - MLA_ROPE
    q_pad = jnp.pad(wq3, ((0, 0), (0, 0), (0, tail))).reshape(kq, MLA_HEADS * LANES)
    wkv3 = w_kv_up[l].reshape(MLA_KV_LORA, MLA_HEADS, MLA_NOPE + MLA_V)
    wk = jnp.pad(wkv3[:, :, :MLA_NOPE], ((0, 0), (0, 0), (0, LANES - MLA_NOPE))).reshape(MLA_KV_LORA, MLA_HEADS * LANES)
    wv = wkv3[:, :, MLA_NOPE:].reshape(MLA_KV_LORA, MLA_HEADS * MLA_V)
    r = jnp.arange(LANES)[:, None]
    c = jnp.arange(MLA_HEADS * LANES)[None, :]
    we = ((r < MLA_ROPE) & (c % LANES == r + MLA_NOPE)).astype(BF16)
    row = lambda g: g[l].reshape(1, -1).astype(F32)
    return dict(
        g_mix=row(g_mix), w1=w1, g_q=row(g_q_lora), wq=q_pad.astype(BF16),
        g_kv=row(g_kv_lora), wke=jnp.concatenate([wk.astype(BF16), we], axis=0), wv=wv.astype(BF16),
        g_ret=g_ret_norm[l].astype(F32).reshape(RET_HEADS // 2, 1, LANES),
        w_out=w_out[l].astype(BF16), g_mem_q=row(g_mem_q), g_mem_kv=row(g_mem_kv),
        w_mem_q=w_mem_q[l].astype(BF16), w_mem_k=w_mem_k[l].astype(BF16), w_mem_v=w_mem_v[l].astype(BF16),
        w_mem_o=w_mem_o[l].astype(BF16), g_ffn=row(g_ffn), w_ffn_up=w_ffn_up[l].astype(BF16),
        w_ffn_conv=w_ffn_conv[l].astype(F32), b_ffn_conv=row(b_ffn_conv), w_ffn_down=w_ffn_down[l].astype(BF16))


def _rope_tables(pos):
    t = pos.shape[0]
    posf = pos.astype(F32)[:, None]

    def cos_sin(d):
        inv = ROPE_BASE ** (-jnp.arange(0, d, 2, dtype=F32) / d)
        ang = posf * inv[None, :]
        return jnp.cos(ang), jnp.sin(ang)

    c16, s16 = cos_sin(MLA_ROPE)
    c32, s32 = cos_sin(RET_DK)
    z = lambda n: jnp.zeros((t, n), F32)
    tail = LANES - MLA_NOPE - MLA_ROPE
    return dict(
        cslot=jnp.concatenate([jnp.ones((t, MLA_NOPE), F32), c16, c16, z(tail)], axis=1),
        slo_slot=jnp.concatenate([z(MLA_NOPE), -s16, z(MLA_ROPE // 2 + tail)], axis=1),
        shi_slot=jnp.concatenate([z(MLA_NOPE + MLA_ROPE // 2), s16, z(tail)], axis=1),
        ck=jnp.concatenate([c16, c16, z(LANES - MLA_ROPE)], axis=1),
        slo_k=jnp.concatenate([-s16, z(LANES - MLA_ROPE // 2)], axis=1),
        shi_k=jnp.concatenate([z(MLA_ROPE // 2), s16, z(LANES - MLA_ROPE)], axis=1),
        cr=jnp.tile(jnp.concatenate([c32, c32], axis=1), (1, LANES // RET_DK)),
        slo_r=jnp.tile(jnp.concatenate([-s32, z(RET_DK // 2)], axis=1), (1, LANES // RET_DK)),
        shi_r=jnp.tile(jnp.concatenate([z(RET_DK // 2), s32], axis=1), (1, LANES // RET_DK)))


def _pad_rows(a, axis, n):
    pad = [(0, 0)] * a.ndim
    pad[axis] = (0, n - a.shape[axis])
    return jnp.pad(a, pad)


def _to_pairs(a):
    B, T, _ = a.shape
    return jnp.transpose(a.reshape(B, T, 2, LANES), (0, 2, 1, 3))


def _layer(x, tabs, lw, mem_k, mem_v, past, g_final, *, q0, layer, depth, prev):
    B, T, _ = x.shape
    final = layer == depth - 1
    outs = _in_proj(x, tabs, lw, fuse_kv=past is None, layer=layer, depth=depth, prev=prev)
    (ckv_st, kpe_st, kpe_b, qcat, rq, rk, rv, rg, sq, sk_st, sv_st, sk_b, sv_b) = outs[:13]
    if past is None:
        kcat, vmla = outs[13:]
        sk_all, sv_all = sk_b, sv_b
        s0 = jnp.zeros((B, RET_HEADS // 2, LANES, LANES), F32)
        conv0 = jnp.zeros((B, CONV_W - 1, 2 * D_FF), F32)
        kv_len = T
        lc = _tile(T, L_RET)
        tq_mla, tk_mla = _tile(T, TQ_MLA), _tile(T, TK_MLA)
        tq_sb, tk_sb = _tile(T, TQ_SB), _tile(T, TK_SB)
    else:
        ckv_c, kpe_c, s0, sk_c, sv_c, conv0 = past
        P = ckv_c.shape[1]
        kv_len = P + T
        tk_sb = TK_SB
        tk_pad = -(-kv_len // tk_sb) * tk_sb
        ckv_all = _pad_rows(jnp.concatenate([ckv_c.astype(F32), ckv_st[layer]], axis=1), 1, tk_pad)
        kpe_cb = _pad_cols(kpe_c.reshape(B * P, MLA_ROPE), LANES).reshape(B, P, LANES).astype(BF16)
        kpe_all = _pad_rows(jnp.concatenate([kpe_cb, kpe_b], axis=1), 1, tk_pad)
        sk_all = _pad_rows(jnp.concatenate([_to_pairs(sk_c.reshape(B, P, -1).astype(BF16)), sk_b], axis=2), 2, tk_pad)
        sv_all = _pad_rows(jnp.concatenate([_to_pairs(sv_c.reshape(B, P, -1).astype(BF16)), sv_b], axis=2), 2, tk_pad)
        s0 = _state_to_pairs(s0)
        conv0 = conv0.astype(F32)
        lc = T
        tq_mla, tk_mla = T, tk_pad
        tq_sb = T
        kcat, vmla = _kv_up(ckv_all, kpe_all, lw)
    o_mla = _mla_attention(qcat, kcat, vmla, q0=q0, kv_len=kv_len, tq=tq_mla, tk=tk_mla)
    o_sb = _sb_attention(sq, sk_all, sv_all, q0=q0, tq=tq_sb, tk=tk_sb)
    ret_y, s_new = _retention(rq, rk, rv, rg, s0, lw["g_ret"], lc=lc)
    x = _out_mem(x, o_mla, ret_y, o_sb, mem_k, mem_v, lw)
    x, conv_new = _conv_ffn(x, conv0, lw, g_final, final=final)
    return x, (ckv_st, kpe_st, sk_st, sv_st), (_pairs_to_state(s_new), conv_new)


def kernel(x_prompt, x_sample, cache_mla_ckv, cache_mla_kpe, state_ret, cache_sb_k, cache_sb_v, cache_mem_k, cache_mem_v, state_ffn_conv, mem_prompt, g_mix, w_in, g_q_lora, w_q_up, g_kv_lora, w_kv_up, g_ret_norm, w_out, g_mem_q, g_mem_kv, w_mem_q, w_mem_k, w_mem_v, w_mem_o, g_ffn, w_ffn_up, w_ffn_conv, b_ffn_conv, w_ffn_down, g_final):
    depth = w_in.shape[0]
    layers = [_prep_layer(l, g_mix, w_in, g_q_lora, w_q_up, g_kv_lora, w_kv_up, g_ret_norm, w_out,
                          g_mem_q, g_mem_kv, w_mem_q, w_mem_k, w_mem_v, w_mem_o,
                          g_ffn, w_ffn_up, w_ffn_conv, b_ffn_conv, w_ffn_down) for l in range(depth)]
    gfin = g_final.reshape(1, -1).astype(F32)
    mw = MEM_HEADS * MEM_HD

    xp = x_prompt
    Bp, Tp, _ = xp.shape
    tabs_p = _rope_tables(jnp.arange(Tp))
    p_states, p_mem_k, p_mem_v = [], [], []
    p_caches = None
    for l in range(depth):
        mk, mv, mk_b, mv_b = _mem_kv(mem_prompt, layers[l])
        xp, p_caches, st = _layer(xp, tabs_p, layers[l], mk_b, mv_b, None, gfin, q0=0, layer=l, depth=depth,
                                  prev=p_caches)
        p_states.append(st)
        p_mem_k.append(mk.reshape(Bp, -1, MEM_HEADS, MEM_HD))
        p_mem_v.append(mv.reshape(Bp, -1, MEM_HEADS, MEM_HD))

    xs = x_sample
    Bs, Ts, _ = xs.shape
    past_len = cache_mla_ckv.shape[2]
    tabs_s = _rope_tables(past_len + jnp.arange(Ts))
    s_states = []
    s_caches = None
    for l in range(depth):
        past = (cache_mla_ckv[l], cache_mla_kpe[l], state_ret[l], cache_sb_k[l], cache_sb_v[l], state_ffn_conv[l])
        mk_b = cache_mem_k[l].reshape(Bs, -1, mw).astype(BF16)
        mv_b = cache_mem_v[l].reshape(Bs, -1, mw).astype(BF16)
        xs, s_caches, st = _layer(xs, tabs_s, layers[l], mk_b, mv_b, past, gfin, q0=past_len, layer=l, depth=depth,
                                  prev=s_caches)
        s_states.append(st)

    def leaves(caches, states, b, t):
        ckv, kpe, sk, sv = caches
        ret, conv = [jnp.stack(f) for f in zip(*states)]
        heads = (depth, b, t, SB_HEADS, SB_HD)
        return ckv, kpe, ret, sk.reshape(heads), sv.reshape(heads), conv

    p_ckv, p_kpe, p_ret, p_sbk, p_sbv, p_conv = leaves(p_caches, p_states, Bp, Tp)
    s_ckv, s_kpe, s_ret, s_sbk, s_sbv, s_conv = leaves(s_caches, s_states, Bs, Ts)
    return (xp, xs, p_ckv, p_kpe, p_ret, p_sbk, p_sbv, jnp.stack(p_mem_k), jnp.stack(p_mem_v), p_conv,
            s_ckv, s_kpe, s_ret, s_sbk, s_sbv, s_conv)
```
